```python
import math
import jax, jax.numpy as jnp
from jax import lax
import numpy as np

D_MODEL = 1024
BATCH = 2
SEQ = 8192
DEPTH = 1

GRID_W = 64
CTX_LEN = 256
D_SSM = D_MODEL // 2
SSM_GROUP_CH = 16
SSM_GROUPS = D_SSM // SSM_GROUP_CH
SSM_STATE = 64
D_POOL = D_MODEL // 2
POOL_WINDOWS = (2, 4, 8, 16)
POOL_GROUP_CH = D_POOL // len(POOL_WINDOWS)
FFN_HIDDEN = ((8 * D_MODEL // 3 + 255) // 256) * 256
RMS_EPS = 1e-6
DT_MIN = 1e-3
DT_MAX = 1e-1

kernel_name = "hybrid_s5_pool_prefix_dit_block"


def rms_norm(x, g):
    x32 = x.astype(jnp.float32)
    y = x32 * lax.rsqrt(jnp.mean(x32 * x32, axis=-1, keepdims=True) + RMS_EPS)
    return (y * g.astype(jnp.float32)).astype(x.dtype)


def modulate(h, shift, scale):
    return h * (1.0 + scale) + shift


def s5_discretize(a_re, a_im, log_dt, b_re, b_im):
    A = lax.complex(a_re.astype(jnp.float32), a_im.astype(jnp.float32))
    dt = jnp.exp(log_dt.astype(jnp.float32))[:, None]
    a_bar = jnp.exp(A * dt)
    B = lax.complex(b_re.astype(jnp.float32), b_im.astype(jnp.float32))
    b_bar = ((a_bar - 1.0) / A)[..., None] * B
    return a_bar, b_bar


def _lin_rec(e1, e2):
    a1, b1 = e1
    a2, b2 = e2
    return a1 * a2, a2 * b1 + b2


def s5_states(u, a_bar, b_bar, s0, reverse):
    bsz, length, _ = u.shape
    ug = u.astype(jnp.float32).reshape(bsz, length, SSM_GROUPS, SSM_GROUP_CH).astype(jnp.complex64)
    bu = jnp.einsum('gpc,blgc->blgp', b_bar, ug)
    if s0 is not None:
        idx = length - 1 if reverse else 0
        bu = bu.at[:, idx].add(a_bar[None] * s0)
    a = jnp.broadcast_to(a_bar, (1, length) + a_bar.shape)
    _, states = lax.associative_scan(_lin_rec, (a, bu), reverse=reverse, axis=1)
    return states


def s5_readout(u_a, st_f, st_b, c_f, c_b, d_skip, w_glu, b_glu):
    bsz, length, _ = u_a.shape
    y = (jnp.einsum('gcp,blgp->blgc', c_f, st_f) + jnp.einsum('gcp,blgp->blgc', c_b, st_b)).real
    y = y.reshape(bsz, length, D_SSM) + d_skip.astype(jnp.float32) * u_a.astype(jnp.float32)
    y = jax.nn.gelu(y)
    return y * jax.nn.sigmoid(y @ w_glu.astype(jnp.float32) + b_glu.astype(jnp.float32))


def pool_mixer(u, pool_w, pool_scale):
    width = u.shape[2]
    pos = jnp.arange(width)
    u32 = u.astype(jnp.float32)
    cs = jnp.pad(jnp.cumsum(u32, axis=2), ((0, 0), (0, 0), (1, 0), (0, 0)))
    outs = []
    for j, w in enumerate(POOL_WINDOWS):
        sl = slice(j * POOL_GROUP_CH, (j + 1) * POOL_GROUP_CH)
        lo = jnp.clip(pos - w // 2, 0, width - 1)
        hi = jnp.clip(pos + w - 1 - w // 2, 0, width - 1) + 1
        csg = cs[..., sl]
        mean = (csg[:, :, hi] - csg[:, :, lo]) / (hi - lo).astype(jnp.float32)[:, None]
        outs.append((mean - u32[..., sl]) @ pool_w[j].astype(jnp.float32))
    return jnp.concatenate(outs, axis=-1) * pool_scale.astype(jnp.float32)


def hybrid_mixer(proj, st_f, st_b, rows, width, c_f, c_b, s5_d, w_glu, b_glu, pool_w, pool_scale,
                 w_branch_a, w_branch_b, w_out):
    bsz, length, _ = proj.shape
    u_a, u_b, gate_a, gate_b = jnp.split(proj, [D_SSM, D_SSM + D_POOL, D_SSM + D_POOL + D_MODEL], axis=-1)
    y_a = s5_readout(u_a, st_f, st_b, c_f, c_b, s5_d, w_glu, b_glu) @ w_branch_a.astype(jnp.float32)
    y_b = pool_mixer(u_b.reshape(bsz, rows, width, D_POOL), pool_w, pool_scale).reshape(bsz, length, D_POOL)
    y_b = y_b @ w_branch_b.astype(jnp.float32)
    merged = jax.nn.sigmoid(gate_a.astype(jnp.float32)) * y_a + jax.nn.sigmoid(gate_b.astype(jnp.float32)) * y_b
    return (merged @ w_out.astype(jnp.float32)).astype(proj.dtype)


def swiglu(h, w_in, w_out):
    gate, up = jnp.split(h @ w_in, 2, axis=-1)
    return (jax.nn.silu(gate) * up) @ w_out


def setup_inputs(seed: int = 0) -> dict:
    key = jax.random.key(seed)
    ks = jax.random.split(key, 28)
    f32 = jnp.float32

    def nrm(k, shape, scale):
        return jax.random.normal(k, shape, f32) * scale

    G, P, CH = SSM_GROUPS, SSM_STATE, SSM_GROUP_CH
    n_idx = jnp.arange(P, dtype=f32)
    return {
        "x": nrm(ks[0], (BATCH, SEQ, D_MODEL), 1.0),
        "c": nrm(ks[1], (BATCH, D_MODEL), 1.0),
        "ctx": nrm(ks[2], (BATCH, CTX_LEN, D_MODEL), 1.0),
        "c_ctx": nrm(ks[3], (D_MODEL,), 1.0),
        "w_mod": nrm(ks[4], (DEPTH, D_MODEL, 6 * D_MODEL), 0.5 * D_MODEL ** -0.5),
        "b_mod": nrm(ks[5], (DEPTH, 6 * D_MODEL), 0.01),
        "norm1_g": 1.0 + nrm(ks[6], (DEPTH, D_MODEL), 0.05),
        "norm2_g": 1.0 + nrm(ks[7], (DEPTH, D_MODEL), 0.05),
        "w_in": nrm(ks[8], (DEPTH, D_MODEL, D_SSM + D_POOL + 2 * D_MODEL), D_MODEL ** -0.5),
        "s5_a_re": -0.5 + nrm(ks[9], (DEPTH, 2, G, P), 0.01),
        "s5_a_im": math.pi * n_idx + nrm(ks[10], (DEPTH, 2, G, P), 0.01),
        "s5_log_dt": jax.random.uniform(ks[11], (DEPTH, 2, G), f32, math.log(DT_MIN), math.log(DT_MAX)),
        "s5_b_re": nrm(ks[12], (DEPTH, 2, G, P, CH), (2 * CH) ** -0.5),
        "s5_b_im": nrm(ks[13], (DEPTH, 2, G, P, CH), (2 * CH) ** -0.5),
        "s5_c_re": nrm(ks[14], (DEPTH, 2, G, CH, P), (2 * P) ** -0.5),
        "s5_c_im": nrm(ks[15], (DEPTH, 2, G, CH, P), (2 * P) ** -0.5),
        "s5_d": nrm(ks[16], (DEPTH, D_SSM), 0.5),
        "w_glu": nrm(ks[17], (DEPTH, D_SSM, D_SSM), D_SSM ** -0.5),
        "b_glu": nrm(ks[18], (DEPTH, D_SSM), 0.01),
        "pool_w": nrm(ks[19], (DEPTH, len(POOL_WINDOWS), POOL_GROUP_CH, POOL_GROUP_CH), POOL_GROUP_CH ** -0.5),
        "pool_scale": 1.0 + nrm(ks[20], (DEPTH, D_POOL), 0.1),
        "w_branch_a": nrm(ks[21], (DEPTH, D_SSM, D_MODEL), D_SSM ** -0.5),
        "w_branch_b": nrm(ks[22], (DEPTH, D_POOL, D_MODEL), D_POOL ** -0.5),
        "w_out": nrm(ks[23], (DEPTH, D_MODEL, D_MODEL), D_MODEL ** -0.5),
        "w_ffn_in": nrm(ks[24], (DEPTH, D_MODEL, 2 * FFN_HIDDEN), D_MODEL ** -0.5),
        "w_ffn_out": nrm(ks[25], (DEPTH, FFN_HIDDEN, D_MODEL), FFN_HIDDEN ** -0.5),
        "final_norm_g": 1.0 + nrm(ks[26], (D_MODEL,), 0.05),
    }


def reference(x, c, ctx, c_ctx, w_mod, b_mod, norm1_g, norm2_g, w_in, s5_a_re, s5_a_im, s5_log_dt,
              s5_b_re, s5_b_im, s5_c_re, s5_c_im, s5_d, w_glu, b_glu, pool_w, pool_scale,
              w_branch_a, w_branch_b, w_out, w_ffn_in, w_ffn_out, final_norm_g):
    bsz, n_tok, _ = x.shape
    rows = n_tok // GRID_W
    ctx_len = ctx.shape[1]
    for i in range(DEPTH):
        last = i == DEPTH - 1
        mod_x = (jax.nn.silu(c) @ w_mod[i] + b_mod[i])[:, None, :]
        mod_c = (jax.nn.silu(c_ctx) @ w_mod[i] + b_mod[i])[None, None, :]
        sh1, sc1, g1, sh2, sc2, g2 = jnp.split(mod_x, 6, axis=-1)
        csh1, csc1, cg1, csh2, csc2, cg2 = jnp.split(mod_c, 6, axis=-1)

        h = modulate(rms_norm(x, norm1_g[i]), sh1, sc1)
        hc = modulate(rms_norm(ctx, norm1_g[i]), csh1, csc1)
        proj = h @ w_in[i]
        proj_c = hc @ (w_in[i][:, :D_SSM] if last else w_in[i])

        a_f, b_f = s5_discretize(s5_a_re[i, 0], s5_a_im[i, 0], s5_log_dt[i, 0], s5_b_re[i, 0], s5_b_im[i, 0])
        a_b, b_b = s5_discretize(s5_a_re[i, 1], s5_a_im[i, 1], s5_log_dt[i, 1], s5_b_re[i, 1], s5_b_im[i, 1])
        c_f = lax.complex(s5_c_re[i, 0].astype(jnp.float32), s5_c_im[i, 0].astype(jnp.float32))
        c_b = lax.complex(s5_c_re[i, 1].astype(jnp.float32), s5_c_im[i, 1].astype(jnp.float32))

        ua_c = proj_c[..., :D_SSM]
        st_cf = s5_states(ua_c, a_f, b_f, None, False)
        st_cb = s5_states(ua_c, a_b, b_b, None, True)
        st_f = s5_states(proj[..., :D_SSM], a_f, b_f, st_cf[:, -1], False)
        st_b = s5_states(proj[..., :D_SSM], a_b, b_b, st_cb[:, 0], True)

        mixed = hybrid_mixer(proj, st_f, st_b, rows, GRID_W, c_f, c_b, s5_d[i], w_glu[i], b_glu[i],
                             pool_w[i], pool_scale[i], w_branch_a[i], w_branch_b[i], w_out[i])
        x = x + g1 * mixed
        h2 = modulate(rms_norm(x, norm2_g[i]), sh2, sc2)
        x = x + g2 * swiglu(h2, w_ffn_in[i], w_ffn_out[i])

        if not last:
            mixed_c = hybrid_mixer(proj_c, st_cf, st_cb, 1, ctx_len, c_f, c_b, s5_d[i], w_glu[i], b_glu[i],
                                   pool_w[i], pool_scale[i], w_branch_a[i], w_branch_b[i], w_out[i])
            ctx = ctx + cg1 * mixed_c
            hc2 = modulate(rms_norm(ctx, norm2_g[i]), csh2, csc2)
            ctx = ctx + cg2 * swiglu(hc2, w_ffn_in[i], w_ffn_out[i])

    return rms_norm(x, final_norm_g)
```

```python
import functools

import numpy as np
import jax
import jax.numpy as jnp
from jax import lax
from jax.experimental import pallas as pl
from jax.experimental.pallas import tpu as pltpu

_F32 = jnp.float32
_BF16 = jnp.bfloat16

D_MODEL = 1024
D_SSM = 512
D_POOL = 512
GROUPS = 32
STATE = 64
GROUP_CH = 16
CHUNK = 16
FLAT = CHUNK * GROUP_CH
PAIRS = GROUPS // 2
PAIR_LANES = 4 * 2 * STATE
LANE = 128
COL_BLOCKS = D_SSM // LANE
GRID_W = 64
POOL_WINDOWS = (2, 4, 8, 16)
POOL_GROUP_CH = D_POOL // len(POOL_WINDOWS)
FFN_HIDDEN = 2816
RMS_EPS = 1e-6

SCAN_TILE = 128
SCAN_ROWS = SCAN_TILE * CHUNK
NORM_ROWS = 256
MIX_ROWS = 512
FFN_ROWS = 512
SCAN_PAIRS = 4
VMEM_LIMIT = 56 * 1024 * 1024

_HI = lax.Precision.HIGHEST


def _rms_mod(x, g, sh, sc):
    ms = jnp.mean(x * x, axis=-1, keepdims=True)
    y = x * lax.rsqrt(ms + RMS_EPS)
    return (y * g) * (1.0 + sc) + sh


def _dot(a, b):
    return jnp.dot(a, b, preferred_element_type=_F32)


def _const_spec(shape):
    zeros = (0,) * len(shape)
    return pl.BlockSpec(shape, lambda *_: zeros, pipeline_mode=pl.Buffered(1))


def _mod_kernel(c_ref, w_ref, b_ref, o_ref):
    c = c_ref[...]
    a = c * jax.nn.sigmoid(c)
    o_ref[...] = jnp.dot(a, w_ref[...], preferred_element_type=_F32, precision=_HI) + b_ref[...]


def _modulation(cc, w_mod, b_mod):
    n_out = w_mod.shape[1]
    blk = 1024
    return pl.pallas_call(
        _mod_kernel,
        grid=(n_out // blk,),
        in_specs=[
            pl.BlockSpec((8, D_MODEL), lambda i: (0, 0)),
            pl.BlockSpec((D_MODEL, blk), lambda i: (0, i)),
            pl.BlockSpec((1, blk), lambda i: (0, i)),
        ],
        out_specs=pl.BlockSpec((8, blk), lambda i: (0, i)),
        out_shape=jax.ShapeDtypeStruct((8, n_out), _F32),
        compiler_params=pltpu.CompilerParams(
            dimension_semantics=("arbitrary",), vmem_limit_bytes=VMEM_LIMIT),
        name="adaln_mod",
    )(cc, w_mod, b_mod.reshape(1, n_out))


def _p1_kernel(x_ref, sh_ref, sc_ref, g_ref, wa_ref, wst_ref, uflat_ref, s_ref, h_ref, u_ref, ut_ref):
    g = g_ref[...]
    sh = sh_ref[0]
    sc = sc_ref[0]

    def norm_body(i, carry):
        r0 = pl.multiple_of(i * NORM_ROWS, NORM_ROWS)
        h_ref[pl.ds(r0, NORM_ROWS), :] = _rms_mod(x_ref[0, pl.ds(r0, NORM_ROWS), :], g, sh, sc).astype(_BF16)
        return carry

    lax.fori_loop(0, SCAN_ROWS // NORM_ROWS, norm_body, 0)
    u = _dot(h_ref[...], wa_ref[...])
    for cb in range(COL_BLOCKS):
        u_ref[cb] = u[:, cb * LANE:(cb + 1) * LANE]

    def slab_body(sg, carry):
        r0 = pl.multiple_of(sg * GROUP_CH, GROUP_CH)
        for cb in range(COL_BLOCKS):
            slab = u_ref[cb, pl.ds(sg, SCAN_TILE, stride=CHUNK), :]
            ut_ref[cb * 8:(cb + 1) * 8, pl.ds(r0, GROUP_CH), :] = slab.T.reshape(8, GROUP_CH, SCAN_TILE)
        return carry

    lax.fori_loop(0, CHUNK, slab_body, 0)

    def pair_body(q, carry):
        uf0 = ut_ref[2 * q].T.astype(_BF16)
        uf1 = ut_ref[2 * q + 1].T.astype(_BF16)
        uflat_ref[0, 2 * q] = uf0
        uflat_ref[0, 2 * q + 1] = uf1
        s_ref[0, q] = _dot(uf0, wst_ref[q, :FLAT, :]) + _dot(uf1, wst_ref[q, FLAT:, :])
        return carry

    lax.fori_loop(0, PAIRS, pair_body, 0)


def _pass1(x, mod3, mod_row0, norm_g, w_a, wst):
    bsz, n_tok, _ = x.shape
    n_chunks = n_tok // CHUNK
    nt = n_chunks // SCAN_TILE
    return pl.pallas_call(
        _p1_kernel,
        grid=(bsz, nt),
        in_specs=[
            pl.BlockSpec((1, SCAN_ROWS, D_MODEL), lambda b, t: (b, t, 0)),
            pl.BlockSpec((1, 1, D_MODEL), lambda b, t: (b + mod_row0, 0, 0)),
            pl.BlockSpec((1, 1, D_MODEL), lambda b, t: (b + mod_row0, 0, 1)),
            _const_spec((1, D_MODEL)),
            _const_spec((D_MODEL, D_SSM)),
            _const_spec((PAIRS, 2 * FLAT, PAIR_LANES)),
        ],
        out_specs=[
            pl.BlockSpec((1, GROUPS, SCAN_TILE, FLAT), lambda b, t: (b, 0, t, 0)),
            pl.BlockSpec((1, PAIRS, SCAN_TILE, PAIR_LANES), lambda b, t: (b, 0, t, 0)),
        ],
        out_shape=[
            jax.ShapeDtypeStruct((bsz, GROUPS, n_chunks, FLAT), _BF16),
            jax.ShapeDtypeStruct((bsz, PAIRS, n_chunks, PAIR_LANES), _F32),
        ],
        scratch_shapes=[
            pltpu.VMEM((SCAN_ROWS, D_MODEL), _BF16),
            pltpu.VMEM((COL_BLOCKS, SCAN_ROWS, LANE), _F32),
            pltpu.VMEM((GROUPS, FLAT, SCAN_TILE), _F32),
        ],
        compiler_params=pltpu.CompilerParams(
            dimension_semantics=("arbitrary", "arbitrary"), vmem_limit_bytes=VMEM_LIMIT),
        name="s5_chunk_states",
    )(x, mod3, mod3, norm_g, w_a, wst)


def _scan_kernel(s_ref, sc_ref, ar_ref, ai_ref, x_ref, xs_ref, *, bsz, n_chunks, n_ctx):
    chains = [(b, qq) for b in range(bsz) for qq in range(SCAN_PAIRS)]
    h = PAIR_LANES // 2
    r = PAIR_LANES // 4

    def step(xs, s_f, s_b):
        out = []
        for (b, qq), x, sf, sb in zip(chains, xs, s_f, s_b):
            a_r = ar_ref[qq]
            a_i = ai_ref[qq]
            pieces = []
            for d, s in ((0, sf), (1, sb)):
                o = d * h
                re, im = x[:, o:o + r], x[:, o + r:o + h]
                pieces.append(a_r[:, o:o + r] * re - a_i[:, o:o + r] * im + s[:, o:o + r])
                pieces.append(a_r[:, o:o + r] * im + a_i[:, o:o + r] * re + s[:, o + r:o + h])
            out.append(jnp.concatenate(pieces, axis=1))
        return tuple(out)

    def ctx_body(i, xs):
        s_f = [sc_ref[0, qq, pl.ds(b * n_ctx + i, 1), :] for (b, qq) in chains]
        s_b = [sc_ref[0, qq, pl.ds(b * n_ctx + n_ctx - 1 - i, 1), :] for (b, qq) in chains]
        return step(xs, s_f, s_b)

    xs = lax.fori_loop(0, n_ctx, ctx_body,
                       tuple(jnp.zeros((1, PAIR_LANES), _F32) for _ in chains))

    def body(i, xs):
        jf = i
        jb = n_chunks - 1 - i
        for (b, qq), x in zip(chains, xs):
            xs_ref[b, qq, pl.ds(jf, 1), 0:h] = x[:, 0:h]
            xs_ref[b, qq, pl.ds(jb, 1), h:PAIR_LANES] = x[:, h:PAIR_LANES]
        s_f = [s_ref[b, qq, pl.ds(jf, 1), :] for (b, qq) in chains]
        s_b = [s_ref[b, qq, pl.ds(jb, 1), :] for (b, qq) in chains]
        return step(xs, s_f, s_b)

    lax.fori_loop(0, n_chunks, body, xs)
    x_ref[...] = xs_ref[...].astype(_BF16)


def _chunk_scan(s_loc, s_ctx, a_re, a_im, n_ctx):
    bsz, _, n_chunks, _ = s_loc.shape
    blk = (bsz, SCAN_PAIRS, n_chunks, PAIR_LANES)
    return pl.pallas_call(
        functools.partial(_scan_kernel, bsz=bsz, n_chunks=n_chunks, n_ctx=n_ctx),
        grid=(PAIRS // SCAN_PAIRS,),
        in_specs=[
            pl.BlockSpec(blk, lambda i: (0, i, 0, 0)),
            pl.BlockSpec((1, SCAN_PAIRS, SCAN_TILE, PAIR_LANES), lambda i: (0, i, 0, 0)),
            pl.BlockSpec((SCAN_PAIRS, 1, PAIR_LANES), lambda i: (i, 0, 0)),
            pl.BlockSpec((SCAN_PAIRS, 1, PAIR_LANES), lambda i: (i, 0, 0)),
        ],
        out_specs=pl.BlockSpec(blk, lambda i: (0, i, 0, 0)),
        out_shape=jax.ShapeDtypeStruct(s_loc.shape, _BF16),
        scratch_shapes=[pltpu.VMEM(blk, _F32)],
        compiler_params=pltpu.CompilerParams(
            dimension_semantics=("arbitrary",), vmem_limit_bytes=VMEM_LIMIT),
        name="s5_chunk_scan",
    )(s_loc, s_ctx, a_re, a_im)


def _readout_kernel(uflat_ref, x_ref, mu_ref, mv_ref, y_ref, yt_ref, ys_ref):
    def pair_body(q, carry):
        y0 = _dot(uflat_ref[0, 2 * q], mu_ref[2 * q])
        y1 = _dot(uflat_ref[0, 2 * q + 1], mu_ref[2 * q + 1])
        yx = _dot(x_ref[0, q], mv_ref[q])
        yt = (jnp.concatenate([y0, y1], axis=1) + yx).T
        yt_ref[2 * q] = yt[:FLAT]
        yt_ref[2 * q + 1] = yt[FLAT:]
        return carry

    lax.fori_loop(0, PAIRS, pair_body, 0)

    def slab_body(sg, carry):
        r0 = pl.multiple_of(sg * GROUP_CH, GROUP_CH)
        for cb in range(COL_BLOCKS):
            yt = yt_ref[cb * 8:(cb + 1) * 8, pl.ds(r0, GROUP_CH), :].reshape(LANE, SCAN_TILE)
            ys_ref[cb, pl.ds(sg, SCAN_TILE, stride=CHUNK), :] = yt.T
        return carry

    lax.fori_loop(0, CHUNK, slab_body, 0)
    for cb in range(COL_BLOCKS):
        y_ref[0, :, cb * LANE:(cb + 1) * LANE] = ys_ref[cb]


def _readout(uflat, xstart, mu, mv):
    bsz, _, n_chunks, _ = uflat.shape
    nt = n_chunks // SCAN_TILE
    return pl.pallas_call(
        _readout_kernel,
        grid=(bsz, nt),
        in_specs=[
            pl.BlockSpec((1, GROUPS, SCAN_TILE, FLAT), lambda b, t: (b, 0, t, 0)),
            pl.BlockSpec((1, PAIRS, SCAN_TILE, PAIR_LANES), lambda b, t: (b, 0, t, 0)),
            _const_spec((GROUPS, FLAT, FLAT)),
            _const_spec((PAIRS, PAIR_LANES, 2 * FLAT)),
        ],
        out_specs=pl.BlockSpec((1, SCAN_ROWS, D_SSM), lambda b, t: (b, t, 0)),
        out_shape=jax.ShapeDtypeStruct((bsz, n_chunks * CHUNK, D_SSM), _F32),
        scratch_shapes=[
            pltpu.VMEM((GROUPS, FLAT, SCAN_TILE), _F32),
            pltpu.VMEM((COL_BLOCKS, SCAN_ROWS, LANE), _F32),
        ],
        compiler_params=pltpu.CompilerParams(
            dimension_semantics=("arbitrary", "arbitrary"), vmem_limit_bytes=VMEM_LIMIT),
        name="s5_readout",
    )(uflat, xstart, mu, mv)


def _mix_kernel(x_ref, y_ref, sh_ref, sc_ref, gt_ref, g_ref, wr_ref, wglu_ref, bglu_ref, wa_ref,
                pmat_ref, pinv_ref, pw_ref, ps_ref, wb_ref, wo_ref, o_ref):
    x = x_ref[0]
    h = _rms_mod(x, g_ref[...], sh_ref[0], sc_ref[0]).astype(_BF16)
    proj = _dot(h, wr_ref[...])
    ub = proj[:, :D_POOL]
    gate_a = proj[:, D_POOL:D_POOL + D_MODEL]
    gate_b = proj[:, D_POOL + D_MODEL:]

    y = jax.nn.gelu(y_ref[0])
    z = y * jax.nn.sigmoid(_dot(y.astype(_BF16), wglu_ref[...]) + bglu_ref[...])
    ya = _dot(z.astype(_BF16), wa_ref[...])

    ub_hi = ub.astype(_BF16)
    ub_lo = (ub - ub_hi.astype(_F32)).astype(_BF16)
    outs = []
    for wi in range(len(POOL_WINDOWS)):
        sl = slice(wi * POOL_GROUP_CH, (wi + 1) * POOL_GROUP_CH)
        wsum = _dot(pmat_ref[wi], ub_hi[:, sl]) + _dot(pmat_ref[wi], ub_lo[:, sl])
        diff = wsum * pinv_ref[wi] - ub[:, sl]
        outs.append(_dot(diff.astype(_BF16), pw_ref[wi]))
    pb = jnp.concatenate(outs, axis=1) * ps_ref[...]
    yb = _dot(pb.astype(_BF16), wb_ref[...])

    merged = jax.nn.sigmoid(gate_a) * ya + jax.nn.sigmoid(gate_b) * yb
    mixed = _dot(merged.astype(_BF16), wo_ref[...])
    o_ref[0] = x + gt_ref[0] * mixed


def _pool_matrices(rows):
    tok = np.arange(rows)
    row, pos = tok // GRID_W, tok % GRID_W
    mats, invs = [], []
    for w in POOL_WINDOWS:
        lo = np.clip(pos - w // 2, 0, GRID_W - 1)
        hi = np.clip(pos + w - 1 - w // 2, 0, GRID_W - 1) + 1
        m = (row[:, None] == row[None, :]) & (pos[None, :] >= lo[:, None]) & (pos[None, :] < hi[:, None])
        mats.append(m.astype(np.float32))
        invs.append(np.broadcast_to((1.0 / (hi - lo).astype(np.float32))[:, None], (rows, POOL_GROUP_CH)))
    return np.stack(mats), np.stack(invs)


def _mixer(x, ypre, mod3, norm_g, w_rest, w_glu, b_glu, w_a, pool_w, pool_scale, w_b, w_out):
    bsz, n_tok, _ = x.shape
    tm = MIX_ROWS
    pmat, pinv = _pool_matrices(tm)
    pmat = jnp.asarray(pmat, _BF16)
    pinv = jnp.asarray(pinv, _F32)
    nw = len(POOL_WINDOWS)
    return pl.pallas_call(
        _mix_kernel,
        grid=(bsz, n_tok // tm),
        in_specs=[
            pl.BlockSpec((1, tm, D_MODEL), lambda b, t: (b, t, 0)),
            pl.BlockSpec((1, tm, D_SSM), lambda b, t: (b, t, 0)),
            pl.BlockSpec((1, 1, D_MODEL), lambda b, t: (b, 0, 0)),
            pl.BlockSpec((1, 1, D_MODEL), lambda b, t: (b, 0, 1)),
            pl.BlockSpec((1, 1, D_MODEL), lambda b, t: (b, 0, 2)),
            _const_spec((1, D_MODEL)),
            _const_spec((D_MODEL, D_POOL + 2 * D_MODEL)),
            _const_spec((D_SSM, D_SSM)),
            _const_spec((1, D_SSM)),
            _const_spec((D_SSM, D_MODEL)),
            _const_spec((nw, tm, tm)),
            _const_spec((nw, tm, POOL_GROUP_CH)),
            _const_spec((nw, POOL_GROUP_CH, POOL_GROUP_CH)),
            _const_spec((1, D_POOL)),
            _const_spec((D_POOL, D_MODEL)),
            _const_spec((D_MODEL, D_MODEL)),
        ],
        out_specs=pl.BlockSpec((1, tm, D_MODEL), lambda b, t: (b, t, 0)),
        out_shape=jax.ShapeDtypeStruct(x.shape, _F32),
        compiler_params=pltpu.CompilerParams(
            dimension_semantics=("arbitrary", "arbitrary"), vmem_limit_bytes=VMEM_LIMIT),
        name="token_mixer",
    )(x, ypre, mod3, mod3, mod3, norm_g, w_rest, w_glu, b_glu, w_a, pmat, pinv, pool_w, pool_scale,
      w_b, w_out)


def _ffn_kernel(x_ref, sh_ref, sc_ref, gt_ref, g2_ref, gf_ref, wg_ref, wu_ref, wo_ref, o_ref):
    x = x_ref[0]
    h = _rms_mod(x, g2_ref[...], sh_ref[0], sc_ref[0]).astype(_BF16)
    gate = _dot(h, wg_ref[...])
    up = _dot(h, wu_ref[...])
    act = (gate * jax.nn.sigmoid(gate) * up).astype(_BF16)
    y = x + gt_ref[0] * _dot(act, wo_ref[...])
    ms = jnp.mean(y * y, axis=-1, keepdims=True)
    o_ref[0] = (y * lax.rsqrt(ms + RMS_EPS)) * gf_ref[...]


def _ffn(x1, mod3, norm2_g, final_g, w_gate, w_up, w_down):
    bsz, n_tok, _ = x1.shape
    tm = FFN_ROWS
    return pl.pallas_call(
        _ffn_kernel,
        grid=(bsz, n_tok // tm),
        in_specs=[
            pl.BlockSpec((1, tm, D_MODEL), lambda b, t: (b, t, 0)),
            pl.BlockSpec((1, 1, D_MODEL), lambda b, t: (b, 0, 3)),
            pl.BlockSpec((1, 1, D_MODEL), lambda b, t: (b, 0, 4)),
            pl.BlockSpec((1, 1, D_MODEL), lambda b, t: (b, 0, 5)),
            _const_spec((1, D_MODEL)),
            _const_spec((1, D_MODEL)),
            _const_spec((D_MODEL, FFN_HIDDEN)),
            _const_spec((D_MODEL, FFN_HIDDEN)),
            _const_spec((FFN_HIDDEN, D_MODEL)),
        ],
        out_specs=pl.BlockSpec((1, tm, D_MODEL), lambda b, t: (b, t, 0)),
        out_shape=jax.ShapeDtypeStruct(x1.shape, _F32),
        compiler_params=pltpu.CompilerParams(
            dimension_semantics=("arbitrary", "arbitrary"), vmem_limit_bytes=VMEM_LIMIT),
        name="swiglu_ffn",
    )(x1, mod3, mod3, mod3, norm2_g, final_g, w_gate, w_up, w_down)


def _s5_tables(a_re, a_im, log_dt, b_re, b_im, c_re, c_im, d_skip):
    f32 = _F32
    g_, p_, ch, t_ = GROUPS, STATE, GROUP_CH, CHUNK
    a_re, a_im = a_re.astype(f32), a_im.astype(f32)
    dt = jnp.exp(log_dt.astype(f32))[..., None]
    k = jnp.arange(t_ + 1, dtype=f32)[None, :, None, None]
    mag = jnp.exp(k * (a_re * dt)[:, None])
    ang = k * (a_im * dt)[:, None]
    pw_re, pw_im = mag * jnp.cos(ang), mag * jnp.sin(ang)
    num_re, num_im = pw_re[:, 1] - 1.0, pw_im[:, 1]
    den = a_re * a_re + a_im * a_im
    f_re = (num_re * a_re + num_im * a_im) / den
    f_im = (num_im * a_re - num_re * a_im) / den
    b_re, b_im = b_re.astype(f32), b_im.astype(f32)
    bb_re = f_re[..., None] * b_re - f_im[..., None] * b_im
    bb_im = f_re[..., None] * b_im + f_im[..., None] * b_re
    c_re, c_im = c_re.astype(f32), c_im.astype(f32)

    cp_re = c_re[:, None] * pw_re[:, :t_, :, None, :] - c_im[:, None] * pw_im[:, :t_, :, None, :]
    cp_im = c_re[:, None] * pw_im[:, :t_, :, None, :] + c_im[:, None] * pw_re[:, :t_, :, None, :]
    kern = (jnp.einsum('dkgcp,dgpe->dkgce', cp_re, bb_re, precision=_HI)
            - jnp.einsum('dkgcp,dgpe->dkgce', cp_im, bb_im, precision=_HI))
    sig = np.arange(t_)[:, None]
    tau = np.arange(t_)[None, :]
    lag_f = np.clip(tau - sig, 0, t_ - 1)
    lag_b = np.clip(sig - tau, 0, t_ - 1)
    mf = kern[0][lag_f] * jnp.asarray(tau >= sig, f32)[:, :, None, None, None]
    mb = kern[1][lag_b] * jnp.asarray(sig >= tau, f32)[:, :, None, None, None]
    m = jnp.transpose(mf + mb, (2, 0, 4, 1, 3))
    eye_t = jnp.eye(t_, dtype=f32)[None, :, None, :, None]
    eye_c = jnp.eye(ch, dtype=f32)[None, None, :, None, :]
    m = m + eye_t * eye_c * d_skip.astype(f32).reshape(g_, 1, 1, 1, ch)
    mu = m.reshape(g_, FLAT, FLAT).astype(_BF16)

    def cmul(xr, xi, yr, yi):
        return xr * yr - xi * yi, xr * yi + xi * yr

    pf_re, pf_im = pw_re[0, :t_][::-1], pw_im[0, :t_][::-1]
    pb_re, pb_im = pw_re[1, :t_], pw_im[1, :t_]
    wf_re, wf_im = cmul(pf_re[:, :, :, None], pf_im[:, :, :, None], bb_re[0][None], bb_im[0][None])
    wb_re, wb_im = cmul(pb_re[:, :, :, None], pb_im[:, :, :, None], bb_re[1][None], bb_im[1][None])
    w4 = jnp.stack([wf_re, wf_im, wb_re, wb_im], axis=0)
    w4 = jnp.transpose(w4, (2, 1, 4, 0, 3)).reshape(PAIRS, 2, FLAT, 4, p_)
    eye2 = jnp.eye(2, dtype=f32)
    wst = w4[:, :, :, :, None, :] * eye2[None, :, None, None, :, None]
    wst = wst.reshape(PAIRS, 2 * FLAT, PAIR_LANES).astype(_BF16)

    qf_re, qf_im = pw_re[0, 1:t_ + 1], pw_im[0, 1:t_ + 1]
    qb_re, qb_im = pw_re[1, 1:t_ + 1][::-1], pw_im[1, 1:t_ + 1][::-1]
    vf_re, vf_im = cmul(c_re[0][None], c_im[0][None], qf_re[:, :, None, :], qf_im[:, :, None, :])
    vb_re, vb_im = cmul(c_re[1][None], c_im[1][None], qb_re[:, :, None, :], qb_im[:, :, None, :])
    v4 = jnp.stack([vf_re, -vf_im, vb_re, -vb_im], axis=0)
    v4 = jnp.transpose(v4, (2, 0, 4, 1, 3)).reshape(PAIRS, 2, 4, p_, FLAT)
    mv = v4[:, :, :, :, None, :] * eye2[None, :, None, None, :, None]
    mv = jnp.transpose(mv, (0, 2, 1, 3, 4, 5)).reshape(PAIRS, PAIR_LANES, 2 * FLAT).astype(_BF16)

    def lanes(v):
        v = v.reshape(2, PAIRS, 2, p_)
        v = jnp.stack([v[0], v[0], v[1], v[1]], axis=1)
        return v.reshape(PAIRS, 1, PAIR_LANES)
    return mu, wst, mv, lanes(pw_re[:, t_]), lanes(pw_im[:, t_])


def kernel(x, c, ctx, c_ctx, w_mod, b_mod, norm1_g, norm2_g, w_in, s5_a_re, s5_a_im, s5_log_dt,
           s5_b_re, s5_b_im, s5_c_re, s5_c_im, s5_d, w_glu, b_glu, pool_w, pool_scale,
           w_branch_a, w_branch_b, w_out, w_ffn_in, w_ffn_out, final_norm_g):
    bsz, n_tok, d = x.shape
    ctx_len = ctx.shape[1]
    assert d == D_MODEL and w_mod.shape[0] == 1 and bsz + 1 <= 8
    assert n_tok % SCAN_ROWS == 0 and n_tok % MIX_ROWS == 0 and MIX_ROWS % GRID_W == 0
    assert ctx_len % CHUNK == 0 and bsz * ctx_len <= SCAN_ROWS
    n_ctx = ctx_len // CHUNK

    cc = jnp.zeros((8, D_MODEL), _F32).at[:bsz].set(c).at[bsz].set(c_ctx)
    mod3 = _modulation(cc, w_mod[0], b_mod[0]).reshape(8, 1, 6 * D_MODEL)

    mu, wst, mv, a16_re, a16_im = _s5_tables(
        s5_a_re[0], s5_a_im[0], s5_log_dt[0], s5_b_re[0], s5_b_im[0], s5_c_re[0], s5_c_im[0], s5_d[0])

    w_in_b = w_in[0].astype(_BF16)
    w_a = w_in_b[:, :D_SSM]
    w_rest = w_in_b[:, D_SSM:]
    n1 = norm1_g[0].reshape(1, D_MODEL)

    ctx_rows = ctx.reshape(1, bsz * ctx_len, D_MODEL)
    ctx_rows = jnp.pad(ctx_rows, ((0, 0), (0, SCAN_ROWS - bsz * ctx_len), (0, 0)))

    uflat, s_loc = _pass1(x, mod3, 0, n1, w_a, wst)
    _, s_ctx = _pass1(ctx_rows, mod3, bsz, n1, w_a, wst)
    xstart = _chunk_scan(s_loc, s_ctx, a16_re, a16_im, n_ctx)
    ypre = _readout(uflat, xstart, mu, mv)

    x1 = _mixer(x, ypre, mod3, n1, w_rest, w_glu[0].astype(_BF16), b_glu[0].reshape(1, D_SSM),
                w_branch_a[0].astype(_BF16), pool_w[0].astype(_BF16), pool_scale[0].reshape(1, D_POOL),
                w_branch_b[0].astype(_BF16), w_out[0].astype(_BF16))

    w_ffn = w_ffn_in[0].astype(_BF16)
    return _ffn(x1, mod3, norm2_g[0].reshape(1, D_MODEL), final_norm_g.reshape(1, D_MODEL),
                w_ffn[:, :FFN_HIDDEN], w_ffn[:, FFN_HIDDEN:], w_ffn_out[0].astype(_BF16))
```

```python
import functools

import numpy as np
import jax
import jax.numpy as jnp
from jax import lax
from jax.experimental import pallas as pl
from jax.experimental.pallas import tpu as pltpu

_F32 = jnp.float32
_BF16 = jnp.bfloat16

D_MODEL = 1024
D_SSM = 512
D_POOL = 512
GROUPS = 32
STATE = 64
GROUP_CH = 16
CHUNK = 16
FLAT = CHUNK * GROUP_CH
PAIRS = GROUPS // 2
PAIR_LANES = 4 * 2 * STATE
LANE = 128
SUBLANES = 8
COL_BLOCKS = D_SSM // LANE
GRID_W = 64
POOL_WINDOWS = (2, 4, 8, 16)
POOL_GROUP_CH = D_POOL // len(POOL_WINDOWS)
FFN_HIDDEN = 2816
RMS_EPS = 1e-6

SCAN_TILE = 128
SCAN_ROWS = SCAN_TILE * CHUNK
NORM_ROWS = 256
MIX_ROWS = 512
FFN_ROWS = 512
SCAN_PAIRS = 4
VMEM_LIMIT = 56 * 1024 * 1024

_HI = lax.Precision.HIGHEST


def _rms_mod(x, gain, sh):
    ms = jnp.mean(x * x, axis=-1, keepdims=True)
    return (x * lax.rsqrt(ms + RMS_EPS)) * gain + sh


def _dot(a, b):
    return jnp.dot(a, b, preferred_element_type=_F32)


def _const_spec(shape, index=None):
    index = (0,) * len(shape) if index is None else index
    return pl.BlockSpec(shape, lambda *_: index, pipeline_mode=pl.Buffered(1))


def _mod_kernel(c_ref, w_ref, b_ref, o_ref):
    c = c_ref[...]
    a = c * jax.nn.sigmoid(c)
    o_ref[...] = jnp.dot(a, w_ref[...], preferred_element_type=_F32, precision=_HI) + b_ref[...]


def _modulation(cc, w_mod, b_mod):
    n_out = w_mod.shape[1]
    blk = 1024
    return pl.pallas_call(
        _mod_kernel,
        grid=(n_out // blk,),
        in_specs=[
            pl.BlockSpec((8, D_MODEL), lambda i: (0, 0)),
            pl.BlockSpec((D_MODEL, blk), lambda i: (0, i)),
            pl.BlockSpec((1, blk), lambda i: (0, i)),
        ],
        out_specs=pl.BlockSpec((8, blk), lambda i: (0, i)),
        out_shape=jax.ShapeDtypeStruct((8, n_out), _F32),
        compiler_params=pltpu.CompilerParams(
            dimension_semantics=("arbitrary",), vmem_limit_bytes=VMEM_LIMIT),
        name="adaln_mod",
    )(cc, w_mod, b_mod.reshape(1, n_out))


def _p1_kernel(x_ref, sh_ref, sc_ref, g_ref, wa_ref, wst_ref, uflat_ref, s_ref, h_ref, u_ref, ut_ref,
               *, rows):
    gain = g_ref[...] * (1.0 + sc_ref[0])
    sh = sh_ref[0]

    def norm_body(i, carry):
        r0 = pl.multiple_of(i * NORM_ROWS, NORM_ROWS)
        h_ref[pl.ds(r0, NORM_ROWS), :] = _rms_mod(x_ref[0, pl.ds(r0, NORM_ROWS), :], gain, sh).astype(_BF16)
        return carry

    lax.fori_loop(0, rows // NORM_ROWS, norm_body, 0)
    if rows < SCAN_ROWS:
        h_ref[rows:, :] = jnp.zeros((SCAN_ROWS - rows, D_MODEL), _BF16)
    u = _dot(h_ref[...], wa_ref[...])
    for cb in range(COL_BLOCKS):
        u_ref[cb] = u[:, cb * LANE:(cb + 1) * LANE]

    def slab_body(sg, carry):
        r0 = pl.multiple_of(sg * GROUP_CH, GROUP_CH)
        for cb in range(COL_BLOCKS):
            slab = u_ref[cb, pl.ds(sg, SCAN_TILE, stride=CHUNK), :]
            ut_ref[cb * 8:(cb + 1) * 8, pl.ds(r0, GROUP_CH), :] = slab.T.reshape(8, GROUP_CH, SCAN_TILE)
        return carry

    lax.fori_loop(0, CHUNK, slab_body, 0, unroll=2)

    def pair_body(q, carry):
        uf0 = ut_ref[2 * q].T.astype(_BF16)
        uf1 = ut_ref[2 * q + 1].T.astype(_BF16)
        uflat_ref[0, 2 * q] = uf0
        uflat_ref[0, 2 * q + 1] = uf1
        s_ref[0, q] = _dot(uf0, wst_ref[q, :FLAT, :]) + _dot(uf1, wst_ref[q, FLAT:, :])
        return carry

    lax.fori_loop(0, PAIRS, pair_body, 0, unroll=2)


def _pass1(x, mod3, mod_row0, norm_g, w_in_b, wst):
    bsz, n_tok, _ = x.shape
    rows = min(n_tok, SCAN_ROWS)
    assert n_tok % rows == 0 and rows % NORM_ROWS == 0
    nt = n_tok // rows
    n_chunks = nt * SCAN_TILE
    return pl.pallas_call(
        functools.partial(_p1_kernel, rows=rows),
        grid=(bsz, nt),
        in_specs=[
            pl.BlockSpec((1, rows, D_MODEL), lambda b, t: (b, t, 0)),
            pl.BlockSpec((1, 1, D_MODEL), lambda b, t: (b + mod_row0, 0, 0)),
            pl.BlockSpec((1, 1, D_MODEL), lambda b, t: (b + mod_row0, 0, 1)),
            _const_spec((1, D_MODEL)),
            _const_spec((D_MODEL, D_SSM)),
            _const_spec((PAIRS, 2 * FLAT, PAIR_LANES)),
        ],
        out_specs=[
            pl.BlockSpec((1, GROUPS, SCAN_TILE, FLAT), lambda b, t: (b, 0, t, 0)),
            pl.BlockSpec((1, PAIRS, SCAN_TILE, PAIR_LANES), lambda b, t: (b, 0, t, 0)),
        ],
        out_shape=[
            jax.ShapeDtypeStruct((bsz, GROUPS, n_chunks, FLAT), _BF16),
            jax.ShapeDtypeStruct((bsz, PAIRS, n_chunks, PAIR_LANES), _F32),
        ],
        scratch_shapes=[
            pltpu.VMEM((SCAN_ROWS, D_MODEL), _BF16),
            pltpu.VMEM((COL_BLOCKS, SCAN_ROWS, LANE), _F32),
            pltpu.VMEM((GROUPS, FLAT, SCAN_TILE), _F32),
        ],
        compiler_params=pltpu.CompilerParams(
            dimension_semantics=("arbitrary", "arbitrary"), vmem_limit_bytes=VMEM_LIMIT),
        name="s5_chunk_states",
    )(x, mod3, mod3, norm_g, w_in_b, wst)


def _scan_kernel(s_ref, sc_ref, ar_ref, ai_ref, x_ref, xs_ref, c_ref, *, bsz, n_chunks, n_ctx):
    rb = SUBLANES
    rid = lax.broadcasted_iota(jnp.int32, (rb, LANE), 0)
    chains = [(b, qq) for b in range(bsz) for qq in range(SCAN_PAIRS)]

    for qq in range(SCAN_PAIRS):
        for d in range(2):
            o = d * 2 * LANE
            a_r = ar_ref[qq][:, o:o + LANE]
            a_i = ai_ref[qq][:, o:o + LANE]
            pows = [(a_r, a_i)]
            for _ in range(rb - 1):
                pows.append(_cmul(pows[-1][0], pows[-1][1], a_r, a_i))
            idx = qq * 2 + d
            for k, shift in enumerate((1, 2, 4)):
                keep = (rid >= shift) if d == 0 else (rid < rb - shift)
                c_ref[idx, 2 * k] = jnp.where(keep, pows[shift - 1][0], 0.0)
                c_ref[idx, 2 * k + 1] = jnp.where(keep, pows[shift - 1][1], 0.0)
            p_r = jnp.zeros((rb, LANE), _F32)
            p_i = jnp.zeros((rb, LANE), _F32)
            for r in range(rb):
                e = r if d == 0 else rb - 1 - r
                p_r = jnp.where(rid == r, pows[e][0], p_r)
                p_i = jnp.where(rid == r, pows[e][1], p_i)
            c_ref[idx, 6] = p_r
            c_ref[idx, 7] = p_i

    def block(idx, d, s_re, s_im, xin_re, xin_im):
        def shifted(v, k):
            return pltpu.roll(v, k if d == 0 else rb - k, 0)

        t_re, t_im = s_re, s_im
        for k in range(3):
            a_r = c_ref[idx, 2 * k]
            a_i = c_ref[idx, 2 * k + 1]
            u_re, u_im = shifted(t_re, 1 << k), shifted(t_im, 1 << k)
            t_re, t_im = t_re + (a_r * u_re - a_i * u_im), t_im + (a_r * u_im + a_i * u_re)
        p_r = c_ref[idx, 6]
        p_i = c_ref[idx, 7]
        after_re = t_re + (p_r * xin_re - p_i * xin_im)
        after_im = t_im + (p_r * xin_im + p_i * xin_re)
        first = 0 if d == 0 else rb - 1
        last = rb - 1 - first
        start_re = jnp.where(rid == first, xin_re, shifted(after_re, 1))
        start_im = jnp.where(rid == first, xin_im, shifted(after_im, 1))
        return start_re, start_im, after_re[last:last + 1], after_im[last:last + 1]

    def sweep(src_ref, lead, row0, n_blocks, i, carry, dst_ref):
        out = []
        for ci, (b, qq) in enumerate(chains):
            for d in range(2):
                blk = i if d == 0 else n_blocks - 1 - i
                r0 = row0(b) + blk * rb
                if not isinstance(r0, int):
                    r0 = pl.multiple_of(r0, rb)
                o = d * 2 * LANE
                bi = lead(b)
                s_re = src_ref[bi, qq, pl.ds(r0, rb), o:o + LANE]
                s_im = src_ref[bi, qq, pl.ds(r0, rb), o + LANE:o + 2 * LANE]
                xin_re, xin_im = carry[4 * ci + 2 * d], carry[4 * ci + 2 * d + 1]
                st_re, st_im, xo_re, xo_im = block(qq * 2 + d, d, s_re, s_im, xin_re, xin_im)
                if dst_ref is not None:
                    dst_ref[bi, qq, pl.ds(r0, rb), o:o + LANE] = st_re
                    dst_ref[bi, qq, pl.ds(r0, rb), o + LANE:o + 2 * LANE] = st_im
                out += [xo_re, xo_im]
        return tuple(out)

    carry = tuple(jnp.zeros((1, LANE), _F32) for _ in range(4 * len(chains)))
    ctx_blocks = n_ctx // rb
    for i in range(ctx_blocks):
        carry = sweep(sc_ref, lambda b: 0, lambda b: b * n_ctx, ctx_blocks, i, carry, None)

    n_blocks = n_chunks // rb
    lax.fori_loop(
        0, n_blocks,
        lambda i, c: sweep(s_ref, lambda b: b, lambda b: 0, n_blocks, i, c, xs_ref),
        carry)
    x_ref[...] = xs_ref[...].astype(_BF16)


def _chunk_scan(s_loc, s_ctx, a_re, a_im, n_ctx):
    bsz, _, n_chunks, _ = s_loc.shape
    assert n_chunks % SUBLANES == 0 and n_ctx % SUBLANES == 0
    blk = (bsz, SCAN_PAIRS, n_chunks, PAIR_LANES)
    return pl.pallas_call(
        functools.partial(_scan_kernel, bsz=bsz, n_chunks=n_chunks, n_ctx=n_ctx),
        grid=(PAIRS // SCAN_PAIRS,),
        in_specs=[
            pl.BlockSpec(blk, lambda i: (0, i, 0, 0)),
            pl.BlockSpec((1, SCAN_PAIRS, SCAN_TILE, PAIR_LANES), lambda i: (0, i, 0, 0)),
            pl.BlockSpec((SCAN_PAIRS, 1, PAIR_LANES), lambda i: (i, 0, 0)),
            pl.BlockSpec((SCAN_PAIRS, 1, PAIR_LANES), lambda i: (i, 0, 0)),
        ],
        out_specs=pl.BlockSpec(blk, lambda i: (0, i, 0, 0)),
        out_shape=jax.ShapeDtypeStruct(s_loc.shape, _BF16),
        scratch_shapes=[pltpu.VMEM(blk, _F32),
                        pltpu.VMEM((2 * SCAN_PAIRS, 8, SUBLANES, LANE), _F32)],
        compiler_params=pltpu.CompilerParams(
            dimension_semantics=("arbitrary",), vmem_limit_bytes=VMEM_LIMIT),
        name="s5_chunk_scan",
    )(s_loc, s_ctx, a_re, a_im)


def _readout_kernel(uflat_ref, x_ref, mu_ref, mv_ref, y_ref, yt_ref, ys_ref):
    def pair_body(q, carry):
        y0 = _dot(uflat_ref[0, 2 * q], mu_ref[2 * q])
        y1 = _dot(uflat_ref[0, 2 * q + 1], mu_ref[2 * q + 1])
        yx = _dot(x_ref[0, q], mv_ref[q])
        yt = (jnp.concatenate([y0, y1], axis=1) + yx).T
        yt_ref[2 * q] = yt[:FLAT]
        yt_ref[2 * q + 1] = yt[FLAT:]
        return carry

    lax.fori_loop(0, PAIRS, pair_body, 0, unroll=2)

    def slab_body(sg, carry):
        r0 = pl.multiple_of(sg * GROUP_CH, GROUP_CH)
        for cb in range(COL_BLOCKS):
            yt = yt_ref[cb * 8:(cb + 1) * 8, pl.ds(r0, GROUP_CH), :].reshape(LANE, SCAN_TILE)
            ys_ref[cb, pl.ds(sg, SCAN_TILE, stride=CHUNK), :] = yt.T
        return carry

    lax.fori_loop(0, CHUNK, slab_body, 0, unroll=2)
    for cb in range(COL_BLOCKS):
        y_ref[0, :, cb * LANE:(cb + 1) * LANE] = ys_ref[cb]


def _readout(uflat, xstart, mu, mv):
    bsz, _, n_chunks, _ = uflat.shape
    nt = n_chunks // SCAN_TILE
    return pl.pallas_call(
        _readout_kernel,
        grid=(bsz, nt),
        in_specs=[
            pl.BlockSpec((1, GROUPS, SCAN_TILE, FLAT), lambda b, t: (b, 0, t, 0)),
            pl.BlockSpec((1, PAIRS, SCAN_TILE, PAIR_LANES), lambda b, t: (b, 0, t, 0)),
            _const_spec((GROUPS, FLAT, FLAT)),
            _const_spec((PAIRS, PAIR_LANES, 2 * FLAT)),
        ],
        out_specs=pl.BlockSpec((1, SCAN_ROWS, D_SSM), lambda b, t: (b, t, 0)),
        out_shape=jax.ShapeDtypeStruct((bsz, n_chunks * CHUNK, D_SSM), _F32),
        scratch_shapes=[
            pltpu.VMEM((GROUPS, FLAT, SCAN_TILE), _F32),
            pltpu.VMEM((COL_BLOCKS, SCAN_ROWS, LANE), _F32),
        ],
        compiler_params=pltpu.CompilerParams(
            dimension_semantics=("arbitrary", "arbitrary"), vmem_limit_bytes=VMEM_LIMIT),
        name="s5_readout",
    )(uflat, xstart, mu, mv)


def _mix_kernel(x_ref, y_ref, sh_ref, sc_ref, gt_ref, g_ref, wr_ref, wglu_ref, bglu_ref, wa_ref,
                pmat_ref, pinv_ref, pw_ref, ps_ref, wb_ref, wo_ref, o_ref):
    x = x_ref[0]
    h = _rms_mod(x, g_ref[...] * (1.0 + sc_ref[0]), sh_ref[0]).astype(_BF16)
    proj = _dot(h, wr_ref[:, D_SSM:])
    ub = proj[:, :D_POOL]
    gate_a = proj[:, D_POOL:D_POOL + D_MODEL]
    gate_b = proj[:, D_POOL + D_MODEL:]

    y = jax.nn.gelu(y_ref[0])
    z = y * jax.nn.sigmoid(_dot(y.astype(_BF16), wglu_ref[...]) + bglu_ref[...])
    ya = _dot(z.astype(_BF16), wa_ref[...])

    ub_hi = ub.astype(_BF16)
    ub_lo = (ub - ub_hi.astype(_F32)).astype(_BF16)
    outs = []
    for wi in range(len(POOL_WINDOWS)):
        sl = slice(wi * POOL_GROUP_CH, (wi + 1) * POOL_GROUP_CH)
        wsum = _dot(pmat_ref[wi], ub_hi[:, sl]) + _dot(pmat_ref[wi], ub_lo[:, sl])
        diff = wsum * pinv_ref[wi] - ub[:, sl]
        outs.append(_dot(diff.astype(_BF16), pw_ref[wi]))
    pb = jnp.concatenate(outs, axis=1) * ps_ref[...]
    yb = _dot(pb.astype(_BF16), wb_ref[...])

    merged = jax.nn.sigmoid(gate_a) * ya + jax.nn.sigmoid(gate_b) * yb
    mixed = _dot(merged.astype(_BF16), wo_ref[...])
    o_ref[0] = x + gt_ref[0] * mixed


def _pool_matrices(rows):
    tok = np.arange(rows)
    row, pos = tok // GRID_W, tok % GRID_W
    mats, invs = [], []
    for w in POOL_WINDOWS:
        lo = np.clip(pos - w // 2, 0, GRID_W - 1)
        hi = np.clip(pos + w - 1 - w // 2, 0, GRID_W - 1) + 1
        m = (row[:, None] == row[None, :]) & (pos[None, :] >= lo[:, None]) & (pos[None, :] < hi[:, None])
        mats.append(m.astype(np.float32))
        invs.append(np.broadcast_to((1.0 / (hi - lo).astype(np.float32))[:, None], (rows, POOL_GROUP_CH)))
    return np.stack(mats), np.stack(invs)


def _mixer(x, ypre, mod3, norm_g, w_in_b, w_glu, b_glu, w_a, pool_w, pool_scale, w_b, w_out):
    bsz, n_tok, _ = x.shape
    tm = MIX_ROWS
    pmat, pinv = _pool_matrices(tm)
    pmat = jnp.asarray(pmat, _BF16)
    pinv = jnp.asarray(pinv, _F32)
    nw = len(POOL_WINDOWS)
    return pl.pallas_call(
        _mix_kernel,
        grid=(bsz, n_tok // tm),
        in_specs=[
            pl.BlockSpec((1, tm, D_MODEL), lambda b, t: (b, t, 0)),
            pl.BlockSpec((1, tm, D_SSM), lambda b, t: (b, t, 0)),
            pl.BlockSpec((1, 1, D_MODEL), lambda b, t: (b, 0, 0)),
            pl.BlockSpec((1, 1, D_MODEL), lambda b, t: (b, 0, 1)),
            pl.BlockSpec((1, 1, D_MODEL), lambda b, t: (b, 0, 2)),
            _const_spec((1, D_MODEL)),
            _const_spec((D_MODEL, D_SSM + D_POOL + 2 * D_MODEL)),
            _const_spec((D_SSM, D_SSM)),
            _const_spec((1, D_SSM)),
            _const_spec((D_SSM, D_MODEL)),
            _const_spec((nw, tm, tm)),
            _const_spec((nw, tm, POOL_GROUP_CH)),
            _const_spec((nw, POOL_GROUP_CH, POOL_GROUP_CH)),
            _const_spec((1, D_POOL)),
            _const_spec((D_POOL, D_MODEL)),
            _const_spec((D_MODEL, D_MODEL)),
        ],
        out_specs=pl.BlockSpec((1, tm, D_MODEL), lambda b, t: (b, t, 0)),
        out_shape=jax.ShapeDtypeStruct(x.shape, _F32),
        compiler_params=pltpu.CompilerParams(
            dimension_semantics=("arbitrary", "arbitrary"), vmem_limit_bytes=VMEM_LIMIT),
        name="token_mixer",
    )(x, ypre, mod3, mod3, mod3, norm_g, w_in_b, w_glu, b_glu, w_a, pmat, pinv, pool_w, pool_scale,
      w_b, w_out)


def _ffn_kernel(x_ref, sh_ref, sc_ref, gt_ref, g2_ref, gf_ref, wg_ref, wu_ref, wo_ref, o_ref):
    x = x_ref[0]
    h = _rms_mod(x, g2_ref[...] * (1.0 + sc_ref[0]), sh_ref[0]).astype(_BF16)
    gate = _dot(h, wg_ref[...])
    up = _dot(h, wu_ref[...])
    act = (gate * jax.nn.sigmoid(gate) * up).astype(_BF16)
    y = x + gt_ref[0] * _dot(act, wo_ref[...])
    ms = jnp.mean(y * y, axis=-1, keepdims=True)
    o_ref[0] = (y * lax.rsqrt(ms + RMS_EPS)) * gf_ref[...]


def _ffn(x1, mod3, norm2_g, final_g, w_ffn, w_down):
    bsz, n_tok, _ = x1.shape
    tm = FFN_ROWS
    return pl.pallas_call(
        _ffn_kernel,
        grid=(bsz, n_tok // tm),
        in_specs=[
            pl.BlockSpec((1, tm, D_MODEL), lambda b, t: (b, t, 0)),
            pl.BlockSpec((1, 1, D_MODEL), lambda b, t: (b, 0, 3)),
            pl.BlockSpec((1, 1, D_MODEL), lambda b, t: (b, 0, 4)),
            pl.BlockSpec((1, 1, D_MODEL), lambda b, t: (b, 0, 5)),
            _const_spec((1, D_MODEL)),
            _const_spec((1, D_MODEL)),
            _const_spec((D_MODEL, FFN_HIDDEN), (0, 0)),
            _const_spec((D_MODEL, FFN_HIDDEN), (0, 1)),
            _const_spec((FFN_HIDDEN, D_MODEL)),
        ],
        out_specs=pl.BlockSpec((1, tm, D_MODEL), lambda b, t: (b, t, 0)),
        out_shape=jax.ShapeDtypeStruct(x1.shape, _F32),
        compiler_params=pltpu.CompilerParams(
            dimension_semantics=("arbitrary", "arbitrary"), vmem_limit_bytes=VMEM_LIMIT),
        name="swiglu_ffn",
    )(x1, mod3, mod3, mod3, norm2_g, final_g, w_ffn, w_ffn, w_down)


def _cmul(xr, xi, yr, yi):
    return xr * yr - xi * yi, xr * yi + xi * yr


def _tables_kernel(are_ref, aim_ref, ldt_ref, btr_ref, bti_ref, cr_ref, ci_ref, d_ref,
                   mu_ref, wst_ref, mv_ref, a16r_ref, a16i_ref, kt_ref, khl_ref, vnat_ref, cp_ref):
    nt_dims = (((1,), (1,)), ((), ()))

    def pair_body(q, carry):
        for d in range(2):
            ar = are_ref[d, q]
            ai = aim_ref[d, q]
            dt = jnp.exp(ldt_ref[d, q])
            mag = jnp.exp(ar * dt)
            ang = ai * dt
            abr, abi = mag * jnp.cos(ang), mag * jnp.sin(ang)
            den = ar * ar + ai * ai
            fr = ((abr - 1.0) * ar + abi * ai) / den
            fi = (abi * ar - (abr - 1.0) * ai) / den
            bbr, bbi = _cmul(btr_ref[d, q], bti_ref[d, q], fr, fi)
            cr = cr_ref[d, q]
            ci = ci_ref[d, q]
            pw = [(jnp.ones_like(ar), jnp.zeros_like(ar))]
            for _ in range(CHUNK):
                pw.append(_cmul(pw[-1][0], pw[-1][1], abr, abi))
            lanes_re = slice(2 * d * LANE, (2 * d + 1) * LANE)
            lanes_im = slice((2 * d + 1) * LANE, (2 * d + 2) * LANE)
            for lanes in (lanes_re, lanes_im):
                a16r_ref[q, :, lanes] = pw[CHUNK][0]
                a16i_ref[q, :, lanes] = pw[CHUNK][1]
            for sg in range(CHUNK):
                e = (CHUNK - 1 - sg) if d == 0 else sg
                wr, wi = _cmul(bbr, bbi, pw[e][0], pw[e][1])
                e = (sg + 1) if d == 0 else (CHUNK - sg)
                vr, vi = _cmul(cr, ci, pw[e][0], pw[e][1])
                for gg in range(2):
                    src = slice(gg * GROUP_CH, (gg + 1) * GROUP_CH)
                    dst = slice(gg * FLAT + sg * GROUP_CH, gg * FLAT + (sg + 1) * GROUP_CH)
                    wst_ref[q, dst, lanes_re] = wr[src].astype(_BF16)
                    wst_ref[q, dst, lanes_im] = wi[src].astype(_BF16)
                    vnat_ref[2 * d, dst, :] = vr[src]
                    vnat_ref[2 * d + 1, dst, :] = -vi[src]
            for gg in range(2):
                src = slice(gg * GROUP_CH, (gg + 1) * GROUP_CH)
                for k in range(CHUNK):
                    e = k if d == 0 else (CHUNK - 1 - k)
                    pr, pi = _cmul(cr[src], ci[src], pw[e][0], pw[e][1])
                    cp_ref[0, k * GROUP_CH:(k + 1) * GROUP_CH, :] = pr
                    cp_ref[1, k * GROUP_CH:(k + 1) * GROUP_CH, :] = pi
                kt = (lax.dot_general(bbr[src], cp_ref[0], nt_dims, precision=_HI, preferred_element_type=_F32)
                      - lax.dot_general(bbi[src], cp_ref[1], nt_dims, precision=_HI, preferred_element_type=_F32))
                r0 = pl.multiple_of((2 * q + gg) * GROUP_CH, GROUP_CH)
                kt_ref[d, pl.ds(r0, GROUP_CH), :] = kt
        for part in range(4):
            mv_ref[q, part * LANE:(part + 1) * LANE, :] = vnat_ref[part].T.astype(_BF16)
        return carry

    lax.fori_loop(0, PAIRS, pair_body, 0)

    for d in range(2):
        kt = kt_ref[d]
        hi = kt.astype(_BF16)
        khl_ref[2 * d] = hi
        khl_ref[2 * d + 1] = (kt - hi.astype(_F32)).astype(_BF16)

    row = lax.broadcasted_iota(jnp.int32, (FLAT, FLAT), 0)
    col = lax.broadcasted_iota(jnp.int32, (FLAT, FLAT), 1)
    same_ch = (row % GROUP_CH) == (col % GROUP_CH)
    row_blk = row // GROUP_CH
    col_blk = col // GROUP_CH
    orow = lax.broadcasted_iota(jnp.int32, (GROUPS * GROUP_CH, FLAT), 0)
    ocol = lax.broadcasted_iota(jnp.int32, (GROUPS * GROUP_CH, FLAT), 1)
    skip_ch = (orow % GROUP_CH) == (ocol % GROUP_CH)
    ocol_blk = ocol // GROUP_CH
    d_col = d_ref[...]

    def toeplitz_body(sg, carry):
        sf = jnp.where(same_ch & (row_blk + sg == col_blk), 1.0, 0.0).astype(_BF16)
        sb = jnp.where(same_ch & (row_blk == col_blk + (CHUNK - 1) - sg), 1.0, 0.0).astype(_BF16)
        out = (_dot(khl_ref[0], sf) + _dot(khl_ref[1], sf)) + (_dot(khl_ref[2], sb) + _dot(khl_ref[3], sb))
        out = out + jnp.where(skip_ch & (ocol_blk == sg), d_col, 0.0)
        r0 = pl.multiple_of(sg * GROUP_CH, GROUP_CH)
        mu_ref[:, pl.ds(r0, GROUP_CH), :] = out.reshape(GROUPS, GROUP_CH, FLAT).astype(_BF16)
        return carry

    lax.fori_loop(0, CHUNK, toeplitz_body, 0)


def _s5_tables(a_re, a_im, log_dt, b_re, b_im, c_re, c_im, d_skip):
    f32 = _F32
    eye2 = jnp.eye(2, dtype=f32)

    def pair_rows(v):
        return v.astype(f32).reshape(2, PAIRS, 1, 2 * STATE)

    def pair_blocks(v):
        v = v.astype(f32).reshape(2, PAIRS, 2, GROUP_CH, 1, STATE)
        v = v * eye2[None, None, :, None, :, None]
        return v.reshape(2, PAIRS, 2 * GROUP_CH, 2 * STATE)

    ldt = jnp.broadcast_to(log_dt.astype(f32)[..., None], (2, GROUPS, STATE))
    args = (pair_rows(a_re), pair_rows(a_im), pair_rows(ldt),
            pair_blocks(jnp.swapaxes(b_re, 2, 3)), pair_blocks(jnp.swapaxes(b_im, 2, 3)),
            pair_blocks(c_re), pair_blocks(c_im), d_skip.astype(f32).reshape(D_SSM, 1))
    whole = lambda a: pl.BlockSpec(a.shape, lambda i, n=a.ndim: (0,) * n)
    out_shape = [
        jax.ShapeDtypeStruct((GROUPS, FLAT, FLAT), _BF16),
        jax.ShapeDtypeStruct((PAIRS, 2 * FLAT, PAIR_LANES), _BF16),
        jax.ShapeDtypeStruct((PAIRS, PAIR_LANES, 2 * FLAT), _BF16),
        jax.ShapeDtypeStruct((PAIRS, 1, PAIR_LANES), _F32),
        jax.ShapeDtypeStruct((PAIRS, 1, PAIR_LANES), _F32),
    ]
    return pl.pallas_call(
        _tables_kernel,
        grid=(1,),
        in_specs=[whole(a) for a in args],
        out_specs=[whole(s) for s in out_shape],
        out_shape=out_shape,
        scratch_shapes=[
            pltpu.VMEM((2, GROUPS * GROUP_CH, FLAT), _F32),
            pltpu.VMEM((4, GROUPS * GROUP_CH, FLAT), _BF16),
            pltpu.VMEM((4, 2 * FLAT, LANE), _F32),
            pltpu.VMEM((2, FLAT, LANE), _F32),
        ],
        compiler_params=pltpu.CompilerParams(
            dimension_semantics=("arbitrary",), vmem_limit_bytes=VMEM_LIMIT),
        name="s5_tables",
    )(*args)


def kernel(x, c, ctx, c_ctx, w_mod, b_mod, norm1_g, norm2_g, w_in, s5_a_re, s5_a_im, s5_log_dt,
           s5_b_re, s5_b_im, s5_c_re, s5_c_im, s5_d, w_glu, b_glu, pool_w, pool_scale,
           w_branch_a, w_branch_b, w_out, w_ffn_in, w_ffn_out, final_norm_g):
    bsz, n_tok, d = x.shape
    ctx_len = ctx.shape[1]
    assert d == D_MODEL and w_mod.shape[0] == 1 and bsz + 1 <= 8
    assert n_tok % SCAN_ROWS == 0 and n_tok % MIX_ROWS == 0 and MIX_ROWS % GRID_W == 0
    assert ctx_len % CHUNK == 0 and bsz * ctx_len <= SCAN_ROWS
    n_ctx = ctx_len // CHUNK

    cc = jnp.zeros((8, D_MODEL), _F32).at[:bsz].set(c).at[bsz].set(c_ctx)
    mod3 = _modulation(cc, w_mod[0], b_mod[0]).reshape(8, 1, 6 * D_MODEL)

    mu, wst, mv, a16_re, a16_im = _s5_tables(
        s5_a_re[0], s5_a_im[0], s5_log_dt[0], s5_b_re[0], s5_b_im[0], s5_c_re[0], s5_c_im[0], s5_d[0])

    w_in_b = w_in[0].astype(_BF16)
    n1 = norm1_g[0].reshape(1, D_MODEL)

    uflat, s_loc = _pass1(x, mod3, 0, n1, w_in_b, wst)
    _, s_ctx = _pass1(ctx.reshape(1, bsz * ctx_len, D_MODEL), mod3, bsz, n1, w_in_b, wst)
    xstart = _chunk_scan(s_loc, s_ctx, a16_re, a16_im, n_ctx)
    ypre = _readout(uflat, xstart, mu, mv)

    x1 = _mixer(x, ypre, mod3, n1, w_in_b, w_glu[0].astype(_BF16), b_glu[0].reshape(1, D_SSM),
                w_branch_a[0].astype(_BF16), pool_w[0].astype(_BF16), pool_scale[0].reshape(1, D_POOL),
                w_branch_b[0].astype(_BF16), w_out[0].astype(_BF16))

    return _ffn(x1, mod3, norm2_g[0].reshape(1, D_MODEL), final_norm_g.reshape(1, D_MODEL),
                w_ffn_in[0].astype(_BF16), w_ffn_out[0].astype(_BF16))
```

```python
import functools

import numpy as np
import jax
import jax.numpy as jnp
from jax import lax
from jax.experimental import pallas as pl
from jax.experimental.pallas import tpu as pltpu

_F32 = jnp.float32
_BF16 = jnp.bfloat16

D_MODEL = 1024
D_SSM = 512
D_POOL = 512
GROUPS = 32
STATE = 64
GROUP_CH = 16
CHUNK = 16
FLAT = CHUNK * GROUP_CH
PAIRS = GROUPS // 2
PAIR_LANES = 4 * 2 * STATE
LANE = 128
SUBLANES = 8
COL_BLOCKS = D_SSM // LANE
GRID_W = 64
POOL_WINDOWS = (2, 4, 8, 16)
POOL_GROUP_CH = D_POOL // len(POOL_WINDOWS)
FFN_HIDDEN = 2816
RMS_EPS = 1e-6

SCAN_TILE = 128
SCAN_ROWS = SCAN_TILE * CHUNK
NORM_ROWS = 512
MIX_ROWS = 512
FFN_ROWS = 512
SCAN_PAIRS = 4
RELAYOUT_UNROLL = 8
VMEM_LIMIT = 56 * 1024 * 1024

_HI = lax.Precision.HIGHEST


def _rms_mod(x, gain, sh):
    ms = jnp.mean(x * x, axis=-1, keepdims=True)
    return (x * lax.rsqrt(ms + RMS_EPS)) * gain + sh


def _dot(a, b):
    return jnp.dot(a, b, preferred_element_type=_F32)


def _const_spec(shape, index=None):
    index = (0,) * len(shape) if index is None else index
    return pl.BlockSpec(shape, lambda *_: index, pipeline_mode=pl.Buffered(1))


def _mod_kernel(ct_ref, w_ref, b_ref, o_ref, *, n_rows):
    ct = ct_ref[...]
    a = ct * jax.nn.sigmoid(ct)
    w = w_ref[...]
    rows = [jnp.sum(w * a[:, r:r + 1], axis=0, keepdims=True) for r in range(n_rows)]
    rows.append(jnp.zeros((SUBLANES - n_rows, w.shape[1]), _F32))
    o_ref[...] = jnp.concatenate(rows, axis=0) + b_ref[...]


def _modulation(cc_t, w_mod, b_mod, n_rows):
    n_out = w_mod.shape[1]
    blk = 1024
    return pl.pallas_call(
        functools.partial(_mod_kernel, n_rows=n_rows),
        grid=(n_out // blk,),
        in_specs=[
            pl.BlockSpec((D_MODEL, SUBLANES), lambda i: (0, 0)),
            pl.BlockSpec((D_MODEL, blk), lambda i: (0, i)),
            pl.BlockSpec((1, blk), lambda i: (0, i)),
        ],
        out_specs=pl.BlockSpec((SUBLANES, blk), lambda i: (0, i)),
        out_shape=jax.ShapeDtypeStruct((SUBLANES, n_out), _F32),
        compiler_params=pltpu.CompilerParams(
            dimension_semantics=("arbitrary",), vmem_limit_bytes=VMEM_LIMIT),
        name="adaln_mod",
    )(cc_t, w_mod, b_mod.reshape(1, n_out))


def _p1_kernel(x_ref, sh_ref, sc_ref, g_ref, wa_ref, wst_ref, uflat_ref, s_ref, h_ref, u_ref, ut_ref,
               *, rows):
    gain = g_ref[...] * (1.0 + sc_ref[0])
    sh = sh_ref[0]

    def norm_block(i):
        r = slice(i * NORM_ROWS, (i + 1) * NORM_ROWS)
        h_ref[r, :] = _rms_mod(x_ref[0, r, :], gain, sh).astype(_BF16)

    norm_block(0)
    for i in range(rows // NORM_ROWS):
        if (i + 1) * NORM_ROWS < rows:
            norm_block(i + 1)
        r = slice(i * NORM_ROWS, (i + 1) * NORM_ROWS)
        u = _dot(h_ref[r, :], wa_ref[...])
        for cb in range(COL_BLOCKS):
            u_ref[cb, r, :] = u[:, cb * LANE:(cb + 1) * LANE]
    if rows < SCAN_ROWS:
        for cb in range(COL_BLOCKS):
            u_ref[cb, rows:, :] = jnp.zeros((SCAN_ROWS - rows, LANE), _F32)

    def slab_body(sg, carry):
        r0 = pl.multiple_of(sg * GROUP_CH, GROUP_CH)
        for cb in range(COL_BLOCKS):
            slab = u_ref[cb, pl.ds(sg, SCAN_TILE, stride=CHUNK), :]
            ut_ref[cb * 8:(cb + 1) * 8, pl.ds(r0, GROUP_CH), :] = slab.T.reshape(8, GROUP_CH, SCAN_TILE)
        return carry

    lax.fori_loop(0, CHUNK, slab_body, 0, unroll=RELAYOUT_UNROLL)

    def pair_body(q, carry):
        uf0 = ut_ref[2 * q].T.astype(_BF16)
        uf1 = ut_ref[2 * q + 1].T.astype(_BF16)
        uflat_ref[0, 2 * q] = uf0
        uflat_ref[0, 2 * q + 1] = uf1
        s_ref[0, q] = _dot(uf0, wst_ref[q, :FLAT, :]) + _dot(uf1, wst_ref[q, FLAT:, :])
        return carry

    lax.fori_loop(0, PAIRS, pair_body, 0, unroll=RELAYOUT_UNROLL)


def _pass1(x, mod3, mod_row0, norm_g, w_in_b, wst):
    bsz, n_tok, _ = x.shape
    rows = min(n_tok, SCAN_ROWS)
    assert n_tok % rows == 0 and rows % NORM_ROWS == 0
    nt = n_tok // rows
    n_chunks = nt * SCAN_TILE
    return pl.pallas_call(
        functools.partial(_p1_kernel, rows=rows),
        grid=(bsz, nt),
        in_specs=[
            pl.BlockSpec((1, rows, D_MODEL), lambda b, t: (b, t, 0)),
            pl.BlockSpec((1, 1, D_MODEL), lambda b, t: (b + mod_row0, 0, 0)),
            pl.BlockSpec((1, 1, D_MODEL), lambda b, t: (b + mod_row0, 0, 1)),
            _const_spec((1, D_MODEL)),
            _const_spec((D_MODEL, D_SSM)),
            _const_spec((PAIRS, 2 * FLAT, PAIR_LANES)),
        ],
        out_specs=[
            pl.BlockSpec((1, GROUPS, SCAN_TILE, FLAT), lambda b, t: (b, 0, t, 0)),
            pl.BlockSpec((1, PAIRS, SCAN_TILE, PAIR_LANES), lambda b, t: (b, 0, t, 0)),
        ],
        out_shape=[
            jax.ShapeDtypeStruct((bsz, GROUPS, n_chunks, FLAT), _BF16),
            jax.ShapeDtypeStruct((bsz, PAIRS, n_chunks, PAIR_LANES), _F32),
        ],
        scratch_shapes=[
            pltpu.VMEM((SCAN_ROWS, D_MODEL), _BF16),
            pltpu.VMEM((COL_BLOCKS, SCAN_ROWS, LANE), _F32),
            pltpu.VMEM((GROUPS, FLAT, SCAN_TILE), _F32),
        ],
        compiler_params=pltpu.CompilerParams(
            dimension_semantics=("arbitrary", "arbitrary"), vmem_limit_bytes=VMEM_LIMIT),
        name="s5_chunk_states",
    )(x, mod3, mod3, norm_g, w_in_b, wst)


def _scan_kernel(s_ref, sc_ref, ar_ref, ai_ref, x_ref, xs_ref, c_ref, *, bsz, n_chunks, n_ctx):
    rb = SUBLANES
    rid = lax.broadcasted_iota(jnp.int32, (rb, LANE), 0)
    chains = [(b, qq) for b in range(bsz) for qq in range(SCAN_PAIRS)]

    for qq in range(SCAN_PAIRS):
        for d in range(2):
            o = d * 2 * LANE
            a_r = ar_ref[qq][:, o:o + LANE]
            a_i = ai_ref[qq][:, o:o + LANE]
            pows = [(a_r, a_i)]
            for _ in range(rb - 1):
                pows.append(_cmul(pows[-1][0], pows[-1][1], a_r, a_i))
            idx = qq * 2 + d
            for k, shift in enumerate((1, 2, 4)):
                keep = (rid >= shift) if d == 0 else (rid < rb - shift)
                c_ref[idx, 2 * k] = jnp.where(keep, pows[shift - 1][0], 0.0)
                c_ref[idx, 2 * k + 1] = jnp.where(keep, pows[shift - 1][1], 0.0)
            p_r = jnp.zeros((rb, LANE), _F32)
            p_i = jnp.zeros((rb, LANE), _F32)
            for r in range(rb):
                e = r if d == 0 else rb - 1 - r
                p_r = jnp.where(rid == r, pows[e][0], p_r)
                p_i = jnp.where(rid == r, pows[e][1], p_i)
            c_ref[idx, 6] = p_r
            c_ref[idx, 7] = p_i

    def block(idx, d, s_re, s_im, xin_re, xin_im):
        def shifted(v, k):
            return pltpu.roll(v, k if d == 0 else rb - k, 0)

        t_re, t_im = s_re, s_im
        for k in range(3):
            a_r = c_ref[idx, 2 * k]
            a_i = c_ref[idx, 2 * k + 1]
            u_re, u_im = shifted(t_re, 1 << k), shifted(t_im, 1 << k)
            t_re, t_im = t_re + (a_r * u_re - a_i * u_im), t_im + (a_r * u_im + a_i * u_re)
        p_r = c_ref[idx, 6]
        p_i = c_ref[idx, 7]
        after_re = t_re + (p_r * xin_re - p_i * xin_im)
        after_im = t_im + (p_r * xin_im + p_i * xin_re)
        first = 0 if d == 0 else rb - 1
        last = rb - 1 - first
        start_re = jnp.where(rid == first, xin_re, shifted(after_re, 1))
        start_im = jnp.where(rid == first, xin_im, shifted(after_im, 1))
        return start_re, start_im, after_re[last:last + 1], after_im[last:last + 1]

    def sweep(src_ref, lead, row0, n_blocks, i, carry, dst_ref):
        out = []
        for ci, (b, qq) in enumerate(chains):
            for d in range(2):
                blk = i if d == 0 else n_blocks - 1 - i
                r0 = row0(b) + blk * rb
                if not isinstance(r0, int):
                    r0 = pl.multiple_of(r0, rb)
                o = d * 2 * LANE
                bi = lead(b)
                s_re = src_ref[bi, qq, pl.ds(r0, rb), o:o + LANE]
                s_im = src_ref[bi, qq, pl.ds(r0, rb), o + LANE:o + 2 * LANE]
                xin_re, xin_im = carry[4 * ci + 2 * d], carry[4 * ci + 2 * d + 1]
                st_re, st_im, xo_re, xo_im = block(qq * 2 + d, d, s_re, s_im, xin_re, xin_im)
                if dst_ref is not None:
                    dst_ref[bi, qq, pl.ds(r0, rb), o:o + LANE] = st_re
                    dst_ref[bi, qq, pl.ds(r0, rb), o + LANE:o + 2 * LANE] = st_im
                out += [xo_re, xo_im]
        return tuple(out)

    carry = tuple(jnp.zeros((1, LANE), _F32) for _ in range(4 * len(chains)))
    ctx_blocks = n_ctx // rb
    for i in range(ctx_blocks):
        carry = sweep(sc_ref, lambda b: 0, lambda b: b * n_ctx, ctx_blocks, i, carry, None)

    n_blocks = n_chunks // rb
    lax.fori_loop(
        0, n_blocks,
        lambda i, c: sweep(s_ref, lambda b: b, lambda b: 0, n_blocks, i, c, xs_ref),
        carry)
    x_ref[...] = xs_ref[...].astype(_BF16)


def _chunk_scan(s_loc, s_ctx, a_re, a_im, n_ctx):
    bsz, _, n_chunks, _ = s_loc.shape
    assert n_chunks % SUBLANES == 0 and n_ctx % SUBLANES == 0
    blk = (bsz, SCAN_PAIRS, n_chunks, PAIR_LANES)
    return pl.pallas_call(
        functools.partial(_scan_kernel, bsz=bsz, n_chunks=n_chunks, n_ctx=n_ctx),
        grid=(PAIRS // SCAN_PAIRS,),
        in_specs=[
            pl.BlockSpec(blk, lambda i: (0, i, 0, 0)),
            pl.BlockSpec((1, SCAN_PAIRS, SCAN_TILE, PAIR_LANES), lambda i: (0, i, 0, 0)),
            pl.BlockSpec((SCAN_PAIRS, 1, PAIR_LANES), lambda i: (i, 0, 0)),
            pl.BlockSpec((SCAN_PAIRS, 1, PAIR_LANES), lambda i: (i, 0, 0)),
        ],
        out_specs=pl.BlockSpec(blk, lambda i: (0, i, 0, 0)),
        out_shape=jax.ShapeDtypeStruct(s_loc.shape, _BF16),
        scratch_shapes=[pltpu.VMEM(blk, _F32),
                        pltpu.VMEM((2 * SCAN_PAIRS, 8, SUBLANES, LANE), _F32)],
        compiler_params=pltpu.CompilerParams(
            dimension_semantics=("arbitrary",), vmem_limit_bytes=VMEM_LIMIT),
        name="s5_chunk_scan",
    )(s_loc, s_ctx, a_re, a_im)


def _readout_kernel(uflat_ref, x_ref, mu_ref, mv_ref, y_ref, yt_ref, ys_ref):
    def pair_body(q, carry):
        y0 = _dot(uflat_ref[0, 2 * q], mu_ref[2 * q])
        y1 = _dot(uflat_ref[0, 2 * q + 1], mu_ref[2 * q + 1])
        yx = _dot(x_ref[0, q], mv_ref[q])
        yt = (jnp.concatenate([y0, y1], axis=1) + yx).T
        yt_ref[2 * q] = yt[:FLAT]
        yt_ref[2 * q + 1] = yt[FLAT:]
        return carry

    lax.fori_loop(0, PAIRS, pair_body, 0, unroll=RELAYOUT_UNROLL)

    def slab_body(sg, carry):
        r0 = pl.multiple_of(sg * GROUP_CH, GROUP_CH)
        for cb in range(COL_BLOCKS):
            yt = yt_ref[cb * 8:(cb + 1) * 8, pl.ds(r0, GROUP_CH), :].reshape(LANE, SCAN_TILE)
            ys_ref[cb, pl.ds(sg, SCAN_TILE, stride=CHUNK), :] = yt.T
        return carry

    lax.fori_loop(0, CHUNK, slab_body, 0, unroll=RELAYOUT_UNROLL)
    for cb in range(COL_BLOCKS):
        y_ref[0, :, cb * LANE:(cb + 1) * LANE] = ys_ref[cb]


def _readout(uflat, xstart, mu, mv):
    bsz, _, n_chunks, _ = uflat.shape
    nt = n_chunks // SCAN_TILE
    return pl.pallas_call(
        _readout_kernel,
        grid=(bsz, nt),
        in_specs=[
            pl.BlockSpec((1, GROUPS, SCAN_TILE, FLAT), lambda b, t: (b, 0, t, 0)),
            pl.BlockSpec((1, PAIRS, SCAN_TILE, PAIR_LANES), lambda b, t: (b, 0, t, 0)),
            _const_spec((GROUPS, FLAT, FLAT)),
            _const_spec((PAIRS, PAIR_LANES, 2 * FLAT)),
        ],
        out_specs=pl.BlockSpec((1, SCAN_ROWS, D_SSM), lambda b, t: (b, t, 0)),
        out_shape=jax.ShapeDtypeStruct((bsz, n_chunks * CHUNK, D_SSM), _F32),
        scratch_shapes=[
            pltpu.VMEM((GROUPS, FLAT, SCAN_TILE), _F32),
            pltpu.VMEM((COL_BLOCKS, SCAN_ROWS, LANE), _F32),
        ],
        compiler_params=pltpu.CompilerParams(
            dimension_semantics=("arbitrary", "arbitrary"), vmem_limit_bytes=VMEM_LIMIT),
        name="s5_readout",
    )(uflat, xstart, mu, mv)


def _mix_kernel(x_ref, y_ref, sh_ref, sc_ref, gt_ref, g_ref, wr_ref, wglu_ref, bglu_ref, wa_ref,
                pmat_ref, pinv_ref, pw_ref, ps_ref, wb_ref, wo_ref, o_ref):
    x = x_ref[0]
    h = _rms_mod(x, g_ref[...] * (1.0 + sc_ref[0]), sh_ref[0]).astype(_BF16)
    c_pool, c_ga, c_gb = D_SSM, D_SSM + D_POOL, D_SSM + D_POOL + D_MODEL
    windows = range(len(POOL_WINDOWS))
    group = lambda wi: slice(wi * POOL_GROUP_CH, (wi + 1) * POOL_GROUP_CH)

    ub = _dot(h, wr_ref[:, c_pool:c_ga])
    y = jax.nn.gelu(y_ref[0])
    glu = _dot(y.astype(_BF16), wglu_ref[...])
    ub_hi = ub.astype(_BF16)
    ub_lo = (ub - ub_hi.astype(_F32)).astype(_BF16)
    wsums = [_dot(pmat_ref[wi], ub_hi[:, group(wi)]) + _dot(pmat_ref[wi], ub_lo[:, group(wi)])
             for wi in windows]
    gate_a = _dot(h, wr_ref[:, c_ga:c_gb])
    z = y * jax.nn.sigmoid(glu + bglu_ref[...])
    ya = _dot(z.astype(_BF16), wa_ref[...])
    outs = [_dot((wsums[wi] * pinv_ref[wi] - ub[:, group(wi)]).astype(_BF16), pw_ref[wi])
            for wi in windows]
    gate_b = _dot(h, wr_ref[:, c_gb:])
    pb = jnp.concatenate(outs, axis=1) * ps_ref[...]
    yb = _dot(pb.astype(_BF16), wb_ref[...])

    merged = jax.nn.sigmoid(gate_a) * ya + jax.nn.sigmoid(gate_b) * yb
    mixed = _dot(merged.astype(_BF16), wo_ref[...])
    o_ref[0] = x + gt_ref[0] * mixed


def _pool_matrices(rows):
    tok = np.arange(rows)
    row, pos = tok // GRID_W, tok % GRID_W
    mats, invs = [], []
    for w in POOL_WINDOWS:
        lo = np.clip(pos - w // 2, 0, GRID_W - 1)
        hi = np.clip(pos + w - 1 - w // 2, 0, GRID_W - 1) + 1
        m = (row[:, None] == row[None, :]) & (pos[None, :] >= lo[:, None]) & (pos[None, :] < hi[:, None])
        mats.append(m.astype(np.float32))
        invs.append(np.broadcast_to((1.0 / (hi - lo).astype(np.float32))[:, None], (rows, POOL_GROUP_CH)))
    return np.stack(mats), np.stack(invs)


def _mixer(x, ypre, mod3, norm_g, w_in_b, w_glu, b_glu, w_a, pool_w, pool_scale, w_b, w_out):
    bsz, n_tok, _ = x.shape
    tm = MIX_ROWS
    pmat, pinv = _pool_matrices(tm)
    pmat = jnp.asarray(pmat, _BF16)
    pinv = jnp.asarray(pinv, _F32)
    nw = len(POOL_WINDOWS)
    return pl.pallas_call(
        _mix_kernel,
        grid=(bsz, n_tok // tm),
        in_specs=[
            pl.BlockSpec((1, tm, D_MODEL), lambda b, t: (b, t, 0)),
            pl.BlockSpec((1, tm, D_SSM), lambda b, t: (b, t, 0)),
            pl.BlockSpec((1, 1, D_MODEL), lambda b, t: (b, 0, 0)),
            pl.BlockSpec((1, 1, D_MODEL), lambda b, t: (b, 0, 1)),
            pl.BlockSpec((1, 1, D_MODEL), lambda b, t: (b, 0, 2)),
            _const_spec((1, D_MODEL)),
            _const_spec((D_MODEL, D_SSM + D_POOL + 2 * D_MODEL)),
            _const_spec((D_SSM, D_SSM)),
            _const_spec((1, D_SSM)),
            _const_spec((D_SSM, D_MODEL)),
            _const_spec((nw, tm, tm)),
            _const_spec((nw, tm, POOL_GROUP_CH)),
            _const_spec((nw, POOL_GROUP_CH, POOL_GROUP_CH)),
            _const_spec((1, D_POOL)),
            _const_spec((D_POOL, D_MODEL)),
            _const_spec((D_MODEL, D_MODEL)),
        ],
        out_specs=pl.BlockSpec((1, tm, D_MODEL), lambda b, t: (b, t, 0)),
        out_shape=jax.ShapeDtypeStruct(x.shape, _F32),
        compiler_params=pltpu.CompilerParams(
            dimension_semantics=("arbitrary", "arbitrary"), vmem_limit_bytes=VMEM_LIMIT),
        name="token_mixer",
    )(x, ypre, mod3, mod3, mod3, norm_g, w_in_b, w_glu, b_glu, w_a, pmat, pinv, pool_w, pool_scale,
      w_b, w_out)


def _ffn_kernel(x_ref, sh_ref, sc_ref, gt_ref, g2_ref, gf_ref, wg_ref, wu_ref, wo_ref, o_ref):
    x = x_ref[0]
    h = _rms_mod(x, g2_ref[...] * (1.0 + sc_ref[0]), sh_ref[0]).astype(_BF16)
    gate = _dot(h, wg_ref[...])
    up = _dot(h, wu_ref[...])
    act = (gate * jax.nn.sigmoid(gate) * up).astype(_BF16)
    y = x + gt_ref[0] * _dot(act, wo_ref[...])
    ms = jnp.mean(y * y, axis=-1, keepdims=True)
    o_ref[0] = (y * lax.rsqrt(ms + RMS_EPS)) * gf_ref[...]


def _ffn(x1, mod3, norm2_g, final_g, w_ffn, w_down):
    bsz, n_tok, _ = x1.shape
    tm = FFN_ROWS
    return pl.pallas_call(
        _ffn_kernel,
        grid=(bsz, n_tok // tm),
        in_specs=[
            pl.BlockSpec((1, tm, D_MODEL), lambda b, t: (b, t, 0)),
            pl.BlockSpec((1, 1, D_MODEL), lambda b, t: (b, 0, 3)),
            pl.BlockSpec((1, 1, D_MODEL), lambda b, t: (b, 0, 4)),
            pl.BlockSpec((1, 1, D_MODEL), lambda b, t: (b, 0, 5)),
            _const_spec((1, D_MODEL)),
            _const_spec((1, D_MODEL)),
            _const_spec((D_MODEL, FFN_HIDDEN), (0, 0)),
            _const_spec((D_MODEL, FFN_HIDDEN), (0, 1)),
            _const_spec((FFN_HIDDEN, D_MODEL)),
        ],
        out_specs=pl.BlockSpec((1, tm, D_MODEL), lambda b, t: (b, t, 0)),
        out_shape=jax.ShapeDtypeStruct(x1.shape, _F32),
        compiler_params=pltpu.CompilerParams(
            dimension_semantics=("arbitrary", "arbitrary"), vmem_limit_bytes=VMEM_LIMIT),
        name="swiglu_ffn",
    )(x1, mod3, mod3, mod3, norm2_g, final_g, w_ffn, w_ffn, w_down)


def _cmul(xr, xi, yr, yi):
    return xr * yr - xi * yi, xr * yi + xi * yr


def _tables_kernel(are_ref, aim_ref, ldt_ref, btr_ref, bti_ref, cr_ref, ci_ref, d_ref,
                   mu_ref, wst_ref, mv_ref, a16r_ref, a16i_ref, kt_ref, khl_ref, vnat_ref, cp_ref):
    nt_dims = (((1,), (1,)), ((), ()))

    def pair_body(q, carry):
        for d in range(2):
            ar = are_ref[d, q]
            ai = aim_ref[d, q]
            dt = jnp.exp(ldt_ref[d, q])
            mag = jnp.exp(ar * dt)
            ang = ai * dt
            abr, abi = mag * jnp.cos(ang), mag * jnp.sin(ang)
            den = ar * ar + ai * ai
            fr = ((abr - 1.0) * ar + abi * ai) / den
            fi = (abi * ar - (abr - 1.0) * ai) / den
            bbr, bbi = _cmul(btr_ref[d, q], bti_ref[d, q], fr, fi)
            cr = cr_ref[d, q]
            ci = ci_ref[d, q]
            pw = [(jnp.ones_like(ar), jnp.zeros_like(ar))]
            for _ in range(CHUNK):
                pw.append(_cmul(pw[-1][0], pw[-1][1], abr, abi))
            lanes_re = slice(2 * d * LANE, (2 * d + 1) * LANE)
            lanes_im = slice((2 * d + 1) * LANE, (2 * d + 2) * LANE)
            for lanes in (lanes_re, lanes_im):
                a16r_ref[q, :, lanes] = pw[CHUNK][0]
                a16i_ref[q, :, lanes] = pw[CHUNK][1]
            for sg in range(CHUNK):
                e = (CHUNK - 1 - sg) if d == 0 else sg
                wr, wi = _cmul(bbr, bbi, pw[e][0], pw[e][1])
                e = (sg + 1) if d == 0 else (CHUNK - sg)
                vr, vi = _cmul(cr, ci, pw[e][0], pw[e][1])
                for gg in range(2):
                    src = slice(gg * GROUP_CH, (gg + 1) * GROUP_CH)
                    dst = slice(gg * FLAT + sg * GROUP_CH, gg * FLAT + (sg + 1) * GROUP_CH)
                    wst_ref[q, dst, lanes_re] = wr[src].astype(_BF16)
                    wst_ref[q, dst, lanes_im] = wi[src].astype(_BF16)
                    vnat_ref[2 * d, dst, :] = vr[src]
                    vnat_ref[2 * d + 1, dst, :] = -vi[src]
            for gg in range(2):
                src = slice(gg * GROUP_CH, (gg + 1) * GROUP_CH)
                for k in range(CHUNK):
                    e = k if d == 0 else (CHUNK - 1 - k)
                    pr, pi = _cmul(cr[src], ci[src], pw[e][0], pw[e][1])
                    cp_ref[0, k * GROUP_CH:(k + 1) * GROUP_CH, :] = pr
                    cp_ref[1, k * GROUP_CH:(k + 1) * GROUP_CH, :] = pi
                kt = (lax.dot_general(bbr[src], cp_ref[0], nt_dims, precision=_HI, preferred_element_type=_F32)
                      - lax.dot_general(bbi[src], cp_ref[1], nt_dims, precision=_HI, preferred_element_type=_F32))
                r0 = pl.multiple_of((2 * q + gg) * GROUP_CH, GROUP_CH)
                kt_ref[d, pl.ds(r0, GROUP_CH), :] = kt
        for part in range(4):
            mv_ref[q, part * LANE:(part + 1) * LANE, :] = vnat_ref[part].T.astype(_BF16)
        return carry

    lax.fori_loop(0, PAIRS, pair_body, 0)

    for d in range(2):
        kt = kt_ref[d]
        hi = kt.astype(_BF16)
        khl_ref[2 * d] = hi
        khl_ref[2 * d + 1] = (kt - hi.astype(_F32)).astype(_BF16)

    row = lax.broadcasted_iota(jnp.int32, (FLAT, FLAT), 0)
    col = lax.broadcasted_iota(jnp.int32, (FLAT, FLAT), 1)
    same_ch = (row % GROUP_CH) == (col % GROUP_CH)
    row_blk = row // GROUP_CH
    col_blk = col // GROUP_CH
    orow = lax.broadcasted_iota(jnp.int32, (GROUPS * GROUP_CH, FLAT), 0)
    ocol = lax.broadcasted_iota(jnp.int32, (GROUPS * GROUP_CH, FLAT), 1)
    skip_ch = (orow % GROUP_CH) == (ocol % GROUP_CH)
    ocol_blk = ocol // GROUP_CH
    d_col = d_ref[...]

    def toeplitz_body(sg, carry):
        sf = jnp.where(same_ch & (row_blk + sg == col_blk), 1.0, 0.0).astype(_BF16)
        sb = jnp.where(same_ch & (row_blk == col_blk + (CHUNK - 1) - sg), 1.0, 0.0).astype(_BF16)
        out = (_dot(khl_ref[0], sf) + _dot(khl_ref[1], sf)) + (_dot(khl_ref[2], sb) + _dot(khl_ref[3], sb))
        out = out + jnp.where(skip_ch & (ocol_blk == sg), d_col, 0.0)
        r0 = pl.multiple_of(sg * GROUP_CH, GROUP_CH)
        mu_ref[:, pl.ds(r0, GROUP_CH), :] = out.reshape(GROUPS, GROUP_CH, FLAT).astype(_BF16)
        return carry

    lax.fori_loop(0, CHUNK, toeplitz_body, 0)


def _s5_tables(a_re, a_im, log_dt, b_re, b_im, c_re, c_im, d_skip):
    f32 = _F32
    eye2 = jnp.eye(2, dtype=f32)

    def pair_rows(v):
        return v.astype(f32).reshape(2, PAIRS, 1, 2 * STATE)

    def pair_blocks(v):
        v = v.astype(f32).reshape(2, PAIRS, 2, GROUP_CH, 1, STATE)
        v = v * eye2[None, None, :, None, :, None]
        return v.reshape(2, PAIRS, 2 * GROUP_CH, 2 * STATE)

    ldt = jnp.broadcast_to(log_dt.astype(f32)[..., None], (2, GROUPS, STATE))
    args = (pair_rows(a_re), pair_rows(a_im), pair_rows(ldt),
            pair_blocks(jnp.swapaxes(b_re, 2, 3)), pair_blocks(jnp.swapaxes(b_im, 2, 3)),
            pair_blocks(c_re), pair_blocks(c_im), d_skip.astype(f32).reshape(D_SSM, 1))
    whole = lambda a: pl.BlockSpec(a.shape, lambda i, n=a.ndim: (0,) * n)
    out_shape = [
        jax.ShapeDtypeStruct((GROUPS, FLAT, FLAT), _BF16),
        jax.ShapeDtypeStruct((PAIRS, 2 * FLAT, PAIR_LANES), _BF16),
        jax.ShapeDtypeStruct((PAIRS, PAIR_LANES, 2 * FLAT), _BF16),
        jax.ShapeDtypeStruct((PAIRS, 1, PAIR_LANES), _F32),
        jax.ShapeDtypeStruct((PAIRS, 1, PAIR_LANES), _F32),
    ]
    return pl.pallas_call(
        _tables_kernel,
        grid=(1,),
        in_specs=[whole(a) for a in args],
        out_specs=[whole(s) for s in out_shape],
        out_shape=out_shape,
        scratch_shapes=[
            pltpu.VMEM((2, GROUPS * GROUP_CH, FLAT), _F32),
            pltpu.VMEM((4, GROUPS * GROUP_CH, FLAT), _BF16),
            pltpu.VMEM((4, 2 * FLAT, LANE), _F32),
            pltpu.VMEM((2, FLAT, LANE), _F32),
        ],
        compiler_params=pltpu.CompilerParams(
            dimension_semantics=("arbitrary",), vmem_limit_bytes=VMEM_LIMIT),
        name="s5_tables",
    )(*args)


def kernel(x, c, ctx, c_ctx, w_mod, b_mod, norm1_g, norm2_g, w_in, s5_a_re, s5_a_im, s5_log_dt,
           s5_b_re, s5_b_im, s5_c_re, s5_c_im, s5_d, w_glu, b_glu, pool_w, pool_scale,
           w_branch_a, w_branch_b, w_out, w_ffn_in, w_ffn_out, final_norm_g):
    bsz, n_tok, d = x.shape
    ctx_len = ctx.shape[1]
    assert d == D_MODEL and w_mod.shape[0] == 1 and bsz + 1 <= 8
    assert n_tok % SCAN_ROWS == 0 and n_tok % MIX_ROWS == 0 and MIX_ROWS % GRID_W == 0
    assert ctx_len % CHUNK == 0 and bsz * ctx_len <= SCAN_ROWS
    n_ctx = ctx_len // CHUNK

    cc_t = jnp.zeros((D_MODEL, SUBLANES), _F32).at[:, :bsz].set(c.T).at[:, bsz].set(c_ctx)
    mod3 = _modulation(cc_t, w_mod[0], b_mod[0], bsz + 1).reshape(SUBLANES, 1, 6 * D_MODEL)

    mu, wst, mv, a16_re, a16_im = _s5_tables(
        s5_a_re[0], s5_a_im[0], s5_log_dt[0], s5_b_re[0], s5_b_im[0], s5_c_re[0], s5_c_im[0], s5_d[0])

    w_in_b = w_in[0].astype(_BF16)
    n1 = norm1_g[0].reshape(1, D_MODEL)

    uflat, s_loc = _pass1(x, mod3, 0, n1, w_in_b, wst)
    _, s_ctx = _pass1(ctx.reshape(1, bsz * ctx_len, D_MODEL), mod3, bsz, n1, w_in_b, wst)
    xstart = _chunk_scan(s_loc, s_ctx, a16_re, a16_im, n_ctx)
    ypre = _readout(uflat, xstart, mu, mv)

    x1 = _mixer(x, ypre, mod3, n1, w_in_b, w_glu[0].astype(_BF16), b_glu[0].reshape(1, D_SSM),
                w_branch_a[0].astype(_BF16), pool_w[0].astype(_BF16), pool_scale[0].reshape(1, D_POOL),
                w_branch_b[0].astype(_BF16), w_out[0].astype(_BF16))

    return _ffn(x1, mod3, norm2_g[0].reshape(1, D_MODEL), final_norm_g.reshape(1, D_MODEL),
                w_ffn_in[0].astype(_BF16), w_ffn_out[0].astype(_BF16))
```

```python
import functools

import numpy as np
import jax
import jax.numpy as jnp
from jax import lax
from jax.experimental import pallas as pl
from jax.experimental.pallas import tpu as pltpu

_F32 = jnp.float32
_BF16 = jnp.bfloat16

D_MODEL = 1024
D_SSM = 512
D_POOL = 512
GROUPS = 32
STATE = 64
GROUP_CH = 16
CHUNK = 16
FLAT = CHUNK * GROUP_CH
PAIRS = GROUPS // 2
PAIR_LANES = 4 * 2 * STATE
LANE = 128
SUBLANES = 8
COL_BLOCKS = D_SSM // LANE
GRID_W = 64
POOL_WINDOWS = (2, 4, 8, 16)
POOL_GROUP_CH = D_POOL // len(POOL_WINDOWS)
FFN_HIDDEN = 2816
RMS_EPS = 1e-6

SCAN_TILE = 128
SCAN_ROWS = SCAN_TILE * CHUNK
NORM_ROWS = 512
MIX_ROWS = 512
FFN_ROWS = 512
SCAN_PAIRS = 4
RELAYOUT_UNROLL = 8
VMEM_LIMIT = 56 * 1024 * 1024

_HI = lax.Precision.HIGHEST


def _rms_mod(x, gain, sh):
    ms = jnp.mean(x * x, axis=-1, keepdims=True)
    return (x * lax.rsqrt(ms + RMS_EPS)) * gain + sh


def _dot(a, b):
    return jnp.dot(a, b, preferred_element_type=_F32)


def _const_spec(shape, index=None):
    index = (0,) * len(shape) if index is None else index
    return pl.BlockSpec(shape, lambda *_: index, pipeline_mode=pl.Buffered(1))


def _mod_kernel(ct_ref, w_ref, b_ref, o_ref, *, n_rows):
    ct = ct_ref[...]
    a = ct * jax.nn.sigmoid(ct)
    w = w_ref[...]
    b = b_ref[...]
    for r in range(n_rows):
        o_ref[r] = jnp.sum(w * a[:, r:r + 1], axis=0, keepdims=True) + b
    for r in range(n_rows, o_ref.shape[0]):
        o_ref[r] = jnp.zeros_like(b)


def _modulation(cc_t, w_mod, b_mod, n_rows):
    n_out = w_mod.shape[-1]
    blk = 1024
    return pl.pallas_call(
        functools.partial(_mod_kernel, n_rows=n_rows),
        grid=(n_out // blk,),
        in_specs=[
            pl.BlockSpec((D_MODEL, SUBLANES), lambda i: (0, 0)),
            pl.BlockSpec((None, D_MODEL, blk), lambda i: (0, 0, i)),
            pl.BlockSpec((1, blk), lambda i: (0, i)),
        ],
        out_specs=pl.BlockSpec((SUBLANES, 1, blk), lambda i: (0, 0, i)),
        out_shape=jax.ShapeDtypeStruct((SUBLANES, 1, n_out), _F32),
        compiler_params=pltpu.CompilerParams(
            dimension_semantics=("arbitrary",), vmem_limit_bytes=VMEM_LIMIT),
        name="adaln_mod",
    )(cc_t, w_mod, b_mod)


def _p1_kernel(x_ref, sh_ref, sc_ref, g_ref, wa_ref, wst_ref, uflat_ref, s_ref, h_ref, u_ref, ut_ref,
               *, rows):
    gain = g_ref[...] * (1.0 + sc_ref[0])
    sh = sh_ref[0]

    def norm_block(i):
        r = slice(i * NORM_ROWS, (i + 1) * NORM_ROWS)
        h_ref[r, :] = _rms_mod(x_ref[0, r, :], gain, sh).astype(_BF16)

    norm_block(0)
    for i in range(rows // NORM_ROWS):
        if (i + 1) * NORM_ROWS < rows:
            norm_block(i + 1)
        r = slice(i * NORM_ROWS, (i + 1) * NORM_ROWS)
        u = _dot(h_ref[r, :], wa_ref[...])
        for cb in range(COL_BLOCKS):
            u_ref[cb, r, :] = u[:, cb * LANE:(cb + 1) * LANE]
    if rows < SCAN_ROWS:
        for cb in range(COL_BLOCKS):
            u_ref[cb, rows:, :] = jnp.zeros((SCAN_ROWS - rows, LANE), _F32)

    def slab_body(sg, carry):
        r0 = pl.multiple_of(sg * GROUP_CH, GROUP_CH)
        for cb in range(COL_BLOCKS):
            slab = u_ref[cb, pl.ds(sg, SCAN_TILE, stride=CHUNK), :]
            ut_ref[cb * 8:(cb + 1) * 8, pl.ds(r0, GROUP_CH), :] = slab.T.reshape(8, GROUP_CH, SCAN_TILE)
        return carry

    lax.fori_loop(0, CHUNK, slab_body, 0, unroll=RELAYOUT_UNROLL)

    def pair_body(q, carry):
        uf0 = ut_ref[2 * q].T.astype(_BF16)
        uf1 = ut_ref[2 * q + 1].T.astype(_BF16)
        uflat_ref[0, 2 * q] = uf0
        uflat_ref[0, 2 * q + 1] = uf1
        s_ref[0, q] = _dot(uf0, wst_ref[q, :FLAT, :]) + _dot(uf1, wst_ref[q, FLAT:, :])
        return carry

    lax.fori_loop(0, PAIRS, pair_body, 0, unroll=RELAYOUT_UNROLL)


def _pass1(x, mod3, mod_row0, norm_g, w_in_b, wst):
    bsz, n_tok, _ = x.shape
    rows = min(n_tok, SCAN_ROWS)
    assert n_tok % rows == 0 and rows % NORM_ROWS == 0
    nt = n_tok // rows
    n_chunks = nt * SCAN_TILE
    return pl.pallas_call(
        functools.partial(_p1_kernel, rows=rows),
        grid=(bsz, nt),
        in_specs=[
            pl.BlockSpec((1, rows, D_MODEL), lambda b, t: (b, t, 0)),
            pl.BlockSpec((1, 1, D_MODEL), lambda b, t: (b + mod_row0, 0, 0)),
            pl.BlockSpec((1, 1, D_MODEL), lambda b, t: (b + mod_row0, 0, 1)),
            _const_spec((1, D_MODEL)),
            _const_spec((D_MODEL, D_SSM)),
            _const_spec((PAIRS, 2 * FLAT, PAIR_LANES)),
        ],
        out_specs=[
            pl.BlockSpec((1, GROUPS, SCAN_TILE, FLAT), lambda b, t: (b, 0, t, 0)),
            pl.BlockSpec((1, PAIRS, SCAN_TILE, PAIR_LANES), lambda b, t: (b, 0, t, 0)),
        ],
        out_shape=[
            jax.ShapeDtypeStruct((bsz, GROUPS, n_chunks, FLAT), _BF16),
            jax.ShapeDtypeStruct((bsz, PAIRS, n_chunks, PAIR_LANES), _F32),
        ],
        scratch_shapes=[
            pltpu.VMEM((SCAN_ROWS, D_MODEL), _BF16),
            pltpu.VMEM((COL_BLOCKS, SCAN_ROWS, LANE), _F32),
            pltpu.VMEM((GROUPS, FLAT, SCAN_TILE), _F32),
        ],
        compiler_params=pltpu.CompilerParams(
            dimension_semantics=("arbitrary", "arbitrary"), vmem_limit_bytes=VMEM_LIMIT),
        name="s5_chunk_states",
    )(x, mod3, mod3, norm_g, w_in_b, wst)


def _scan_kernel(s_ref, sc_ref, ar_ref, ai_ref, x_ref, xs_ref, c_ref, *, bsz, n_chunks, n_ctx):
    rb = SUBLANES
    rid = lax.broadcasted_iota(jnp.int32, (rb, LANE), 0)
    chains = [(b, qq) for b in range(bsz) for qq in range(SCAN_PAIRS)]

    for qq in range(SCAN_PAIRS):
        for d in range(2):
            o = d * 2 * LANE
            a_r = ar_ref[qq][:, o:o + LANE]
            a_i = ai_ref[qq][:, o:o + LANE]
            pows = [(a_r, a_i)]
            for _ in range(rb - 1):
                pows.append(_cmul(pows[-1][0], pows[-1][1], a_r, a_i))
            idx = qq * 2 + d
            for k, shift in enumerate((1, 2, 4)):
                keep = (rid >= shift) if d == 0 else (rid < rb - shift)
                c_ref[idx, 2 * k] = jnp.where(keep, pows[shift - 1][0], 0.0)
                c_ref[idx, 2 * k + 1] = jnp.where(keep, pows[shift - 1][1], 0.0)
            p_r = jnp.zeros((rb, LANE), _F32)
            p_i = jnp.zeros((rb, LANE), _F32)
            for r in range(rb):
                e = r if d == 0 else rb - 1 - r
                p_r = jnp.where(rid == r, pows[e][0], p_r)
                p_i = jnp.where(rid == r, pows[e][1], p_i)
            c_ref[idx, 6] = p_r
            c_ref[idx, 7] = p_i

    def block(idx, d, s_re, s_im, xin_re, xin_im):
        def shifted(v, k):
            return pltpu.roll(v, k if d == 0 else rb - k, 0)

        t_re, t_im = s_re, s_im
        for k in range(3):
            a_r = c_ref[idx, 2 * k]
            a_i = c_ref[idx, 2 * k + 1]
            u_re, u_im = shifted(t_re, 1 << k), shifted(t_im, 1 << k)
            t_re, t_im = t_re + (a_r * u_re - a_i * u_im), t_im + (a_r * u_im + a_i * u_re)
        p_r = c_ref[idx, 6]
        p_i = c_ref[idx, 7]
        after_re = t_re + (p_r * xin_re - p_i * xin_im)
        after_im = t_im + (p_r * xin_im + p_i * xin_re)
        first = 0 if d == 0 else rb - 1
        last = rb - 1 - first
        start_re = jnp.where(rid == first, xin_re, shifted(after_re, 1))
        start_im = jnp.where(rid == first, xin_im, shifted(after_im, 1))
        return start_re, start_im, after_re[last:last + 1], after_im[last:last + 1]

    def sweep(src_ref, lead, row0, n_blocks, i, carry, dst_ref):
        out = []
        for ci, (b, qq) in enumerate(chains):
            for d in range(2):
                blk = i if d == 0 else n_blocks - 1 - i
                r0 = row0(b) + blk * rb
                if not isinstance(r0, int):
                    r0 = pl.multiple_of(r0, rb)
                o = d * 2 * LANE
                bi = lead(b)
                s_re = src_ref[bi, qq, pl.ds(r0, rb), o:o + LANE]
                s_im = src_ref[bi, qq, pl.ds(r0, rb), o + LANE:o + 2 * LANE]
                xin_re, xin_im = carry[4 * ci + 2 * d], carry[4 * ci + 2 * d + 1]
                st_re, st_im, xo_re, xo_im = block(qq * 2 + d, d, s_re, s_im, xin_re, xin_im)
                if dst_ref is not None:
                    dst_ref[bi, qq, pl.ds(r0, rb), o:o + LANE] = st_re
                    dst_ref[bi, qq, pl.ds(r0, rb), o + LANE:o + 2 * LANE] = st_im
                out += [xo_re, xo_im]
        return tuple(out)

    carry = tuple(jnp.zeros((1, LANE), _F32) for _ in range(4 * len(chains)))
    ctx_blocks = n_ctx // rb
    for i in range(ctx_blocks):
        carry = sweep(sc_ref, lambda b: 0, lambda b: b * n_ctx, ctx_blocks, i, carry, None)

    n_blocks = n_chunks // rb
    lax.fori_loop(
        0, n_blocks,
        lambda i, c: sweep(s_ref, lambda b: b, lambda b: 0, n_blocks, i, c, xs_ref),
        carry)
    x_ref[...] = xs_ref[...].astype(_BF16)


def _chunk_scan(s_loc, s_ctx, a_re, a_im, n_ctx):
    bsz, _, n_chunks, _ = s_loc.shape
    assert n_chunks % SUBLANES == 0 and n_ctx % SUBLANES == 0
    blk = (bsz, SCAN_PAIRS, n_chunks, PAIR_LANES)
    return pl.pallas_call(
        functools.partial(_scan_kernel, bsz=bsz, n_chunks=n_chunks, n_ctx=n_ctx),
        grid=(PAIRS // SCAN_PAIRS,),
        in_specs=[
            pl.BlockSpec(blk, lambda i: (0, i, 0, 0)),
            pl.BlockSpec((1, SCAN_PAIRS, SCAN_TILE, PAIR_LANES), lambda i: (0, i, 0, 0)),
            pl.BlockSpec((SCAN_PAIRS, 1, PAIR_LANES), lambda i: (i, 0, 0)),
            pl.BlockSpec((SCAN_PAIRS, 1, PAIR_LANES), lambda i: (i, 0, 0)),
        ],
        out_specs=pl.BlockSpec(blk, lambda i: (0, i, 0, 0)),
        out_shape=jax.ShapeDtypeStruct(s_loc.shape, _BF16),
        scratch_shapes=[pltpu.VMEM(blk, _F32),
                        pltpu.VMEM((2 * SCAN_PAIRS, 8, SUBLANES, LANE), _F32)],
        compiler_params=pltpu.CompilerParams(
            dimension_semantics=("arbitrary",), vmem_limit_bytes=VMEM_LIMIT),
        name="s5_chunk_scan",
    )(s_loc, s_ctx, a_re, a_im)


def _readout_kernel(uflat_ref, x_ref, mu_ref, mv_ref, y_ref, yt_ref, ys_ref):
    def pair_body(q, carry):
        y0 = _dot(uflat_ref[0, 2 * q], mu_ref[2 * q])
        y1 = _dot(uflat_ref[0, 2 * q + 1], mu_ref[2 * q + 1])
        yx = _dot(x_ref[0, q], mv_ref[q])
        yt = (jnp.concatenate([y0, y1], axis=1) + yx).T
        yt_ref[2 * q] = yt[:FLAT]
        yt_ref[2 * q + 1] = yt[FLAT:]
        return carry

    lax.fori_loop(0, PAIRS, pair_body, 0, unroll=RELAYOUT_UNROLL)

    def slab_body(sg, carry):
        r0 = pl.multiple_of(sg * GROUP_CH, GROUP_CH)
        for cb in range(COL_BLOCKS):
            yt = yt_ref[cb * 8:(cb + 1) * 8, pl.ds(r0, GROUP_CH), :].reshape(LANE, SCAN_TILE)
            ys_ref[cb, pl.ds(sg, SCAN_TILE, stride=CHUNK), :] = yt.T
        return carry

    lax.fori_loop(0, CHUNK, slab_body, 0, unroll=RELAYOUT_UNROLL)
    for cb in range(COL_BLOCKS):
        y_ref[0, :, cb * LANE:(cb + 1) * LANE] = ys_ref[cb].astype(y_ref.dtype)


def _readout(uflat, xstart, mu, mv):
    bsz, _, n_chunks, _ = uflat.shape
    nt = n_chunks // SCAN_TILE
    return pl.pallas_call(
        _readout_kernel,
        grid=(bsz, nt),
        in_specs=[
            pl.BlockSpec((1, GROUPS, SCAN_TILE, FLAT), lambda b, t: (b, 0, t, 0)),
            pl.BlockSpec((1, PAIRS, SCAN_TILE, PAIR_LANES), lambda b, t: (b, 0, t, 0)),
            _const_spec((GROUPS, FLAT, FLAT)),
            _const_spec((PAIRS, PAIR_LANES, 2 * FLAT)),
        ],
        out_specs=pl.BlockSpec((1, SCAN_ROWS, D_SSM), lambda b, t: (b, t, 0)),
        out_shape=jax.ShapeDtypeStruct((bsz, n_chunks * CHUNK, D_SSM), _BF16),
        scratch_shapes=[
            pltpu.VMEM((GROUPS, FLAT, SCAN_TILE), _F32),
            pltpu.VMEM((COL_BLOCKS, SCAN_ROWS, LANE), _F32),
        ],
        compiler_params=pltpu.CompilerParams(
            dimension_semantics=("arbitrary", "arbitrary"), vmem_limit_bytes=VMEM_LIMIT),
        name="s5_readout",
    )(uflat, xstart, mu, mv)


def _window_sum(u, w):
    assert w // 2 <= SUBLANES
    rows, lanes = u.shape
    nb = rows // GRID_W
    pad = jnp.zeros((nb, SUBLANES, lanes), _F32)
    z = jnp.concatenate([pad, u.reshape(nb, GRID_W, lanes), pad], axis=1)
    n = nb * (GRID_W + 2 * SUBLANES)
    z = z.reshape(n, lanes)
    acc = z + pltpu.roll(z, 1, 0)
    m = 2
    while m < w:
        acc = pltpu.roll(acc, m // 2, 0) + pltpu.roll(acc, n - m // 2, 0)
        m *= 2
    return acc.reshape(nb, GRID_W + 2 * SUBLANES, lanes)[:, SUBLANES:SUBLANES + GRID_W, :].reshape(rows, lanes)


def _mix_kernel(x_ref, y_ref, sh_ref, sc_ref, gt_ref, g_ref, wr_ref, wglu_ref, bglu_ref, wa_ref,
                pinv_ref, pw_ref, ps_ref, wb_ref, wo_ref, o_ref):
    x = x_ref[0]
    h = _rms_mod(x, g_ref[...] * (1.0 + sc_ref[0]), sh_ref[0]).astype(_BF16)
    c_pool, c_ga, c_gb = D_SSM, D_SSM + D_POOL, D_SSM + D_POOL + D_MODEL
    windows = range(len(POOL_WINDOWS))
    group = lambda wi: slice(wi * POOL_GROUP_CH, (wi + 1) * POOL_GROUP_CH)

    ub = _dot(h, wr_ref[:, c_pool:c_ga])
    y = jax.nn.gelu(y_ref[0].astype(_F32))
    glu = _dot(y.astype(_BF16), wglu_ref[...])
    wsums = [_window_sum(ub[:, group(wi)], POOL_WINDOWS[wi]) for wi in windows]
    gate_a = _dot(h, wr_ref[:, c_ga:c_gb])
    z = y * jax.nn.sigmoid(glu + bglu_ref[...])
    ya = _dot(z.astype(_BF16), wa_ref[...])
    outs = [_dot((wsums[wi] * pinv_ref[wi] - ub[:, group(wi)]).astype(_BF16), pw_ref[wi])
            for wi in windows]
    gate_b = _dot(h, wr_ref[:, c_gb:])
    pb = jnp.concatenate(outs, axis=1) * ps_ref[...]
    yb = _dot(pb.astype(_BF16), wb_ref[...])

    merged = jax.nn.sigmoid(gate_a) * ya + jax.nn.sigmoid(gate_b) * yb
    mixed = _dot(merged.astype(_BF16), wo_ref[...])
    o_ref[0] = x + gt_ref[0] * mixed


def _pool_inverse_counts(rows):
    pos = np.arange(rows) % GRID_W
    invs = []
    for w in POOL_WINDOWS:
        lo = np.clip(pos - w // 2, 0, GRID_W - 1)
        hi = np.clip(pos + w - 1 - w // 2, 0, GRID_W - 1) + 1
        invs.append(np.broadcast_to((1.0 / (hi - lo).astype(np.float32))[:, None], (rows, POOL_GROUP_CH)))
    return np.stack(invs)


def _mixer(x, ypre, mod3, norm_g, w_in_b, w_glu, b_glu, w_a, pool_w, pool_scale, w_b, w_out):
    bsz, n_tok, _ = x.shape
    tm = MIX_ROWS
    pinv = jnp.asarray(_pool_inverse_counts(tm), _F32)
    nw = len(POOL_WINDOWS)
    return pl.pallas_call(
        _mix_kernel,
        grid=(bsz, n_tok // tm),
        in_specs=[
            pl.BlockSpec((1, tm, D_MODEL), lambda b, t: (b, t, 0)),
            pl.BlockSpec((1, tm, D_SSM), lambda b, t: (b, t, 0)),
            pl.BlockSpec((1, 1, D_MODEL), lambda b, t: (b, 0, 0)),
            pl.BlockSpec((1, 1, D_MODEL), lambda b, t: (b, 0, 1)),
            pl.BlockSpec((1, 1, D_MODEL), lambda b, t: (b, 0, 2)),
            _const_spec((1, D_MODEL)),
            _const_spec((D_MODEL, D_SSM + D_POOL + 2 * D_MODEL)),
            _const_spec((D_SSM, D_SSM)),
            _const_spec((1, D_SSM)),
            _const_spec((D_SSM, D_MODEL)),
            _const_spec((nw, tm, POOL_GROUP_CH)),
            _const_spec((nw, POOL_GROUP_CH, POOL_GROUP_CH)),
            _const_spec((1, D_POOL)),
            _const_spec((D_POOL, D_MODEL)),
            _const_spec((D_MODEL, D_MODEL)),
        ],
        out_specs=pl.BlockSpec((1, tm, D_MODEL), lambda b, t: (b, t, 0)),
        out_shape=jax.ShapeDtypeStruct(x.shape, _F32),
        compiler_params=pltpu.CompilerParams(
            dimension_semantics=("arbitrary", "arbitrary"), vmem_limit_bytes=VMEM_LIMIT),
        name="token_mixer",
    )(x, ypre, mod3, mod3, mod3, norm_g, w_in_b, w_glu, b_glu, w_a, pinv, pool_w, pool_scale,
      w_b, w_out)


def _ffn_kernel(x_ref, sh_ref, sc_ref, gt_ref, g2_ref, gf_ref, wg_ref, wu_ref, wo_ref, o_ref):
    x = x_ref[0]
    h = _rms_mod(x, g2_ref[...] * (1.0 + sc_ref[0]), sh_ref[0]).astype(_BF16)
    gate = _dot(h, wg_ref[...])
    up = _dot(h, wu_ref[...])
    act = (gate * jax.nn.sigmoid(gate) * up).astype(_BF16)
    y = x + gt_ref[0] * _dot(act, wo_ref[...])
    ms = jnp.mean(y * y, axis=-1, keepdims=True)
    o_ref[0] = (y * lax.rsqrt(ms + RMS_EPS)) * gf_ref[...]


def _ffn(x1, mod3, norm2_g, final_g, w_ffn, w_down):
    bsz, n_tok, _ = x1.shape
    tm = FFN_ROWS
    return pl.pallas_call(
        _ffn_kernel,
        grid=(bsz, n_tok // tm),
        in_specs=[
            pl.BlockSpec((1, tm, D_MODEL), lambda b, t: (b, t, 0)),
            pl.BlockSpec((1, 1, D_MODEL), lambda b, t: (b, 0, 3)),
            pl.BlockSpec((1, 1, D_MODEL), lambda b, t: (b, 0, 4)),
            pl.BlockSpec((1, 1, D_MODEL), lambda b, t: (b, 0, 5)),
            _const_spec((1, D_MODEL)),
            _const_spec((1, D_MODEL)),
            _const_spec((D_MODEL, FFN_HIDDEN), (0, 0)),
            _const_spec((D_MODEL, FFN_HIDDEN), (0, 1)),
            _const_spec((FFN_HIDDEN, D_MODEL)),
        ],
        out_specs=pl.BlockSpec((1, tm, D_MODEL), lambda b, t: (b, t, 0)),
        out_shape=jax.ShapeDtypeStruct(x1.shape, _F32),
        compiler_params=pltpu.CompilerParams(
            dimension_semantics=("arbitrary", "arbitrary"), vmem_limit_bytes=VMEM_LIMIT),
        name="swiglu_ffn",
    )(x1, mod3, mod3, mod3, norm2_g, final_g, w_ffn, w_ffn, w_down)


def _cmul(xr, xi, yr, yi):
    return xr * yr - xi * yi, xr * yi + xi * yr


def _tables_kernel(a_ref, bt_ref, c_ref, d_ref,
                   mu_ref, wst_ref, mv_ref, a16r_ref, a16i_ref, kt_ref, khl_ref, vnat_ref, cp_ref):
    nt_dims = (((1,), (1,)), ((), ()))

    def pair_body(q, carry):
        for d in range(2):
            ar = a_ref[0, d, q]
            ai = a_ref[1, d, q]
            dt = jnp.exp(a_ref[2, d, q])
            mag = jnp.exp(ar * dt)
            ang = ai * dt
            abr, abi = mag * jnp.cos(ang), mag * jnp.sin(ang)
            den = ar * ar + ai * ai
            fr = ((abr - 1.0) * ar + abi * ai) / den
            fi = (abi * ar - (abr - 1.0) * ai) / den
            bbr, bbi = _cmul(bt_ref[0, d, q], bt_ref[1, d, q], fr, fi)
            cr = c_ref[0, d, q]
            ci = c_ref[1, d, q]
            pw = [(jnp.ones_like(ar), jnp.zeros_like(ar))]
            for _ in range(CHUNK):
                pw.append(_cmul(pw[-1][0], pw[-1][1], abr, abi))
            lanes_re = slice(2 * d * LANE, (2 * d + 1) * LANE)
            lanes_im = slice((2 * d + 1) * LANE, (2 * d + 2) * LANE)
            for lanes in (lanes_re, lanes_im):
                a16r_ref[q, :, lanes] = pw[CHUNK][0]
                a16i_ref[q, :, lanes] = pw[CHUNK][1]
            for sg in range(CHUNK):
                e = (CHUNK - 1 - sg) if d == 0 else sg
                wr, wi = _cmul(bbr, bbi, pw[e][0], pw[e][1])
                e = (sg + 1) if d == 0 else (CHUNK - sg)
                vr, vi = _cmul(cr, ci, pw[e][0], pw[e][1])
                for gg in range(2):
                    src = slice(gg * GROUP_CH, (gg + 1) * GROUP_CH)
                    dst = slice(gg * FLAT + sg * GROUP_CH, gg * FLAT + (sg + 1) * GROUP_CH)
                    wst_ref[q, dst, lanes_re] = wr[src].astype(_BF16)
                    wst_ref[q, dst, lanes_im] = wi[src].astype(_BF16)
                    vnat_ref[2 * d, dst, :] = vr[src]
                    vnat_ref[2 * d + 1, dst, :] = -vi[src]
            for gg in range(2):
                src = slice(gg * GROUP_CH, (gg + 1) * GROUP_CH)
                for k in range(CHUNK):
                    e = k if d == 0 else (CHUNK - 1 - k)
                    pr, pi = _cmul(cr[src], ci[src], pw[e][0], pw[e][1])
                    cp_ref[0, k * GROUP_CH:(k + 1) * GROUP_CH, :] = pr
                    cp_ref[1, k * GROUP_CH:(k + 1) * GROUP_CH, :] = pi
                kt = (lax.dot_general(bbr[src], cp_ref[0], nt_dims, precision=_HI, preferred_element_type=_F32)
                      - lax.dot_general(bbi[src], cp_ref[1], nt_dims, precision=_HI, preferred_element_type=_F32))
                r0 = pl.multiple_of((2 * q + gg) * GROUP_CH, GROUP_CH)
                kt_ref[d, pl.ds(r0, GROUP_CH), :] = kt
        for part in range(4):
            mv_ref[q, part * LANE:(part + 1) * LANE, :] = vnat_ref[part].T.astype(_BF16)
        return carry

    lax.fori_loop(0, PAIRS, pair_body, 0)

    for d in range(2):
        kt = kt_ref[d]
        hi = kt.astype(_BF16)
        khl_ref[2 * d] = hi
        khl_ref[2 * d + 1] = (kt - hi.astype(_F32)).astype(_BF16)

    row = lax.broadcasted_iota(jnp.int32, (FLAT, FLAT), 0)
    col = lax.broadcasted_iota(jnp.int32, (FLAT, FLAT), 1)
    same_ch = (row % GROUP_CH) == (col % GROUP_CH)
    row_blk = row // GROUP_CH
    col_blk = col // GROUP_CH
    orow = lax.broadcasted_iota(jnp.int32, (GROUPS * GROUP_CH, FLAT), 0)
    ocol = lax.broadcasted_iota(jnp.int32, (GROUPS * GROUP_CH, FLAT), 1)
    skip_ch = (orow % GROUP_CH) == (ocol % GROUP_CH)
    ocol_blk = ocol // GROUP_CH
    d_col = d_ref[...]

    def toeplitz_body(sg, carry):
        sf = jnp.where(same_ch & (row_blk + sg == col_blk), 1.0, 0.0).astype(_BF16)
        sb = jnp.where(same_ch & (row_blk == col_blk + (CHUNK - 1) - sg), 1.0, 0.0).astype(_BF16)
        out = (_dot(khl_ref[0], sf) + _dot(khl_ref[1], sf)) + (_dot(khl_ref[2], sb) + _dot(khl_ref[3], sb))
        out = out + jnp.where(skip_ch & (ocol_blk == sg), d_col, 0.0)
        r0 = pl.multiple_of(sg * GROUP_CH, GROUP_CH)
        mu_ref[:, pl.ds(r0, GROUP_CH), :] = out.reshape(GROUPS, GROUP_CH, FLAT).astype(_BF16)
        return carry

    lax.fori_loop(0, CHUNK, toeplitz_body, 0)


def _s5_tables(a_re, a_im, log_dt, b_re, b_im, c_re, c_im, d_skip):
    f32 = _F32
    eye2 = jnp.eye(2, dtype=f32)

    def pair_blocks(re, im):
        v = jnp.stack([re, im]).astype(f32).reshape(2, 2, PAIRS, 2, GROUP_CH, 1, STATE)
        v = v * eye2[None, None, None, :, None, :, None]
        return v.reshape(2, 2, PAIRS, 2 * GROUP_CH, 2 * STATE)

    ldt = jnp.broadcast_to(log_dt.astype(f32)[..., None], (2, GROUPS, STATE))
    a_rows = jnp.stack([a_re.astype(f32), a_im.astype(f32), ldt]).reshape(3, 2, PAIRS, 1, 2 * STATE)
    args = (a_rows, pair_blocks(jnp.swapaxes(b_re, 2, 3), jnp.swapaxes(b_im, 2, 3)),
            pair_blocks(c_re, c_im), d_skip.astype(f32).reshape(D_SSM, 1))
    whole = lambda a: pl.BlockSpec(a.shape, lambda i, n=a.ndim: (0,) * n)
    out_shape = [
        jax.ShapeDtypeStruct((GROUPS, FLAT, FLAT), _BF16),
        jax.ShapeDtypeStruct((PAIRS, 2 * FLAT, PAIR_LANES), _BF16),
        jax.ShapeDtypeStruct((PAIRS, PAIR_LANES, 2 * FLAT), _BF16),
        jax.ShapeDtypeStruct((PAIRS, 1, PAIR_LANES), _F32),
        jax.ShapeDtypeStruct((PAIRS, 1, PAIR_LANES), _F32),
    ]
    return pl.pallas_call(
        _tables_kernel,
        grid=(1,),
        in_specs=[whole(a) for a in args],
        out_specs=[whole(s) for s in out_shape],
        out_shape=out_shape,
        scratch_shapes=[
            pltpu.VMEM((2, GROUPS * GROUP_CH, FLAT), _F32),
            pltpu.VMEM((4, GROUPS * GROUP_CH, FLAT), _BF16),
            pltpu.VMEM((4, 2 * FLAT, LANE), _F32),
            pltpu.VMEM((2, FLAT, LANE), _F32),
        ],
        compiler_params=pltpu.CompilerParams(
            dimension_semantics=("arbitrary",), vmem_limit_bytes=VMEM_LIMIT),
        name="s5_tables",
    )(*args)


def kernel(x, c, ctx, c_ctx, w_mod, b_mod, norm1_g, norm2_g, w_in, s5_a_re, s5_a_im, s5_log_dt,
           s5_b_re, s5_b_im, s5_c_re, s5_c_im, s5_d, w_glu, b_glu, pool_w, pool_scale,
           w_branch_a, w_branch_b, w_out, w_ffn_in, w_ffn_out, final_norm_g):
    bsz, n_tok, d = x.shape
    ctx_len = ctx.shape[1]
    assert d == D_MODEL and w_mod.shape[0] == 1 and bsz + 1 <= SUBLANES
    assert n_tok % SCAN_ROWS == 0 and n_tok % MIX_ROWS == 0 and MIX_ROWS % GRID_W == 0
    assert ctx_len % CHUNK == 0 and bsz * ctx_len <= SCAN_ROWS
    n_ctx = ctx_len // CHUNK

    cc_t = jnp.concatenate(
        [c.T, c_ctx[:, None], jnp.zeros((D_MODEL, SUBLANES - bsz - 1), _F32)], axis=1)
    mod3 = _modulation(cc_t, w_mod, b_mod, bsz + 1)

    mu, wst, mv, a16_re, a16_im = _s5_tables(
        s5_a_re[0], s5_a_im[0], s5_log_dt[0], s5_b_re[0], s5_b_im[0], s5_c_re[0], s5_c_im[0], s5_d[0])

    w_in_b = w_in[0].astype(_BF16)

    uflat, s_loc = _pass1(x, mod3, 0, norm1_g, w_in_b, wst)
    _, s_ctx = _pass1(ctx.reshape(1, bsz * ctx_len, D_MODEL), mod3, bsz, norm1_g, w_in_b, wst)
    xstart = _chunk_scan(s_loc, s_ctx, a16_re, a16_im, n_ctx)
    ypre = _readout(uflat, xstart, mu, mv)

    x1 = _mixer(x, ypre, mod3, norm1_g, w_in_b, w_glu[0].astype(_BF16), b_glu,
                w_branch_a[0].astype(_BF16), pool_w[0].astype(_BF16), pool_scale,
                w_branch_b[0].astype(_BF16), w_out[0].astype(_BF16))

    return _ffn(x1, mod3, norm2_g, final_norm_g.reshape(1, D_MODEL),
                w_ffn_in[0].astype(_BF16), w_ffn_out[0].astype(_BF16))
```

```python
import functools

import numpy as np
import jax
import jax.numpy as jnp
from jax import lax
from jax.experimental import pallas as pl
from jax.experimental.pallas import tpu as pltpu

_F32 = jnp.float32
_BF16 = jnp.bfloat16

D_MODEL = 1024
D_SSM = 512
D_POOL = 512
GROUPS = 32
STATE = 64
GROUP_CH = 16
CHUNK = 16
FLAT = CHUNK * GROUP_CH
PAIRS = GROUPS // 2
PAIR_LANES = 4 * 2 * STATE
LANE = 128
SUBLANES = 8
COL_BLOCKS = D_SSM // LANE
GRID_W = 64
POOL_WINDOWS = (2, 4, 8, 16)
POOL_GROUP_CH = D_POOL // len(POOL_WINDOWS)
FFN_HIDDEN = 2816
RMS_EPS = 1e-6

SCAN_TILE = 128
SCAN_ROWS = SCAN_TILE * CHUNK
NORM_ROWS = 512
MIX_ROWS = 512
FFN_ROWS = 512
SCAN_PAIRS = 4
RELAYOUT_UNROLL = 8
WEIGHT_STAGE_ROWS = 128
VMEM_LIMIT = 56 * 1024 * 1024

_HI = lax.Precision.HIGHEST


def _rms_mod(x, gain, sh):
    ms = jnp.mean(x * x, axis=-1, keepdims=True)
    return (x * lax.rsqrt(ms + RMS_EPS)) * gain + sh


def _dot(a, b):
    return jnp.dot(a, b, preferred_element_type=_F32)


def _const_spec(shape, index=None):
    index = (0,) * len(shape) if index is None else index
    return pl.BlockSpec(shape, lambda *_: index, pipeline_mode=pl.Buffered(1))


def _mod_kernel(ct_ref, w_ref, b_ref, o_ref, *, n_rows):
    ct = ct_ref[...]
    a = ct * jax.nn.sigmoid(ct)
    w = w_ref[...]
    b = b_ref[...]
    for r in range(n_rows):
        o_ref[r] = jnp.sum(w * a[:, r:r + 1], axis=0, keepdims=True) + b
    for r in range(n_rows, o_ref.shape[0]):
        o_ref[r] = jnp.zeros_like(b)


def _modulation(cc_t, w_mod, b_mod, n_rows):
    n_out = w_mod.shape[-1]
    blk = 1024
    return pl.pallas_call(
        functools.partial(_mod_kernel, n_rows=n_rows),
        grid=(n_out // blk,),
        in_specs=[
            pl.BlockSpec((D_MODEL, SUBLANES), lambda i: (0, 0)),
            pl.BlockSpec((None, D_MODEL, blk), lambda i: (0, 0, i)),
            pl.BlockSpec((1, blk), lambda i: (0, i)),
        ],
        out_specs=pl.BlockSpec((SUBLANES, 1, blk), lambda i: (0, 0, i)),
        out_shape=jax.ShapeDtypeStruct((SUBLANES, 1, n_out), _F32),
        compiler_params=pltpu.CompilerParams(
            dimension_semantics=("arbitrary",), vmem_limit_bytes=VMEM_LIMIT),
        name="adaln_mod",
    )(cc_t, w_mod, b_mod)


def _p1_kernel(x_ref, sh_ref, sc_ref, g_ref, wa_ref, wst_ref, uflat_ref, s_ref, h_ref, u_ref, ut_ref,
               *, rows):
    gain = g_ref[...] * (1.0 + sc_ref[0])
    sh = sh_ref[0]
    wa = wa_ref[...].astype(_BF16)

    def norm_block(i):
        r = slice(i * NORM_ROWS, (i + 1) * NORM_ROWS)
        h_ref[r, :] = _rms_mod(x_ref[0, r, :], gain, sh).astype(_BF16)

    norm_block(0)
    for i in range(rows // NORM_ROWS):
        if (i + 1) * NORM_ROWS < rows:
            norm_block(i + 1)
        r = slice(i * NORM_ROWS, (i + 1) * NORM_ROWS)
        u = _dot(h_ref[r, :], wa)
        for cb in range(COL_BLOCKS):
            u_ref[cb, r, :] = u[:, cb * LANE:(cb + 1) * LANE]
    if rows < SCAN_ROWS:
        for cb in range(COL_BLOCKS):
            u_ref[cb, rows:, :] = jnp.zeros((SCAN_ROWS - rows, LANE), _F32)

    def slab_body(sg, carry):
        r0 = pl.multiple_of(sg * GROUP_CH, GROUP_CH)
        for cb in range(COL_BLOCKS):
            slab = u_ref[cb, pl.ds(sg, SCAN_TILE, stride=CHUNK), :]
            ut_ref[cb * 8:(cb + 1) * 8, pl.ds(r0, GROUP_CH), :] = slab.T.reshape(8, GROUP_CH, SCAN_TILE)
        return carry

    lax.fori_loop(0, CHUNK, slab_body, 0, unroll=RELAYOUT_UNROLL)

    def pair_body(q, carry):
        uf0 = ut_ref[2 * q].T.astype(_BF16)
        uf1 = ut_ref[2 * q + 1].T.astype(_BF16)
        uflat_ref[0, 2 * q] = uf0
        uflat_ref[0, 2 * q + 1] = uf1
        s_ref[0, q] = _dot(uf0, wst_ref[q, :FLAT, :]) + _dot(uf1, wst_ref[q, FLAT:, :])
        return carry

    lax.fori_loop(0, PAIRS, pair_body, 0, unroll=RELAYOUT_UNROLL)


def _pass1(x, mod3, mod_row0, norm_g, w_in, wst):
    bsz, n_tok, _ = x.shape
    rows = min(n_tok, SCAN_ROWS)
    assert n_tok % rows == 0 and rows % NORM_ROWS == 0
    nt = n_tok // rows
    n_chunks = nt * SCAN_TILE
    return pl.pallas_call(
        functools.partial(_p1_kernel, rows=rows),
        grid=(bsz, nt),
        in_specs=[
            pl.BlockSpec((1, rows, D_MODEL), lambda b, t: (b, t, 0)),
            pl.BlockSpec((1, 1, D_MODEL), lambda b, t: (b + mod_row0, 0, 0)),
            pl.BlockSpec((1, 1, D_MODEL), lambda b, t: (b + mod_row0, 0, 1)),
            _const_spec((1, D_MODEL)),
            _const_spec((None, D_MODEL, D_SSM)),
            _const_spec((PAIRS, 2 * FLAT, PAIR_LANES)),
        ],
        out_specs=[
            pl.BlockSpec((1, GROUPS, SCAN_TILE, FLAT), lambda b, t: (b, 0, t, 0)),
            pl.BlockSpec((1, PAIRS, SCAN_TILE, PAIR_LANES), lambda b, t: (b, 0, t, 0)),
        ],
        out_shape=[
            jax.ShapeDtypeStruct((bsz, GROUPS, n_chunks, FLAT), _BF16),
            jax.ShapeDtypeStruct((bsz, PAIRS, n_chunks, PAIR_LANES), _F32),
        ],
        scratch_shapes=[
            pltpu.VMEM((SCAN_ROWS, D_MODEL), _BF16),
            pltpu.VMEM((COL_BLOCKS, SCAN_ROWS, LANE), _F32),
            pltpu.VMEM((GROUPS, FLAT, SCAN_TILE), _F32),
        ],
        compiler_params=pltpu.CompilerParams(
            dimension_semantics=("arbitrary", "arbitrary"), vmem_limit_bytes=VMEM_LIMIT),
        name="s5_chunk_states",
    )(x, mod3, mod3, norm_g, w_in, wst)


def _scan_kernel(s_ref, sc_ref, ar_ref, ai_ref, x_ref, xs_ref, c_ref, *, bsz, n_chunks, n_ctx):
    rb = SUBLANES
    rid = lax.broadcasted_iota(jnp.int32, (rb, LANE), 0)
    chains = [(b, qq) for b in range(bsz) for qq in range(SCAN_PAIRS)]

    for qq in range(SCAN_PAIRS):
        for d in range(2):
            o = d * 2 * LANE
            a_r = ar_ref[qq][:, o:o + LANE]
            a_i = ai_ref[qq][:, o:o + LANE]
            pows = [(a_r, a_i)]
            for _ in range(rb - 1):
                pows.append(_cmul(pows[-1][0], pows[-1][1], a_r, a_i))
            idx = qq * 2 + d
            for k, shift in enumerate((1, 2, 4)):
                keep = (rid >= shift) if d == 0 else (rid < rb - shift)
                c_ref[idx, 2 * k] = jnp.where(keep, pows[shift - 1][0], 0.0)
                c_ref[idx, 2 * k + 1] = jnp.where(keep, pows[shift - 1][1], 0.0)
            p_r = jnp.zeros((rb, LANE), _F32)
            p_i = jnp.zeros((rb, LANE), _F32)
            for r in range(rb):
                e = r if d == 0 else rb - 1 - r
                p_r = jnp.where(rid == r, pows[e][0], p_r)
                p_i = jnp.where(rid == r, pows[e][1], p_i)
            c_ref[idx, 6] = p_r
            c_ref[idx, 7] = p_i

    def block(idx, d, s_re, s_im, xin_re, xin_im):
        def shifted(v, k):
            return pltpu.roll(v, k if d == 0 else rb - k, 0)

        t_re, t_im = s_re, s_im
        for k in range(3):
            a_r = c_ref[idx, 2 * k]
            a_i = c_ref[idx, 2 * k + 1]
            u_re, u_im = shifted(t_re, 1 << k), shifted(t_im, 1 << k)
            t_re, t_im = t_re + (a_r * u_re - a_i * u_im), t_im + (a_r * u_im + a_i * u_re)
        p_r = c_ref[idx, 6]
        p_i = c_ref[idx, 7]
        after_re = t_re + (p_r * xin_re - p_i * xin_im)
        after_im = t_im + (p_r * xin_im + p_i * xin_re)
        first = 0 if d == 0 else rb - 1
        last = rb - 1 - first
        start_re = jnp.where(rid == first, xin_re, shifted(after_re, 1))
        start_im = jnp.where(rid == first, xin_im, shifted(after_im, 1))
        return start_re, start_im, after_re[last:last + 1], after_im[last:last + 1]

    def sweep(src_ref, lead, row0, n_blocks, i, carry, dst_ref):
        out = []
        for ci, (b, qq) in enumerate(chains):
            for d in range(2):
                blk = i if d == 0 else n_blocks - 1 - i
                r0 = row0(b) + blk * rb
                if not isinstance(r0, int):
                    r0 = pl.multiple_of(r0, rb)
                o = d * 2 * LANE
                bi = lead(b)
                s_re = src_ref[bi, qq, pl.ds(r0, rb), o:o + LANE]
                s_im = src_ref[bi, qq, pl.ds(r0, rb), o + LANE:o + 2 * LANE]
                xin_re, xin_im = carry[4 * ci + 2 * d], carry[4 * ci + 2 * d + 1]
                st_re, st_im, xo_re, xo_im = block(qq * 2 + d, d, s_re, s_im, xin_re, xin_im)
                if dst_ref is not None:
                    dst_ref[bi, qq, pl.ds(r0, rb), o:o + LANE] = st_re
                    dst_ref[bi, qq, pl.ds(r0, rb), o + LANE:o + 2 * LANE] = st_im
                out += [xo_re, xo_im]
        return tuple(out)

    carry = tuple(jnp.zeros((1, LANE), _F32) for _ in range(4 * len(chains)))
    ctx_blocks = n_ctx // rb
    for i in range(ctx_blocks):
        carry = sweep(sc_ref, lambda b: 0, lambda b: b * n_ctx, ctx_blocks, i, carry, None)

    n_blocks = n_chunks // rb
    lax.fori_loop(
        0, n_blocks,
        lambda i, c: sweep(s_ref, lambda b: b, lambda b: 0, n_blocks, i, c, xs_ref),
        carry)
    x_ref[...] = xs_ref[...].astype(_BF16)


def _chunk_scan(s_loc, s_ctx, a_re, a_im, n_ctx):
    bsz, _, n_chunks, _ = s_loc.shape
    assert n_chunks % SUBLANES == 0 and n_ctx % SUBLANES == 0
    blk = (bsz, SCAN_PAIRS, n_chunks, PAIR_LANES)
    return pl.pallas_call(
        functools.partial(_scan_kernel, bsz=bsz, n_chunks=n_chunks, n_ctx=n_ctx),
        grid=(PAIRS // SCAN_PAIRS,),
        in_specs=[
            pl.BlockSpec(blk, lambda i: (0, i, 0, 0)),
            pl.BlockSpec((1, SCAN_PAIRS, SCAN_TILE, PAIR_LANES), lambda i: (0, i, 0, 0)),
            pl.BlockSpec((SCAN_PAIRS, 1, PAIR_LANES), lambda i: (i, 0, 0)),
            pl.BlockSpec((SCAN_PAIRS, 1, PAIR_LANES), lambda i: (i, 0, 0)),
        ],
        out_specs=pl.BlockSpec(blk, lambda i: (0, i, 0, 0)),
        out_shape=jax.ShapeDtypeStruct(s_loc.shape, _BF16),
        scratch_shapes=[pltpu.VMEM(blk, _F32),
                        pltpu.VMEM((2 * SCAN_PAIRS, 8, SUBLANES, LANE), _F32)],
        compiler_params=pltpu.CompilerParams(
            dimension_semantics=("arbitrary",), vmem_limit_bytes=VMEM_LIMIT),
        name="s5_chunk_scan",
    )(s_loc, s_ctx, a_re, a_im)


def _readout_kernel(uflat_ref, x_ref, mu_ref, mv_ref, y_ref, yt_ref, ys_ref):
    def pair_body(q, carry):
        y0 = _dot(uflat_ref[0, 2 * q], mu_ref[2 * q])
        y1 = _dot(uflat_ref[0, 2 * q + 1], mu_ref[2 * q + 1])
        yx = _dot(x_ref[0, q], mv_ref[q])
        yt = (jnp.concatenate([y0, y1], axis=1) + yx).T
        yt_ref[2 * q] = yt[:FLAT]
        yt_ref[2 * q + 1] = yt[FLAT:]
        return carry

    lax.fori_loop(0, PAIRS, pair_body, 0, unroll=RELAYOUT_UNROLL)

    def slab_body(sg, carry):
        r0 = pl.multiple_of(sg * GROUP_CH, GROUP_CH)
        for cb in range(COL_BLOCKS):
            yt = yt_ref[cb * 8:(cb + 1) * 8, pl.ds(r0, GROUP_CH), :].reshape(LANE, SCAN_TILE)
            ys_ref[cb, pl.ds(sg, SCAN_TILE, stride=CHUNK), :] = yt.T
        return carry

    lax.fori_loop(0, CHUNK, slab_body, 0, unroll=RELAYOUT_UNROLL)
    for cb in range(COL_BLOCKS):
        y_ref[0, :, cb * LANE:(cb + 1) * LANE] = ys_ref[cb].astype(y_ref.dtype)


def _readout(uflat, xstart, mu, mv):
    bsz, _, n_chunks, _ = uflat.shape
    nt = n_chunks // SCAN_TILE
    return pl.pallas_call(
        _readout_kernel,
        grid=(bsz, nt),
        in_specs=[
            pl.BlockSpec((1, GROUPS, SCAN_TILE, FLAT), lambda b, t: (b, 0, t, 0)),
            pl.BlockSpec((1, PAIRS, SCAN_TILE, PAIR_LANES), lambda b, t: (b, 0, t, 0)),
            _const_spec((GROUPS, FLAT, FLAT)),
            _const_spec((PAIRS, PAIR_LANES, 2 * FLAT)),
        ],
        out_specs=pl.BlockSpec((1, SCAN_ROWS, D_SSM), lambda b, t: (b, t, 0)),
        out_shape=jax.ShapeDtypeStruct((bsz, n_chunks * CHUNK, D_SSM), _BF16),
        scratch_shapes=[
            pltpu.VMEM((GROUPS, FLAT, SCAN_TILE), _F32),
            pltpu.VMEM((COL_BLOCKS, SCAN_ROWS, LANE), _F32),
        ],
        compiler_params=pltpu.CompilerParams(
            dimension_semantics=("arbitrary", "arbitrary"), vmem_limit_bytes=VMEM_LIMIT),
        name="s5_readout",
    )(uflat, xstart, mu, mv)


def _window_sum(u, w):
    assert w // 2 <= SUBLANES
    rows, lanes = u.shape
    nb = rows // GRID_W
    pad = jnp.zeros((nb, SUBLANES, lanes), _F32)
    z = jnp.concatenate([pad, u.reshape(nb, GRID_W, lanes), pad], axis=1)
    n = nb * (GRID_W + 2 * SUBLANES)
    z = z.reshape(n, lanes)
    acc = z + pltpu.roll(z, 1, 0)
    m = 2
    while m < w:
        acc = pltpu.roll(acc, m // 2, 0) + pltpu.roll(acc, n - m // 2, 0)
        m *= 2
    return acc.reshape(nb, GRID_W + 2 * SUBLANES, lanes)[:, SUBLANES:SUBLANES + GRID_W, :].reshape(rows, lanes)


def _mix_kernel(x_ref, y_ref, sh_ref, sc_ref, gt_ref, g_ref, bglu_ref, pinv_ref, ps_ref,
                win_hbm, wglu_hbm, wa_hbm, pw_hbm, wb_hbm, wo_hbm, o_ref,
                wr_ref, wglu_ref, wa_ref, pw_ref, wb_ref, wo_ref, stage_ref, sem):
    @pl.when(_first_step())
    def _load_weights():
        r = WEIGHT_STAGE_ROWS
        n_rest = D_POOL + 2 * D_MODEL

        def pieces(src3, dst, n_rows, c0, n_cols):
            return [(src3.at[0, pl.ds(k * r, r), pl.ds(c0, n_cols)],
                     [(dst.at[pl.ds(k * r, r), :], slice(0, n_cols))]) for k in range(n_rows // r)]

        jobs = pieces(win_hbm, wr_ref, D_MODEL, D_SSM, n_rest)
        jobs += pieces(wglu_hbm, wglu_ref, D_SSM, 0, D_SSM)
        jobs += pieces(wa_hbm, wa_ref, D_SSM, 0, D_MODEL)
        jobs += [(pw_hbm.at[0, wi], [(pw_ref.at[wi], slice(0, POOL_GROUP_CH))])
                 for wi in range(len(POOL_WINDOWS))]
        jobs += pieces(wb_hbm, wb_ref, D_POOL, 0, D_MODEL)
        jobs += pieces(wo_hbm, wo_ref, D_MODEL, 0, D_MODEL)
        _stream_cast(jobs, stage_ref, sem)

    x = x_ref[0]
    h = _rms_mod(x, g_ref[...] * (1.0 + sc_ref[0]), sh_ref[0]).astype(_BF16)
    c_pool, c_ga, c_gb = 0, D_POOL, D_POOL + D_MODEL
    windows = range(len(POOL_WINDOWS))
    group = lambda wi: slice(wi * POOL_GROUP_CH, (wi + 1) * POOL_GROUP_CH)

    ub = _dot(h, wr_ref[:, c_pool:c_ga])
    y = jax.nn.gelu(y_ref[0].astype(_F32))
    glu = _dot(y.astype(_BF16), wglu_ref[...])
    wsums = [_window_sum(ub[:, group(wi)], POOL_WINDOWS[wi]) for wi in windows]
    gate_a = _dot(h, wr_ref[:, c_ga:c_gb])
    z = y * jax.nn.sigmoid(glu + bglu_ref[...])
    ya = _dot(z.astype(_BF16), wa_ref[...])
    outs = [_dot((wsums[wi] * pinv_ref[wi] - ub[:, group(wi)]).astype(_BF16), pw_ref[wi])
            for wi in windows]
    gate_b = _dot(h, wr_ref[:, c_gb:])
    pb = jnp.concatenate(outs, axis=1) * ps_ref[...]
    yb = _dot(pb.astype(_BF16), wb_ref[...])

    merged = jax.nn.sigmoid(gate_a) * ya + jax.nn.sigmoid(gate_b) * yb
    mixed = _dot(merged.astype(_BF16), wo_ref[...])
    o_ref[0] = x + gt_ref[0] * mixed


def _pool_inverse_counts(rows):
    pos = np.arange(rows) % GRID_W
    invs = []
    for w in POOL_WINDOWS:
        lo = np.clip(pos - w // 2, 0, GRID_W - 1)
        hi = np.clip(pos + w - 1 - w // 2, 0, GRID_W - 1) + 1
        invs.append(np.broadcast_to((1.0 / (hi - lo).astype(np.float32))[:, None], (rows, POOL_GROUP_CH)))
    return np.stack(invs)


def _mixer(x, ypre, mod3, norm_g, b_glu, pool_scale, w_in, w_glu, w_a, pool_w, w_b, w_out):
    bsz, n_tok, _ = x.shape
    tm = MIX_ROWS
    pinv = jnp.asarray(_pool_inverse_counts(tm), _F32)
    nw = len(POOL_WINDOWS)
    n_rest = D_POOL + 2 * D_MODEL
    hbm = pl.BlockSpec(memory_space=pl.ANY)
    return pl.pallas_call(
        _mix_kernel,
        grid=(bsz, n_tok // tm),
        in_specs=[
            pl.BlockSpec((1, tm, D_MODEL), lambda b, t: (b, t, 0)),
            pl.BlockSpec((1, tm, D_SSM), lambda b, t: (b, t, 0)),
            pl.BlockSpec((1, 1, D_MODEL), lambda b, t: (b, 0, 0)),
            pl.BlockSpec((1, 1, D_MODEL), lambda b, t: (b, 0, 1)),
            pl.BlockSpec((1, 1, D_MODEL), lambda b, t: (b, 0, 2)),
            _const_spec((1, D_MODEL)),
            _const_spec((1, D_SSM)),
            _const_spec((nw, tm, POOL_GROUP_CH)),
            _const_spec((1, D_POOL)),
            hbm, hbm, hbm, hbm, hbm, hbm,
        ],
        out_specs=pl.BlockSpec((1, tm, D_MODEL), lambda b, t: (b, t, 0)),
        out_shape=jax.ShapeDtypeStruct(x.shape, _F32),
        scratch_shapes=[
            pltpu.VMEM((D_MODEL, n_rest), _BF16),
            pltpu.VMEM((D_SSM, D_SSM), _BF16),
            pltpu.VMEM((D_SSM, D_MODEL), _BF16),
            pltpu.VMEM((nw, POOL_GROUP_CH, POOL_GROUP_CH), _BF16),
            pltpu.VMEM((D_POOL, D_MODEL), _BF16),
            pltpu.VMEM((D_MODEL, D_MODEL), _BF16),
            pltpu.VMEM((2, WEIGHT_STAGE_ROWS, n_rest), _F32),
            pltpu.SemaphoreType.DMA((2,)),
        ],
        compiler_params=pltpu.CompilerParams(
            dimension_semantics=("arbitrary", "arbitrary"), vmem_limit_bytes=VMEM_LIMIT),
        name="token_mixer",
    )(x, ypre, mod3, mod3, mod3, norm_g, b_glu, pinv, pool_scale, w_in, w_glu, w_a, pool_w, w_b, w_out)


def _stream_cast(jobs, stage_ref, sem):
    def copy(i):
        src = jobs[i][0]
        rows, cols = src.shape
        return pltpu.make_async_copy(src, stage_ref.at[i % 2, pl.ds(0, rows), pl.ds(0, cols)], sem.at[i % 2])

    copy(0).start()
    for i, (src, dsts) in enumerate(jobs):
        if i + 1 < len(jobs):
            copy(i + 1).start()
        copy(i).wait()
        rows = src.shape[0]
        for dst, cols in dsts:
            dst[...] = stage_ref[i % 2, 0:rows, cols].astype(_BF16)


def _first_step():
    return jnp.logical_and(pl.program_id(0) == 0, pl.program_id(1) == 0)


def _ffn_kernel(x_ref, sh_ref, sc_ref, gt_ref, g2_ref, gf_ref, win_hbm, wout_hbm, o_ref,
                wg_ref, wu_ref, wo_ref, stage_ref, sem):
    @pl.when(_first_step())
    def _load_weights():
        r = WEIGHT_STAGE_ROWS
        jobs = [(win_hbm.at[0, pl.ds(k * r, r), :],
                 [(wg_ref.at[pl.ds(k * r, r), :], slice(0, FFN_HIDDEN)),
                  (wu_ref.at[pl.ds(k * r, r), :], slice(FFN_HIDDEN, 2 * FFN_HIDDEN))])
                for k in range(D_MODEL // r)]
        jobs += [(wout_hbm.at[0, pl.ds(k * r, r), :], [(wo_ref.at[pl.ds(k * r, r), :], slice(0, D_MODEL))])
                 for k in range(FFN_HIDDEN // r)]
        _stream_cast(jobs, stage_ref, sem)

    x = x_ref[0]
    h = _rms_mod(x, g2_ref[...] * (1.0 + sc_ref[0]), sh_ref[0]).astype(_BF16)
    gate = _dot(h, wg_ref[...])
    up = _dot(h, wu_ref[...])
    act = (gate * jax.nn.sigmoid(gate) * up).astype(_BF16)
    y = x + gt_ref[0] * _dot(act, wo_ref[...])
    ms = jnp.mean(y * y, axis=-1, keepdims=True)
    o_ref[0] = (y * lax.rsqrt(ms + RMS_EPS)) * gf_ref[...]


def _ffn(x1, mod3, norm2_g, final_g, w_ffn, w_down):
    bsz, n_tok, _ = x1.shape
    tm = FFN_ROWS
    hbm = pl.BlockSpec(memory_space=pl.ANY)
    return pl.pallas_call(
        _ffn_kernel,
        grid=(bsz, n_tok // tm),
        in_specs=[
            pl.BlockSpec((1, tm, D_MODEL), lambda b, t: (b, t, 0)),
            pl.BlockSpec((1, 1, D_MODEL), lambda b, t: (b, 0, 3)),
            pl.BlockSpec((1, 1, D_MODEL), lambda b, t: (b, 0, 4)),
            pl.BlockSpec((1, 1, D_MODEL), lambda b, t: (b, 0, 5)),
            _const_spec((1, D_MODEL)),
            _const_spec((1, D_MODEL)),
            hbm,
            hbm,
        ],
        out_specs=pl.BlockSpec((1, tm, D_MODEL), lambda b, t: (b, t, 0)),
        out_shape=jax.ShapeDtypeStruct(x1.shape, _F32),
        scratch_shapes=[
            pltpu.VMEM((D_MODEL, FFN_HIDDEN), _BF16),
            pltpu.VMEM((D_MODEL, FFN_HIDDEN), _BF16),
            pltpu.VMEM((FFN_HIDDEN, D_MODEL), _BF16),
            pltpu.VMEM((2, WEIGHT_STAGE_ROWS, 2 * FFN_HIDDEN), _F32),
            pltpu.SemaphoreType.DMA((2,)),
        ],
        compiler_params=pltpu.CompilerParams(
            dimension_semantics=("arbitrary", "arbitrary"), vmem_limit_bytes=VMEM_LIMIT),
        name="swiglu_ffn",
    )(x1, mod3, mod3, mod3, norm2_g, final_g, w_ffn, w_down)


def _cmul(xr, xi, yr, yi):
    return xr * yr - xi * yi, xr * yi + xi * yr


def _tables_kernel(a_ref, bt_ref, c_ref, d_ref,
                   mu_ref, wst_ref, mv_ref, a16r_ref, a16i_ref, kt_ref, khl_ref, vnat_ref, cp_ref):
    nt_dims = (((1,), (1,)), ((), ()))

    def pair_body(q, carry):
        for d in range(2):
            ar = a_ref[0, d, q]
            ai = a_ref[1, d, q]
            dt = jnp.exp(a_ref[2, d, q])
            mag = jnp.exp(ar * dt)
            ang = ai * dt
            abr, abi = mag * jnp.cos(ang), mag * jnp.sin(ang)
            den = ar * ar + ai * ai
            fr = ((abr - 1.0) * ar + abi * ai) / den
            fi = (abi * ar - (abr - 1.0) * ai) / den
            bbr, bbi = _cmul(bt_ref[0, d, q], bt_ref[1, d, q], fr, fi)
            cr = c_ref[0, d, q]
            ci = c_ref[1, d, q]
            pw = [(jnp.ones_like(ar), jnp.zeros_like(ar))]
            for _ in range(CHUNK):
                pw.append(_cmul(pw[-1][0], pw[-1][1], abr, abi))
            lanes_re = slice(2 * d * LANE, (2 * d + 1) * LANE)
            lanes_im = slice((2 * d + 1) * LANE, (2 * d + 2) * LANE)
            for lanes in (lanes_re, lanes_im):
                a16r_ref[q, :, lanes] = pw[CHUNK][0]
                a16i_ref[q, :, lanes] = pw[CHUNK][1]
            for sg in range(CHUNK):
                e = (CHUNK - 1 - sg) if d == 0 else sg
                wr, wi = _cmul(bbr, bbi, pw[e][0], pw[e][1])
                e = (sg + 1) if d == 0 else (CHUNK - sg)
                vr, vi = _cmul(cr, ci, pw[e][0], pw[e][1])
                for gg in range(2):
                    src = slice(gg * GROUP_CH, (gg + 1) * GROUP_CH)
                    dst = slice(gg * FLAT + sg * GROUP_CH, gg * FLAT + (sg + 1) * GROUP_CH)
                    wst_ref[q, dst, lanes_re] = wr[src].astype(_BF16)
                    wst_ref[q, dst, lanes_im] = wi[src].astype(_BF16)
                    vnat_ref[2 * d, dst, :] = vr[src]
                    vnat_ref[2 * d + 1, dst, :] = -vi[src]
            for gg in range(2):
                src = slice(gg * GROUP_CH, (gg + 1) * GROUP_CH)
                for k in range(CHUNK):
                    e = k if d == 0 else (CHUNK - 1 - k)
                    pr, pi = _cmul(cr[src], ci[src], pw[e][0], pw[e][1])
                    cp_ref[0, k * GROUP_CH:(k + 1) * GROUP_CH, :] = pr
                    cp_ref[1, k * GROUP_CH:(k + 1) * GROUP_CH, :] = pi
                kt = (lax.dot_general(bbr[src], cp_ref[0], nt_dims, precision=_HI, preferred_element_type=_F32)
                      - lax.dot_general(bbi[src], cp_ref[1], nt_dims, precision=_HI, preferred_element_type=_F32))
                r0 = pl.multiple_of((2 * q + gg) * GROUP_CH, GROUP_CH)
                kt_ref[d, pl.ds(r0, GROUP_CH), :] = kt
        for part in range(4):
            mv_ref[q, part * LANE:(part + 1) * LANE, :] = vnat_ref[part].T.astype(_BF16)
        return carry

    lax.fori_loop(0, PAIRS, pair_body, 0)

    for d in range(2):
        kt = kt_ref[d]
        hi = kt.astype(_BF16)
        khl_ref[2 * d] = hi
        khl_ref[2 * d + 1] = (kt - hi.astype(_F32)).astype(_BF16)

    row = lax.broadcasted_iota(jnp.int32, (FLAT, FLAT), 0)
    col = lax.broadcasted_iota(jnp.int32, (FLAT, FLAT), 1)
    same_ch = (row % GROUP_CH) == (col % GROUP_CH)
    row_blk = row // GROUP_CH
    col_blk = col // GROUP_CH
    orow = lax.broadcasted_iota(jnp.int32, (GROUPS * GROUP_CH, FLAT), 0)
    ocol = lax.broadcasted_iota(jnp.int32, (GROUPS * GROUP_CH, FLAT), 1)
    skip_ch = (orow % GROUP_CH) == (ocol % GROUP_CH)
    ocol_blk = ocol // GROUP_CH
    d_col = d_ref[...]

    def toeplitz_body(sg, carry):
        sf = jnp.where(same_ch & (row_blk + sg == col_blk), 1.0, 0.0).astype(_BF16)
        sb = jnp.where(same_ch & (row_blk == col_blk + (CHUNK - 1) - sg), 1.0, 0.0).astype(_BF16)
        out = (_dot(khl_ref[0], sf) + _dot(khl_ref[1], sf)) + (_dot(khl_ref[2], sb) + _dot(khl_ref[3], sb))
        out = out + jnp.where(skip_ch & (ocol_blk == sg), d_col, 0.0)
        r0 = pl.multiple_of(sg * GROUP_CH, GROUP_CH)
        mu_ref[:, pl.ds(r0, GROUP_CH), :] = out.reshape(GROUPS, GROUP_CH, FLAT).astype(_BF16)
        return carry

    lax.fori_loop(0, CHUNK, toeplitz_body, 0)


def _s5_tables(a_re, a_im, log_dt, b_re, b_im, c_re, c_im, d_skip):
    f32 = _F32
    eye2 = jnp.eye(2, dtype=f32)

    def pair_blocks(re, im):
        v = jnp.stack([re, im]).astype(f32).reshape(2, 2, PAIRS, 2, GROUP_CH, 1, STATE)
        v = v * eye2[None, None, None, :, None, :, None]
        return v.reshape(2, 2, PAIRS, 2 * GROUP_CH, 2 * STATE)

    ldt = jnp.broadcast_to(log_dt.astype(f32)[..., None], (2, GROUPS, STATE))
    a_rows = jnp.stack([a_re.astype(f32), a_im.astype(f32), ldt]).reshape(3, 2, PAIRS, 1, 2 * STATE)
    args = (a_rows, pair_blocks(jnp.swapaxes(b_re, 2, 3), jnp.swapaxes(b_im, 2, 3)),
            pair_blocks(c_re, c_im), d_skip.astype(f32).reshape(D_SSM, 1))
    whole = lambda a: pl.BlockSpec(a.shape, lambda i, n=a.ndim: (0,) * n)
    out_shape = [
        jax.ShapeDtypeStruct((GROUPS, FLAT, FLAT), _BF16),
        jax.ShapeDtypeStruct((PAIRS, 2 * FLAT, PAIR_LANES), _BF16),
        jax.ShapeDtypeStruct((PAIRS, PAIR_LANES, 2 * FLAT), _BF16),
        jax.ShapeDtypeStruct((PAIRS, 1, PAIR_LANES), _F32),
        jax.ShapeDtypeStruct((PAIRS, 1, PAIR_LANES), _F32),
    ]
    return pl.pallas_call(
        _tables_kernel,
        grid=(1,),
        in_specs=[whole(a) for a in args],
        out_specs=[whole(s) for s in out_shape],
        out_shape=out_shape,
        scratch_shapes=[
            pltpu.VMEM((2, GROUPS * GROUP_CH, FLAT), _F32),
            pltpu.VMEM((4, GROUPS * GROUP_CH, FLAT), _BF16),
            pltpu.VMEM((4, 2 * FLAT, LANE), _F32),
            pltpu.VMEM((2, FLAT, LANE), _F32),
        ],
        compiler_params=pltpu.CompilerParams(
            dimension_semantics=("arbitrary",), vmem_limit_bytes=VMEM_LIMIT),
        name="s5_tables",
    )(*args)


def kernel(x, c, ctx, c_ctx, w_mod, b_mod, norm1_g, norm2_g, w_in, s5_a_re, s5_a_im, s5_log_dt,
           s5_b_re, s5_b_im, s5_c_re, s5_c_im, s5_d, w_glu, b_glu, pool_w, pool_scale,
           w_branch_a, w_branch_b, w_out, w_ffn_in, w_ffn_out, final_norm_g):
    bsz, n_tok, d = x.shape
    ctx_len = ctx.shape[1]
    assert d == D_MODEL and w_mod.shape[0] == 1 and bsz + 1 <= SUBLANES
    assert n_tok % SCAN_ROWS == 0 and n_tok % MIX_ROWS == 0 and MIX_ROWS % GRID_W == 0
    assert ctx_len % CHUNK == 0 and bsz * ctx_len <= SCAN_ROWS
    n_ctx = ctx_len // CHUNK

    cc_t = jnp.concatenate(
        [c.T, c_ctx[:, None], jnp.zeros((D_MODEL, SUBLANES - bsz - 1), _F32)], axis=1)
    mod3 = _modulation(cc_t, w_mod, b_mod, bsz + 1)

    mu, wst, mv, a16_re, a16_im = _s5_tables(
        s5_a_re[0], s5_a_im[0], s5_log_dt[0], s5_b_re[0], s5_b_im[0], s5_c_re[0], s5_c_im[0], s5_d[0])

    uflat, s_loc = _pass1(x, mod3, 0, norm1_g, w_in, wst)
    _, s_ctx = _pass1(ctx.reshape(1, bsz * ctx_len, D_MODEL), mod3, bsz, norm1_g, w_in, wst)
    xstart = _chunk_scan(s_loc, s_ctx, a16_re, a16_im, n_ctx)
    ypre = _readout(uflat, xstart, mu, mv)

    x1 = _mixer(x, ypre, mod3, norm1_g, b_glu, pool_scale, w_in, w_glu, w_branch_a, pool_w, w_branch_b, w_out)
    return _ffn(x1, mod3, norm2_g, final_norm_g.reshape(1, D_MODEL), w_ffn_in, w_ffn_out)
```

```python
import functools

import numpy as np
import jax
import jax.numpy as jnp
from jax import lax
from jax.experimental import pallas as pl
from jax.experimental.pallas import tpu as pltpu

_F32 = jnp.float32
_BF16 = jnp.bfloat16

D_MODEL = 1024
D_SSM = 512
D_POOL = 512
GROUPS = 32
STATE = 64
GROUP_CH = 16
CHUNK = 16
FLAT = CHUNK * GROUP_CH
PAIRS = GROUPS // 2
PAIR_LANES = 4 * 2 * STATE
LANE = 128
SUBLANES = 8
COL_BLOCKS = D_SSM // LANE
GRID_W = 64
POOL_WINDOWS = (2, 4, 8, 16)
POOL_GROUP_CH = D_POOL // len(POOL_WINDOWS)
FFN_HIDDEN = 2816
RMS_EPS = 1e-6

SCAN_TILE = 128
SCAN_ROWS = SCAN_TILE * CHUNK
NORM_ROWS = 512
MIX_ROWS = 512
FFN_ROWS = 512
SCAN_PAIRS = 4
RELAYOUT_UNROLL = 8
WEIGHT_STAGE_ROWS = 128
WEIGHT_STAGE_SLOTS = 3
VMEM_LIMIT = 56 * 1024 * 1024

_HI = lax.Precision.HIGHEST


def _rms_mod(x, gain, sh):
    ms = jnp.mean(x * x, axis=-1, keepdims=True)
    return (x * lax.rsqrt(ms + RMS_EPS)) * gain + sh


def _dot(a, b):
    return jnp.dot(a, b, preferred_element_type=_F32)


def _const_spec(shape, index=None):
    index = (0,) * len(shape) if index is None else index
    return pl.BlockSpec(shape, lambda *_: index, pipeline_mode=pl.Buffered(1))


def _mod_kernel(ct_ref, w_ref, b_ref, o_ref, *, n_rows):
    ct = ct_ref[...]
    a = ct * jax.nn.sigmoid(ct)
    w = w_ref[...]
    b = b_ref[...]
    for r in range(n_rows):
        o_ref[r] = jnp.sum(w * a[:, r:r + 1], axis=0, keepdims=True) + b
    for r in range(n_rows, o_ref.shape[0]):
        o_ref[r] = jnp.zeros_like(b)


def _modulation(cc_t, w_mod, b_mod, n_rows):
    n_out = w_mod.shape[-1]
    blk = 1024
    return pl.pallas_call(
        functools.partial(_mod_kernel, n_rows=n_rows),
        grid=(n_out // blk,),
        in_specs=[
            pl.BlockSpec((D_MODEL, SUBLANES), lambda i: (0, 0)),
            pl.BlockSpec((None, D_MODEL, blk), lambda i: (0, 0, i)),
            pl.BlockSpec((1, blk), lambda i: (0, i)),
        ],
        out_specs=pl.BlockSpec((SUBLANES, 1, blk), lambda i: (0, 0, i)),
        out_shape=jax.ShapeDtypeStruct((SUBLANES, 1, n_out), _F32),
        compiler_params=pltpu.CompilerParams(
            dimension_semantics=("arbitrary",), vmem_limit_bytes=VMEM_LIMIT),
        name="adaln_mod",
    )(cc_t, w_mod, b_mod)


def _p1_kernel(x_ref, sh_ref, sc_ref, g_ref, wa_ref, wst_ref, uflat_ref, s_ref, h_ref, u_ref, ut_ref,
               *, rows):
    gain = g_ref[...] * (1.0 + sc_ref[0])
    sh = sh_ref[0]
    wa = wa_ref[...].astype(_BF16)

    def norm_block(i):
        r = slice(i * NORM_ROWS, (i + 1) * NORM_ROWS)
        h_ref[r, :] = _rms_mod(x_ref[0, r, :], gain, sh).astype(_BF16)

    norm_block(0)
    for i in range(rows // NORM_ROWS):
        if (i + 1) * NORM_ROWS < rows:
            norm_block(i + 1)
        r = slice(i * NORM_ROWS, (i + 1) * NORM_ROWS)
        u = _dot(h_ref[r, :], wa)
        for cb in range(COL_BLOCKS):
            u_ref[cb, r, :] = u[:, cb * LANE:(cb + 1) * LANE]
    if rows < SCAN_ROWS:
        for cb in range(COL_BLOCKS):
            u_ref[cb, rows:, :] = jnp.zeros((SCAN_ROWS - rows, LANE), _F32)

    def slab_body(sg, carry):
        r0 = pl.multiple_of(sg * GROUP_CH, GROUP_CH)
        for cb in range(COL_BLOCKS):
            slab = u_ref[cb, pl.ds(sg, SCAN_TILE, stride=CHUNK), :]
            ut_ref[cb * 8:(cb + 1) * 8, pl.ds(r0, GROUP_CH), :] = slab.T.reshape(8, GROUP_CH, SCAN_TILE)
        return carry

    lax.fori_loop(0, CHUNK, slab_body, 0, unroll=RELAYOUT_UNROLL)

    def pair_body(q, carry):
        uf0 = ut_ref[2 * q].T.astype(_BF16)
        uf1 = ut_ref[2 * q + 1].T.astype(_BF16)
        uflat_ref[0, 2 * q] = uf0
        uflat_ref[0, 2 * q + 1] = uf1
        s_ref[0, q] = _dot(uf0, wst_ref[q, :FLAT, :]) + _dot(uf1, wst_ref[q, FLAT:, :])
        return carry

    lax.fori_loop(0, PAIRS, pair_body, 0, unroll=RELAYOUT_UNROLL)


def _pass1(x, mod3, mod_row0, norm_g, w_in, wst):
    bsz, n_tok, _ = x.shape
    rows = min(n_tok, SCAN_ROWS)
    assert n_tok % rows == 0 and rows % NORM_ROWS == 0
    nt = n_tok // rows
    n_chunks = nt * SCAN_TILE
    return pl.pallas_call(
        functools.partial(_p1_kernel, rows=rows),
        grid=(bsz, nt),
        in_specs=[
            pl.BlockSpec((1, rows, D_MODEL), lambda b, t: (b, t, 0)),
            pl.BlockSpec((1, 1, D_MODEL), lambda b, t: (b + mod_row0, 0, 0)),
            pl.BlockSpec((1, 1, D_MODEL), lambda b, t: (b + mod_row0, 0, 1)),
            _const_spec((1, D_MODEL)),
            _const_spec((None, D_MODEL, D_SSM)),
            _const_spec((PAIRS, 2 * FLAT, PAIR_LANES)),
        ],
        out_specs=[
            pl.BlockSpec((1, GROUPS, SCAN_TILE, FLAT), lambda b, t: (b, 0, t, 0)),
            pl.BlockSpec((1, PAIRS, SCAN_TILE, PAIR_LANES), lambda b, t: (b, 0, t, 0)),
        ],
        out_shape=[
            jax.ShapeDtypeStruct((bsz, GROUPS, n_chunks, FLAT), _BF16),
            jax.ShapeDtypeStruct((bsz, PAIRS, n_chunks, PAIR_LANES), _F32),
        ],
        scratch_shapes=[
            pltpu.VMEM((SCAN_ROWS, D_MODEL), _BF16),
            pltpu.VMEM((COL_BLOCKS, SCAN_ROWS, LANE), _F32),
            pltpu.VMEM((GROUPS, FLAT, SCAN_TILE), _F32),
        ],
        compiler_params=pltpu.CompilerParams(
            dimension_semantics=("arbitrary", "arbitrary"), vmem_limit_bytes=VMEM_LIMIT),
        name="s5_chunk_states",
    )(x, mod3, mod3, norm_g, w_in, wst)


def _scan_kernel(s_ref, sc_ref, ar_ref, ai_ref, x_ref, xs_ref, c_ref, *, bsz, n_chunks, n_ctx):
    rb = SUBLANES
    rid = lax.broadcasted_iota(jnp.int32, (rb, LANE), 0)
    chains = [(b, qq) for b in range(bsz) for qq in range(SCAN_PAIRS)]

    for qq in range(SCAN_PAIRS):
        for d in range(2):
            o = d * 2 * LANE
            a_r = ar_ref[qq][:, o:o + LANE]
            a_i = ai_ref[qq][:, o:o + LANE]
            pows = [(a_r, a_i)]
            for _ in range(rb - 1):
                pows.append(_cmul(pows[-1][0], pows[-1][1], a_r, a_i))
            idx = qq * 2 + d
            for k, shift in enumerate((1, 2, 4)):
                keep = (rid >= shift) if d == 0 else (rid < rb - shift)
                c_ref[idx, 2 * k] = jnp.where(keep, pows[shift - 1][0], 0.0)
                c_ref[idx, 2 * k + 1] = jnp.where(keep, pows[shift - 1][1], 0.0)
            p_r = jnp.zeros((rb, LANE), _F32)
            p_i = jnp.zeros((rb, LANE), _F32)
            for r in range(rb):
                e = r if d == 0 else rb - 1 - r
                p_r = jnp.where(rid == r, pows[e][0], p_r)
                p_i = jnp.where(rid == r, pows[e][1], p_i)
            c_ref[idx, 6] = p_r
            c_ref[idx, 7] = p_i

    def block(idx, d, s_re, s_im, xin_re, xin_im):
        def shifted(v, k):
            return pltpu.roll(v, k if d == 0 else rb - k, 0)

        t_re, t_im = s_re, s_im
        for k in range(3):
            a_r = c_ref[idx, 2 * k]
            a_i = c_ref[idx, 2 * k + 1]
            u_re, u_im = shifted(t_re, 1 << k), shifted(t_im, 1 << k)
            t_re, t_im = t_re + (a_r * u_re - a_i * u_im), t_im + (a_r * u_im + a_i * u_re)
        p_r = c_ref[idx, 6]
        p_i = c_ref[idx, 7]
        after_re = t_re + (p_r * xin_re - p_i * xin_im)
        after_im = t_im + (p_r * xin_im + p_i * xin_re)
        first = 0 if d == 0 else rb - 1
        last = rb - 1 - first
        start_re = jnp.where(rid == first, xin_re, shifted(after_re, 1))
        start_im = jnp.where(rid == first, xin_im, shifted(after_im, 1))
        return start_re, start_im, after_re[last:last + 1], after_im[last:last + 1]

    def sweep(src_ref, lead, row0, n_blocks, i, carry, dst_ref):
        out = []
        for ci, (b, qq) in enumerate(chains):
            for d in range(2):
                blk = i if d == 0 else n_blocks - 1 - i
                r0 = row0(b) + blk * rb
                if not isinstance(r0, int):
                    r0 = pl.multiple_of(r0, rb)
                o = d * 2 * LANE
                bi = lead(b)
                s_re = src_ref[bi, qq, pl.ds(r0, rb), o:o + LANE]
                s_im = src_ref[bi, qq, pl.ds(r0, rb), o + LANE:o + 2 * LANE]
                xin_re, xin_im = carry[4 * ci + 2 * d], carry[4 * ci + 2 * d + 1]
                st_re, st_im, xo_re, xo_im = block(qq * 2 + d, d, s_re, s_im, xin_re, xin_im)
                if dst_ref is not None:
                    dst_ref[bi, qq, pl.ds(r0, rb), o:o + LANE] = st_re
                    dst_ref[bi, qq, pl.ds(r0, rb), o + LANE:o + 2 * LANE] = st_im
                out += [xo_re, xo_im]
        return tuple(out)

    carry = tuple(jnp.zeros((1, LANE), _F32) for _ in range(4 * len(chains)))
    ctx_blocks = n_ctx // rb
    for i in range(ctx_blocks):
        carry = sweep(sc_ref, lambda b: 0, lambda b: b * n_ctx, ctx_blocks, i, carry, None)

    n_blocks = n_chunks // rb
    lax.fori_loop(
        0, n_blocks,
        lambda i, c: sweep(s_ref, lambda b: b, lambda b: 0, n_blocks, i, c, xs_ref),
        carry)
    x_ref[...] = xs_ref[...].astype(_BF16)


def _chunk_scan(s_loc, s_ctx, a_re, a_im, n_ctx):
    bsz, _, n_chunks, _ = s_loc.shape
    assert n_chunks % SUBLANES == 0 and n_ctx % SUBLANES == 0
    blk = (bsz, SCAN_PAIRS, n_chunks, PAIR_LANES)
    return pl.pallas_call(
        functools.partial(_scan_kernel, bsz=bsz, n_chunks=n_chunks, n_ctx=n_ctx),
        grid=(PAIRS // SCAN_PAIRS,),
        in_specs=[
            pl.BlockSpec(blk, lambda i: (0, i, 0, 0)),
            pl.BlockSpec((1, SCAN_PAIRS, SCAN_TILE, PAIR_LANES), lambda i: (0, i, 0, 0)),
            pl.BlockSpec((SCAN_PAIRS, 1, PAIR_LANES), lambda i: (i, 0, 0)),
            pl.BlockSpec((SCAN_PAIRS, 1, PAIR_LANES), lambda i: (i, 0, 0)),
        ],
        out_specs=pl.BlockSpec(blk, lambda i: (0, i, 0, 0)),
        out_shape=jax.ShapeDtypeStruct(s_loc.shape, _BF16),
        scratch_shapes=[pltpu.VMEM(blk, _F32),
                        pltpu.VMEM((2 * SCAN_PAIRS, 8, SUBLANES, LANE), _F32)],
        compiler_params=pltpu.CompilerParams(
            dimension_semantics=("arbitrary",), vmem_limit_bytes=VMEM_LIMIT),
        name="s5_chunk_scan",
    )(s_loc, s_ctx, a_re, a_im)


def _readout_kernel(uflat_ref, x_ref, mu_ref, mv_ref, y_ref, yt_ref, ys_ref):
    def pair_body(q, carry):
        y0 = _dot(uflat_ref[0, 2 * q], mu_ref[2 * q])
        y1 = _dot(uflat_ref[0, 2 * q + 1], mu_ref[2 * q + 1])
        yx = _dot(x_ref[0, q], mv_ref[q])
        yt = (jnp.concatenate([y0, y1], axis=1) + yx).T
        yt_ref[2 * q] = yt[:FLAT]
        yt_ref[2 * q + 1] = yt[FLAT:]
        return carry

    lax.fori_loop(0, PAIRS, pair_body, 0, unroll=RELAYOUT_UNROLL)

    def slab_body(sg, carry):
        r0 = pl.multiple_of(sg * GROUP_CH, GROUP_CH)
        for cb in range(COL_BLOCKS):
            yt = yt_ref[cb * 8:(cb + 1) * 8, pl.ds(r0, GROUP_CH), :].reshape(LANE, SCAN_TILE)
            ys_ref[cb, pl.ds(sg, SCAN_TILE, stride=CHUNK), :] = yt.T
        return carry

    lax.fori_loop(0, CHUNK, slab_body, 0, unroll=RELAYOUT_UNROLL)
    for cb in range(COL_BLOCKS):
        y_ref[0, :, cb * LANE:(cb + 1) * LANE] = ys_ref[cb].astype(y_ref.dtype)


def _readout(uflat, xstart, mu, mv):
    bsz, _, n_chunks, _ = uflat.shape
    nt = n_chunks // SCAN_TILE
    return pl.pallas_call(
        _readout_kernel,
        grid=(bsz, nt),
        in_specs=[
            pl.BlockSpec((1, GROUPS, SCAN_TILE, FLAT), lambda b, t: (b, 0, t, 0)),
            pl.BlockSpec((1, PAIRS, SCAN_TILE, PAIR_LANES), lambda b, t: (b, 0, t, 0)),
            _const_spec((GROUPS, FLAT, FLAT)),
            _const_spec((PAIRS, PAIR_LANES, 2 * FLAT)),
        ],
        out_specs=pl.BlockSpec((1, SCAN_ROWS, D_SSM), lambda b, t: (b, t, 0)),
        out_shape=jax.ShapeDtypeStruct((bsz, n_chunks * CHUNK, D_SSM), _BF16),
        scratch_shapes=[
            pltpu.VMEM((GROUPS, FLAT, SCAN_TILE), _F32),
            pltpu.VMEM((COL_BLOCKS, SCAN_ROWS, LANE), _F32),
        ],
        compiler_params=pltpu.CompilerParams(
            dimension_semantics=("arbitrary", "arbitrary"), vmem_limit_bytes=VMEM_LIMIT),
        name="s5_readout",
    )(uflat, xstart, mu, mv)


def _window_sum(u, w):
    assert w // 2 <= SUBLANES
    rows, lanes = u.shape
    nb = rows // GRID_W
    pad = jnp.zeros((nb, SUBLANES, lanes), _F32)
    z = jnp.concatenate([pad, u.reshape(nb, GRID_W, lanes), pad], axis=1)
    n = nb * (GRID_W + 2 * SUBLANES)
    z = z.reshape(n, lanes)
    acc = z + pltpu.roll(z, 1, 0)
    m = 2
    while m < w:
        acc = pltpu.roll(acc, m // 2, 0) + pltpu.roll(acc, n - m // 2, 0)
        m *= 2
    return acc.reshape(nb, GRID_W + 2 * SUBLANES, lanes)[:, SUBLANES:SUBLANES + GRID_W, :].reshape(rows, lanes)


def _mix_kernel(x_ref, y_ref, sh_ref, sc_ref, gt_ref, g_ref, bglu_ref, pinv_ref, ps_ref, pw_ref,
                win_hbm, wglu_hbm, wa_hbm, wb_hbm, wo_hbm, o_ref,
                wr_ref, wglu_ref, wa_ref, wb_ref, wo_ref, stage_ref, sem):
    @pl.when(_first_step())
    def _load_weights():
        r = stage_ref.shape[1]
        n_rest = D_POOL + 2 * D_MODEL

        def pieces(src3, dst, n_rows, c0, n_cols):
            return [(src3.at[0, pl.ds(k * r, r), pl.ds(c0, n_cols)],
                     [(dst.at[pl.ds(k * r, r), :], slice(0, n_cols))]) for k in range(n_rows // r)]

        jobs = pieces(win_hbm, wr_ref, D_MODEL, D_SSM, n_rest)
        jobs += pieces(wglu_hbm, wglu_ref, D_SSM, 0, D_SSM)
        jobs += pieces(wa_hbm, wa_ref, D_SSM, 0, D_MODEL)
        jobs += pieces(wb_hbm, wb_ref, D_POOL, 0, D_MODEL)
        jobs += pieces(wo_hbm, wo_ref, D_MODEL, 0, D_MODEL)
        _stream_cast(jobs, stage_ref, sem)

    x = x_ref[0]
    h = _rms_mod(x, g_ref[...] * (1.0 + sc_ref[0]), sh_ref[0]).astype(_BF16)
    c_pool, c_ga, c_gb = 0, D_POOL, D_POOL + D_MODEL
    windows = range(len(POOL_WINDOWS))
    group = lambda wi: slice(wi * POOL_GROUP_CH, (wi + 1) * POOL_GROUP_CH)

    ub = _dot(h, wr_ref[:, c_pool:c_ga])
    y = jax.nn.gelu(y_ref[0].astype(_F32))
    glu = _dot(y.astype(_BF16), wglu_ref[...])
    wsums = [_window_sum(ub[:, group(wi)], POOL_WINDOWS[wi]) for wi in windows]
    gate_a = _dot(h, wr_ref[:, c_ga:c_gb])
    z = y * jax.nn.sigmoid(glu + bglu_ref[...])
    ya = _dot(z.astype(_BF16), wa_ref[...])
    outs = [_dot((wsums[wi] * pinv_ref[wi] - ub[:, group(wi)]).astype(_BF16), pw_ref[wi])
            for wi in windows]
    gate_b = _dot(h, wr_ref[:, c_gb:])
    pb = jnp.concatenate(outs, axis=1) * ps_ref[...]
    yb = _dot(pb.astype(_BF16), wb_ref[...])

    merged = jax.nn.sigmoid(gate_a) * ya + jax.nn.sigmoid(gate_b) * yb
    mixed = _dot(merged.astype(_BF16), wo_ref[...])
    o_ref[0] = x + gt_ref[0] * mixed


def _pool_inverse_counts(rows):
    pos = np.arange(rows) % GRID_W
    invs = []
    for w in POOL_WINDOWS:
        lo = np.clip(pos - w // 2, 0, GRID_W - 1)
        hi = np.clip(pos + w - 1 - w // 2, 0, GRID_W - 1) + 1
        invs.append(np.broadcast_to((1.0 / (hi - lo).astype(np.float32))[:, None], (rows, POOL_GROUP_CH)))
    return np.stack(invs)


def _mixer(x, ypre, mod3, norm_g, b_glu, pool_scale, w_in, w_glu, w_a, pool_w, w_b, w_out):
    bsz, n_tok, _ = x.shape
    tm = MIX_ROWS
    pinv = jnp.asarray(_pool_inverse_counts(tm), _F32)
    nw = len(POOL_WINDOWS)
    n_rest = D_POOL + 2 * D_MODEL
    hbm = pl.BlockSpec(memory_space=pl.ANY)
    return pl.pallas_call(
        _mix_kernel,
        grid=(bsz, n_tok // tm),
        in_specs=[
            pl.BlockSpec((1, tm, D_MODEL), lambda b, t: (b, t, 0)),
            pl.BlockSpec((1, tm, D_SSM), lambda b, t: (b, t, 0)),
            pl.BlockSpec((1, 1, D_MODEL), lambda b, t: (b, 0, 0)),
            pl.BlockSpec((1, 1, D_MODEL), lambda b, t: (b, 0, 1)),
            pl.BlockSpec((1, 1, D_MODEL), lambda b, t: (b, 0, 2)),
            _const_spec((1, D_MODEL)),
            _const_spec((1, D_SSM)),
            _const_spec((nw, tm, POOL_GROUP_CH)),
            _const_spec((1, D_POOL)),
            _const_spec((None, nw, POOL_GROUP_CH, POOL_GROUP_CH)),
            hbm, hbm, hbm, hbm, hbm,
        ],
        out_specs=pl.BlockSpec((1, tm, D_MODEL), lambda b, t: (b, t, 0)),
        out_shape=jax.ShapeDtypeStruct(x.shape, _F32),
        scratch_shapes=[
            pltpu.VMEM((D_MODEL, n_rest), _BF16),
            pltpu.VMEM((D_SSM, D_SSM), _BF16),
            pltpu.VMEM((D_SSM, D_MODEL), _BF16),
            pltpu.VMEM((D_POOL, D_MODEL), _BF16),
            pltpu.VMEM((D_MODEL, D_MODEL), _BF16),
            pltpu.VMEM((WEIGHT_STAGE_SLOTS, 2 * WEIGHT_STAGE_ROWS, n_rest), _F32),
            pltpu.SemaphoreType.DMA((WEIGHT_STAGE_SLOTS,)),
        ],
        compiler_params=pltpu.CompilerParams(
            dimension_semantics=("arbitrary", "arbitrary"), vmem_limit_bytes=VMEM_LIMIT),
        name="token_mixer",
    )(x, ypre, mod3, mod3, mod3, norm_g, b_glu, pinv, pool_scale, pool_w.astype(_BF16),
      w_in, w_glu, w_a, w_b, w_out)


def _stream_cast(jobs, stage_ref, sem):
    n_slots = stage_ref.shape[0]

    def copy(i):
        src = jobs[i][0]
        rows, cols = src.shape
        slot = i % n_slots
        return pltpu.make_async_copy(src, stage_ref.at[slot, pl.ds(0, rows), pl.ds(0, cols)], sem.at[slot])

    for i in range(min(n_slots - 1, len(jobs))):
        copy(i).start()
    for i, (src, dsts) in enumerate(jobs):
        if i + n_slots - 1 < len(jobs):
            copy(i + n_slots - 1).start()
        copy(i).wait()
        rows = src.shape[0]
        for dst, cols in dsts:
            dst[...] = stage_ref[i % n_slots, 0:rows, cols].astype(_BF16)


def _first_step():
    return jnp.logical_and(pl.program_id(0) == 0, pl.program_id(1) == 0)


def _ffn_kernel(x_ref, sh_ref, sc_ref, gt_ref, g2_ref, gf_ref, win_hbm, wout_hbm, o_ref,
                wg_ref, wu_ref, wo_ref, stage_in_ref, stage_out_ref, sem_in, sem_out):
    @pl.when(_first_step())
    def _load_weights():
        r = stage_in_ref.shape[1]
        _stream_cast([(win_hbm.at[0, pl.ds(k * r, r), :],
                       [(wg_ref.at[pl.ds(k * r, r), :], slice(0, FFN_HIDDEN)),
                        (wu_ref.at[pl.ds(k * r, r), :], slice(FFN_HIDDEN, 2 * FFN_HIDDEN))])
                      for k in range(D_MODEL // r)], stage_in_ref, sem_in)
        r = stage_out_ref.shape[1]
        _stream_cast([(wout_hbm.at[0, pl.ds(k * r, r), :], [(wo_ref.at[pl.ds(k * r, r), :], slice(0, D_MODEL))])
                      for k in range(FFN_HIDDEN // r)], stage_out_ref, sem_out)

    x = x_ref[0]
    h = _rms_mod(x, g2_ref[...] * (1.0 + sc_ref[0]), sh_ref[0]).astype(_BF16)
    gate = _dot(h, wg_ref[...])
    up = _dot(h, wu_ref[...])
    act = (gate * jax.nn.sigmoid(gate) * up).astype(_BF16)
    y = x + gt_ref[0] * _dot(act, wo_ref[...])
    ms = jnp.mean(y * y, axis=-1, keepdims=True)
    o_ref[0] = (y * lax.rsqrt(ms + RMS_EPS)) * gf_ref[...]


def _ffn(x1, mod3, norm2_g, final_g, w_ffn, w_down):
    bsz, n_tok, _ = x1.shape
    tm = FFN_ROWS
    hbm = pl.BlockSpec(memory_space=pl.ANY)
    return pl.pallas_call(
        _ffn_kernel,
        grid=(bsz, n_tok // tm),
        in_specs=[
            pl.BlockSpec((1, tm, D_MODEL), lambda b, t: (b, t, 0)),
            pl.BlockSpec((1, 1, D_MODEL), lambda b, t: (b, 0, 3)),
            pl.BlockSpec((1, 1, D_MODEL), lambda b, t: (b, 0, 4)),
            pl.BlockSpec((1, 1, D_MODEL), lambda b, t: (b, 0, 5)),
            _const_spec((1, D_MODEL)),
            _const_spec((1, D_MODEL)),
            hbm,
            hbm,
        ],
        out_specs=pl.BlockSpec((1, tm, D_MODEL), lambda b, t: (b, t, 0)),
        out_shape=jax.ShapeDtypeStruct(x1.shape, _F32),
        scratch_shapes=[
            pltpu.VMEM((D_MODEL, FFN_HIDDEN), _BF16),
            pltpu.VMEM((D_MODEL, FFN_HIDDEN), _BF16),
            pltpu.VMEM((FFN_HIDDEN, D_MODEL), _BF16),
            pltpu.VMEM((WEIGHT_STAGE_SLOTS, WEIGHT_STAGE_ROWS, 2 * FFN_HIDDEN), _F32),
            pltpu.VMEM((WEIGHT_STAGE_SLOTS, FFN_HIDDEN // SUBLANES, D_MODEL), _F32),
            pltpu.SemaphoreType.DMA((WEIGHT_STAGE_SLOTS,)),
            pltpu.SemaphoreType.DMA((WEIGHT_STAGE_SLOTS,)),
        ],
        compiler_params=pltpu.CompilerParams(
            dimension_semantics=("arbitrary", "arbitrary"), vmem_limit_bytes=VMEM_LIMIT),
        name="swiglu_ffn",
    )(x1, mod3, mod3, mod3, norm2_g, final_g, w_ffn, w_down)


def _cmul(xr, xi, yr, yi):
    return xr * yr - xi * yi, xr * yi + xi * yr


def _tables_kernel(a_ref, bt_ref, c_ref, d_ref,
                   mu_ref, wst_ref, mv_ref, a16r_ref, a16i_ref, kt_ref, khl_ref, vnat_ref, cp_ref):
    nt_dims = (((1,), (1,)), ((), ()))

    def pair_body(q, carry):
        for d in range(2):
            ar = a_ref[0, d, q]
            ai = a_ref[1, d, q]
            dt = jnp.exp(a_ref[2, d, q])
            mag = jnp.exp(ar * dt)
            ang = ai * dt
            abr, abi = mag * jnp.cos(ang), mag * jnp.sin(ang)
            den = ar * ar + ai * ai
            fr = ((abr - 1.0) * ar + abi * ai) / den
            fi = (abi * ar - (abr - 1.0) * ai) / den
            bbr, bbi = _cmul(bt_ref[0, d, q], bt_ref[1, d, q], fr, fi)
            cr = c_ref[0, d, q]
            ci = c_ref[1, d, q]
            pw = [(jnp.ones_like(ar), jnp.zeros_like(ar))]
            for _ in range(CHUNK):
                pw.append(_cmul(pw[-1][0], pw[-1][1], abr, abi))
            lanes_re = slice(2 * d * LANE, (2 * d + 1) * LANE)
            lanes_im = slice((2 * d + 1) * LANE, (2 * d + 2) * LANE)
            for lanes in (lanes_re, lanes_im):
                a16r_ref[q, :, lanes] = pw[CHUNK][0]
                a16i_ref[q, :, lanes] = pw[CHUNK][1]
            for sg in range(CHUNK):
                e = (CHUNK - 1 - sg) if d == 0 else sg
                wr, wi = _cmul(bbr, bbi, pw[e][0], pw[e][1])
                e = (sg + 1) if d == 0 else (CHUNK - sg)
                vr, vi = _cmul(cr, ci, pw[e][0], pw[e][1])
                for gg in range(2):
                    src = slice(gg * GROUP_CH, (gg + 1) * GROUP_CH)
                    dst = slice(gg * FLAT + sg * GROUP_CH, gg * FLAT + (sg + 1) * GROUP_CH)
                    wst_ref[q, dst, lanes_re] = wr[src].astype(_BF16)
                    wst_ref[q, dst, lanes_im] = wi[src].astype(_BF16)
                    vnat_ref[2 * d, dst, :] = vr[src]
                    vnat_ref[2 * d + 1, dst, :] = -vi[src]
            for gg in range(2):
                src = slice(gg * GROUP_CH, (gg + 1) * GROUP_CH)
                for k in range(CHUNK):
                    e = k if d == 0 else (CHUNK - 1 - k)
                    pr, pi = _cmul(cr[src], ci[src], pw[e][0], pw[e][1])
                    cp_ref[0, k * GROUP_CH:(k + 1) * GROUP_CH, :] = pr
                    cp_ref[1, k * GROUP_CH:(k + 1) * GROUP_CH, :] = pi
                kt = (lax.dot_general(bbr[src], cp_ref[0], nt_dims, precision=_HI, preferred_element_type=_F32)
                      - lax.dot_general(bbi[src], cp_ref[1], nt_dims, precision=_HI, preferred_element_type=_F32))
                r0 = pl.multiple_of((2 * q + gg) * GROUP_CH, GROUP_CH)
                kt_ref[d, pl.ds(r0, GROUP_CH), :] = kt
        for part in range(4):
            mv_ref[q, part * LANE:(part + 1) * LANE, :] = vnat_ref[part].T.astype(_BF16)
        return carry

    lax.fori_loop(0, PAIRS, pair_body, 0)

    for d in range(2):
        kt = kt_ref[d]
        hi = kt.astype(_BF16)
        khl_ref[2 * d] = hi
        khl_ref[2 * d + 1] = (kt - hi.astype(_F32)).astype(_BF16)

    row = lax.broadcasted_iota(jnp.int32, (FLAT, FLAT), 0)
    col = lax.broadcasted_iota(jnp.int32, (FLAT, FLAT), 1)
    same_ch = (row % GROUP_CH) == (col % GROUP_CH)
    row_blk = row // GROUP_CH
    col_blk = col // GROUP_CH
    orow = lax.broadcasted_iota(jnp.int32, (GROUPS * GROUP_CH, FLAT), 0)
    ocol = lax.broadcasted_iota(jnp.int32, (GROUPS * GROUP_CH, FLAT), 1)
    skip_ch = (orow % GROUP_CH) == (ocol % GROUP_CH)
    ocol_blk = ocol // GROUP_CH
    d_col = d_ref[...]

    def toeplitz_body(sg, carry):
        sf = jnp.where(same_ch & (row_blk + sg == col_blk), 1.0, 0.0).astype(_BF16)
        sb = jnp.where(same_ch & (row_blk == col_blk + (CHUNK - 1) - sg), 1.0, 0.0).astype(_BF16)
        out = (_dot(khl_ref[0], sf) + _dot(khl_ref[1], sf)) + (_dot(khl_ref[2], sb) + _dot(khl_ref[3], sb))
        out = out + jnp.where(skip_ch & (ocol_blk == sg), d_col, 0.0)
        r0 = pl.multiple_of(sg * GROUP_CH, GROUP_CH)
        mu_ref[:, pl.ds(r0, GROUP_CH), :] = out.reshape(GROUPS, GROUP_CH, FLAT).astype(_BF16)
        return carry

    lax.fori_loop(0, CHUNK, toeplitz_body, 0)


def _s5_tables(a_re, a_im, log_dt, b_re, b_im, c_re, c_im, d_skip):
    f32 = _F32
    eye2 = jnp.eye(2, dtype=f32)

    def pair_blocks(re, im):
        v = jnp.stack([re, im]).astype(f32).reshape(2, 2, PAIRS, 2, GROUP_CH, 1, STATE)
        v = v * eye2[None, None, None, :, None, :, None]
        return v.reshape(2, 2, PAIRS, 2 * GROUP_CH, 2 * STATE)

    ldt = jnp.broadcast_to(log_dt.astype(f32)[..., None], (2, GROUPS, STATE))
    a_rows = jnp.stack([a_re.astype(f32), a_im.astype(f32), ldt]).reshape(3, 2, PAIRS, 1, 2 * STATE)
    args = (a_rows, pair_blocks(jnp.swapaxes(b_re, 2, 3), jnp.swapaxes(b_im, 2, 3)),
            pair_blocks(c_re, c_im), d_skip.astype(f32).reshape(D_SSM, 1))
    whole = lambda a: pl.BlockSpec(a.shape, lambda i, n=a.ndim: (0,) * n)
    out_shape = [
        jax.ShapeDtypeStruct((GROUPS, FLAT, FLAT), _BF16),
        jax.ShapeDtypeStruct((PAIRS, 2 * FLAT, PAIR_LANES), _BF16),
        jax.ShapeDtypeStruct((PAIRS, PAIR_LANES, 2 * FLAT), _BF16),
        jax.ShapeDtypeStruct((PAIRS, 1, PAIR_LANES), _F32),
        jax.ShapeDtypeStruct((PAIRS, 1, PAIR_LANES), _F32),
    ]
    return pl.pallas_call(
        _tables_kernel,
        grid=(1,),
        in_specs=[whole(a) for a in args],
        out_specs=[whole(s) for s in out_shape],
        out_shape=out_shape,
        scratch_shapes=[
            pltpu.VMEM((2, GROUPS * GROUP_CH, FLAT), _F32),
            pltpu.VMEM((4, GROUPS * GROUP_CH, FLAT), _BF16),
            pltpu.VMEM((4, 2 * FLAT, LANE), _F32),
            pltpu.VMEM((2, FLAT, LANE), _F32),
        ],
        compiler_params=pltpu.CompilerParams(
            dimension_semantics=("arbitrary",), vmem_limit_bytes=VMEM_LIMIT),
        name="s5_tables",
    )(*args)


def kernel(x, c, ctx, c_ctx, w_mod, b_mod, norm1_g, norm2_g, w_in, s5_a_re, s5_a_im, s5_log_dt,
           s5_b_re, s5_b_im, s5_c_re, s5_c_im, s5_d, w_glu, b_glu, pool_w, pool_scale,
           w_branch_a, w_branch_b, w_out, w_ffn_in, w_ffn_out, final_norm_g):
    bsz, n_tok, d = x.shape
    ctx_len = ctx.shape[1]
    assert d == D_MODEL and w_mod.shape[0] == 1 and bsz + 1 <= SUBLANES
    assert n_tok % SCAN_ROWS == 0 and n_tok % MIX_ROWS == 0 and MIX_ROWS % GRID_W == 0
    assert ctx_len % CHUNK == 0 and bsz * ctx_len <= SCAN_ROWS
    n_ctx = ctx_len // CHUNK

    cc_t = jnp.concatenate(
        [c.T, c_ctx[:, None], jnp.zeros((D_MODEL, SUBLANES - bsz - 1), _F32)], axis=1)
    mod3 = _modulation(cc_t, w_mod, b_mod, bsz + 1)

    mu, wst, mv, a16_re, a16_im = _s5_tables(
        s5_a_re[0], s5_a_im[0], s5_log_dt[0], s5_b_re[0], s5_b_im[0], s5_c_re[0], s5_c_im[0], s5_d[0])

    uflat, s_loc = _pass1(x, mod3, 0, norm1_g, w_in, wst)
    _, s_ctx = _pass1(ctx.reshape(1, bsz * ctx_len, D_MODEL), mod3, bsz, norm1_g, w_in, wst)
    xstart = _chunk_scan(s_loc, s_ctx, a16_re, a16_im, n_ctx)
    ypre = _readout(uflat, xstart, mu, mv)

    x1 = _mixer(x, ypre, mod3, norm1_g, b_glu, pool_scale, w_in, w_glu, w_branch_a, pool_w, w_branch_b, w_out)
    return _ffn(x1, mod3, norm2_g, final_norm_g.reshape(1, D_MODEL), w_ffn_in, w_ffn_out)
```

```python
import functools

import numpy as np
import jax
import jax.numpy as jnp
from jax import lax
from jax.experimental import pallas as pl
from jax.experimental.pallas import tpu as pltpu

_F32 = jnp.float32
_BF16 = jnp.bfloat16

D_MODEL = 1024
D_SSM = 512
D_POOL = 512
GROUPS = 32
STATE = 64
GROUP_CH = 16
CHUNK = 16
FLAT = CHUNK * GROUP_CH
PAIRS = GROUPS // 2
PAIR_LANES = 4 * 2 * STATE
LANE = 128
SUBLANES = 8
COL_BLOCKS = D_SSM // LANE
GRID_W = 64
POOL_WINDOWS = (2, 4, 8, 16)
POOL_GROUP_CH = D_POOL // len(POOL_WINDOWS)
FFN_HIDDEN = 2816
RMS_EPS = 1e-6

SCAN_TILE = 128
SCAN_ROWS = SCAN_TILE * CHUNK
NORM_ROWS = 512
MIX_ROWS = 1024
FFN_ROWS = 512
SCAN_PAIRS = 4
RELAYOUT_UNROLL = 8
WEIGHT_STAGE_ROWS = 128
WEIGHT_STAGE_SLOTS = 3
VMEM_LIMIT = 56 * 1024 * 1024


def _rms_mod(x, gain, sh):
    ms = jnp.mean(x * x, axis=-1, keepdims=True)
    return (x * lax.rsqrt(ms + RMS_EPS)) * gain + sh


def _dot(a, b):
    return jnp.dot(a, b, preferred_element_type=_F32)


def _const_spec(shape, index=None):
    index = (0,) * len(shape) if index is None else index
    return pl.BlockSpec(shape, lambda *_: index, pipeline_mode=pl.Buffered(1))


def _mod_kernel(ct_ref, w_ref, b_ref, o_ref, *, n_rows):
    ct = ct_ref[...]
    a = ct * jax.nn.sigmoid(ct)
    w = w_ref[...]
    b = b_ref[...]
    for r in range(n_rows):
        o_ref[r] = jnp.sum(w * a[:, r:r + 1], axis=0, keepdims=True) + b
    for r in range(n_rows, o_ref.shape[0]):
        o_ref[r] = jnp.zeros_like(b)


def _modulation(cc_t, w_mod, b_mod, n_rows):
    n_out = w_mod.shape[-1]
    blk = 1024
    return pl.pallas_call(
        functools.partial(_mod_kernel, n_rows=n_rows),
        grid=(n_out // blk,),
        in_specs=[
            pl.BlockSpec((D_MODEL, SUBLANES), lambda i: (0, 0)),
            pl.BlockSpec((None, D_MODEL, blk), lambda i: (0, 0, i)),
            pl.BlockSpec((1, blk), lambda i: (0, i)),
        ],
        out_specs=pl.BlockSpec((SUBLANES, 1, blk), lambda i: (0, 0, i)),
        out_shape=jax.ShapeDtypeStruct((SUBLANES, 1, n_out), _F32),
        compiler_params=pltpu.CompilerParams(
            dimension_semantics=("arbitrary",), vmem_limit_bytes=VMEM_LIMIT),
        name="adaln_mod",
    )(cc_t, w_mod, b_mod)


def _p1_kernel(x_ref, sh_ref, sc_ref, g_ref, wa_ref, wst_ref, uflat_ref, s_ref, h_ref, u_ref, ut_ref,
               *, rows):
    gain = g_ref[...] * (1.0 + sc_ref[0])
    sh = sh_ref[0]
    wa = wa_ref[...].astype(_BF16)

    def norm_block(i):
        r = slice(i * NORM_ROWS, (i + 1) * NORM_ROWS)
        h_ref[r, :] = _rms_mod(x_ref[0, r, :], gain, sh).astype(_BF16)

    norm_block(0)
    for i in range(rows // NORM_ROWS):
        if (i + 1) * NORM_ROWS < rows:
            norm_block(i + 1)
        r = slice(i * NORM_ROWS, (i + 1) * NORM_ROWS)
        u = _dot(h_ref[r, :], wa)
        for cb in range(COL_BLOCKS):
            u_ref[cb, r, :] = u[:, cb * LANE:(cb + 1) * LANE]
    if rows < SCAN_ROWS:
        for cb in range(COL_BLOCKS):
            u_ref[cb, rows:, :] = jnp.zeros((SCAN_ROWS - rows, LANE), _F32)

    def slab_body(sg, carry):
        r0 = pl.multiple_of(sg * GROUP_CH, GROUP_CH)
        for cb in range(COL_BLOCKS):
            slab = u_ref[cb, pl.ds(sg, SCAN_TILE, stride=CHUNK), :]
            ut_ref[cb * 8:(cb + 1) * 8, pl.ds(r0, GROUP_CH), :] = slab.T.reshape(8, GROUP_CH, SCAN_TILE)
        return carry

    lax.fori_loop(0, CHUNK, slab_body, 0, unroll=RELAYOUT_UNROLL)

    def pair_body(q, carry):
        uf0 = ut_ref[2 * q].T.astype(_BF16)
        uf1 = ut_ref[2 * q + 1].T.astype(_BF16)
        uflat_ref[0, 2 * q] = uf0
        uflat_ref[0, 2 * q + 1] = uf1
        s_ref[0, q] = _dot(uf0, wst_ref[q, :FLAT, :]) + _dot(uf1, wst_ref[q, FLAT:, :])
        return carry

    lax.fori_loop(0, PAIRS, pair_body, 0, unroll=RELAYOUT_UNROLL)


def _pass1(x, mod3, mod_row0, norm_g, w_in, wst):
    bsz, n_tok, _ = x.shape
    rows = min(n_tok, SCAN_ROWS)
    assert n_tok % rows == 0 and rows % NORM_ROWS == 0
    nt = n_tok // rows
    n_chunks = nt * SCAN_TILE
    return pl.pallas_call(
        functools.partial(_p1_kernel, rows=rows),
        grid=(bsz, nt),
        in_specs=[
            pl.BlockSpec((1, rows, D_MODEL), lambda b, t: (b, t, 0)),
            pl.BlockSpec((1, 1, D_MODEL), lambda b, t: (b + mod_row0, 0, 0)),
            pl.BlockSpec((1, 1, D_MODEL), lambda b, t: (b + mod_row0, 0, 1)),
            _const_spec((1, D_MODEL)),
            _const_spec((None, D_MODEL, D_SSM)),
            _const_spec((PAIRS, 2 * FLAT, PAIR_LANES)),
        ],
        out_specs=[
            pl.BlockSpec((1, GROUPS, SCAN_TILE, FLAT), lambda b, t: (b, 0, t, 0)),
            pl.BlockSpec((1, PAIRS, SCAN_TILE, PAIR_LANES), lambda b, t: (b, 0, t, 0)),
        ],
        out_shape=[
            jax.ShapeDtypeStruct((bsz, GROUPS, n_chunks, FLAT), _BF16),
            jax.ShapeDtypeStruct((bsz, PAIRS, n_chunks, PAIR_LANES), _F32),
        ],
        scratch_shapes=[
            pltpu.VMEM((SCAN_ROWS, D_MODEL), _BF16),
            pltpu.VMEM((COL_BLOCKS, SCAN_ROWS, LANE), _F32),
            pltpu.VMEM((GROUPS, FLAT, SCAN_TILE), _F32),
        ],
        compiler_params=pltpu.CompilerParams(
            dimension_semantics=("arbitrary", "arbitrary"), vmem_limit_bytes=VMEM_LIMIT),
        name="s5_chunk_states",
    )(x, mod3, mod3, norm_g, w_in, wst)


def _scan_kernel(s_ref, sc_ref, ar_ref, ai_ref, x_ref, xs_ref, c_ref, *, bsz, n_chunks, n_ctx):
    rb = SUBLANES
    rid = lax.broadcasted_iota(jnp.int32, (rb, LANE), 0)
    chains = [(b, qq) for b in range(bsz) for qq in range(SCAN_PAIRS)]

    for qq in range(SCAN_PAIRS):
        for d in range(2):
            o = d * 2 * LANE
            a_r = ar_ref[qq][:, o:o + LANE]
            a_i = ai_ref[qq][:, o:o + LANE]
            pows = [(a_r, a_i)]
            for _ in range(rb - 1):
                pows.append(_cmul(pows[-1][0], pows[-1][1], a_r, a_i))
            idx = qq * 2 + d
            for k, shift in enumerate((1, 2, 4)):
                keep = (rid >= shift) if d == 0 else (rid < rb - shift)
                c_ref[idx, 2 * k] = jnp.where(keep, pows[shift - 1][0], 0.0)
                c_ref[idx, 2 * k + 1] = jnp.where(keep, pows[shift - 1][1], 0.0)
            p_r = jnp.zeros((rb, LANE), _F32)
            p_i = jnp.zeros((rb, LANE), _F32)
            for r in range(rb):
                e = r if d == 0 else rb - 1 - r
                p_r = jnp.where(rid == r, pows[e][0], p_r)
                p_i = jnp.where(rid == r, pows[e][1], p_i)
            c_ref[idx, 6] = p_r
            c_ref[idx, 7] = p_i

    def block(idx, d, s_re, s_im, xin_re, xin_im):
        def shifted(v, k):
            return pltpu.roll(v, k if d == 0 else rb - k, 0)

        t_re, t_im = s_re, s_im
        for k in range(3):
            a_r = c_ref[idx, 2 * k]
            a_i = c_ref[idx, 2 * k + 1]
            u_re, u_im = shifted(t_re, 1 << k), shifted(t_im, 1 << k)
            t_re, t_im = t_re + (a_r * u_re - a_i * u_im), t_im + (a_r * u_im + a_i * u_re)
        p_r = c_ref[idx, 6]
        p_i = c_ref[idx, 7]
        after_re = t_re + (p_r * xin_re - p_i * xin_im)
        after_im = t_im + (p_r * xin_im + p_i * xin_re)
        first = 0 if d == 0 else rb - 1
        last = rb - 1 - first
        start_re = jnp.where(rid == first, xin_re, shifted(after_re, 1))
        start_im = jnp.where(rid == first, xin_im, shifted(after_im, 1))
        return start_re, start_im, after_re[last:last + 1], after_im[last:last + 1]

    def sweep(src_ref, lead, row0, n_blocks, i, carry, dst_ref):
        out = []
        for ci, (b, qq) in enumerate(chains):
            for d in range(2):
                blk = i if d == 0 else n_blocks - 1 - i
                r0 = row0(b) + blk * rb
                if not isinstance(r0, int):
                    r0 = pl.multiple_of(r0, rb)
                o = d * 2 * LANE
                bi = lead(b)
                s_re = src_ref[bi, qq, pl.ds(r0, rb), o:o + LANE]
                s_im = src_ref[bi, qq, pl.ds(r0, rb), o + LANE:o + 2 * LANE]
                xin_re, xin_im = carry[4 * ci + 2 * d], carry[4 * ci + 2 * d + 1]
                st_re, st_im, xo_re, xo_im = block(qq * 2 + d, d, s_re, s_im, xin_re, xin_im)
                if dst_ref is not None:
                    dst_ref[bi, qq, pl.ds(r0, rb), o:o + LANE] = st_re
                    dst_ref[bi, qq, pl.ds(r0, rb), o + LANE:o + 2 * LANE] = st_im
                out += [xo_re, xo_im]
        return tuple(out)

    carry = tuple(jnp.zeros((1, LANE), _F32) for _ in range(4 * len(chains)))
    ctx_blocks = n_ctx // rb
    for i in range(ctx_blocks):
        carry = sweep(sc_ref, lambda b: 0, lambda b: b * n_ctx, ctx_blocks, i, carry, None)

    n_blocks = n_chunks // rb
    lax.fori_loop(
        0, n_blocks,
        lambda i, c: sweep(s_ref, lambda b: b, lambda b: 0, n_blocks, i, c, xs_ref),
        carry)
    x_ref[...] = xs_ref[...].astype(_BF16)


def _chunk_scan(s_loc, s_ctx, a_re, a_im, n_ctx):
    bsz, _, n_chunks, _ = s_loc.shape
    assert n_chunks % SUBLANES == 0 and n_ctx % SUBLANES == 0
    blk = (bsz, SCAN_PAIRS, n_chunks, PAIR_LANES)
    return pl.pallas_call(
        functools.partial(_scan_kernel, bsz=bsz, n_chunks=n_chunks, n_ctx=n_ctx),
        grid=(PAIRS // SCAN_PAIRS,),
        in_specs=[
            pl.BlockSpec(blk, lambda i: (0, i, 0, 0)),
            pl.BlockSpec((1, SCAN_PAIRS, SCAN_TILE, PAIR_LANES), lambda i: (0, i, 0, 0)),
            pl.BlockSpec((SCAN_PAIRS, 1, PAIR_LANES), lambda i: (i, 0, 0)),
            pl.BlockSpec((SCAN_PAIRS, 1, PAIR_LANES), lambda i: (i, 0, 0)),
        ],
        out_specs=pl.BlockSpec(blk, lambda i: (0, i, 0, 0)),
        out_shape=jax.ShapeDtypeStruct(s_loc.shape, _BF16),
        scratch_shapes=[pltpu.VMEM(blk, _F32),
                        pltpu.VMEM((2 * SCAN_PAIRS, 8, SUBLANES, LANE), _F32)],
        compiler_params=pltpu.CompilerParams(
            dimension_semantics=("arbitrary",), vmem_limit_bytes=VMEM_LIMIT),
        name="s5_chunk_scan",
    )(s_loc, s_ctx, a_re, a_im)


def _readout_kernel(uflat_ref, x_ref, mu_ref, mv_ref, y_ref, yt_ref, ys_ref):
    def pair_body(q, carry):
        y0 = _dot(uflat_ref[0, 2 * q], mu_ref[2 * q])
        y1 = _dot(uflat_ref[0, 2 * q + 1], mu_ref[2 * q + 1])
        yx = _dot(x_ref[0, q], mv_ref[q])
        yt = (jnp.concatenate([y0, y1], axis=1) + yx).T
        yt_ref[2 * q] = yt[:FLAT]
        yt_ref[2 * q + 1] = yt[FLAT:]
        return carry

    lax.fori_loop(0, PAIRS, pair_body, 0, unroll=RELAYOUT_UNROLL)

    def slab_body(sg, carry):
        r0 = pl.multiple_of(sg * GROUP_CH, GROUP_CH)
        for cb in range(COL_BLOCKS):
            yt = yt_ref[cb * 8:(cb + 1) * 8, pl.ds(r0, GROUP_CH), :].reshape(LANE, SCAN_TILE)
            ys_ref[cb, pl.ds(sg, SCAN_TILE, stride=CHUNK), :] = yt.T
        return carry

    lax.fori_loop(0, CHUNK, slab_body, 0, unroll=RELAYOUT_UNROLL)
    for cb in range(COL_BLOCKS):
        y_ref[0, :, cb * LANE:(cb + 1) * LANE] = ys_ref[cb].astype(y_ref.dtype)


def _readout(uflat, xstart, mu, mv):
    bsz, _, n_chunks, _ = uflat.shape
    nt = n_chunks // SCAN_TILE
    return pl.pallas_call(
        _readout_kernel,
        grid=(bsz, nt),
        in_specs=[
            pl.BlockSpec((1, GROUPS, SCAN_TILE, FLAT), lambda b, t: (b, 0, t, 0)),
            pl.BlockSpec((1, PAIRS, SCAN_TILE, PAIR_LANES), lambda b, t: (b, 0, t, 0)),
            _const_spec((GROUPS, FLAT, FLAT)),
            _const_spec((PAIRS, PAIR_LANES, 2 * FLAT)),
        ],
        out_specs=pl.BlockSpec((1, SCAN_ROWS, D_SSM), lambda b, t: (b, t, 0)),
        out_shape=jax.ShapeDtypeStruct((bsz, n_chunks * CHUNK, D_SSM), _BF16),
        scratch_shapes=[
            pltpu.VMEM((GROUPS, FLAT, SCAN_TILE), _F32),
            pltpu.VMEM((COL_BLOCKS, SCAN_ROWS, LANE), _F32),
        ],
        compiler_params=pltpu.CompilerParams(
            dimension_semantics=("arbitrary", "arbitrary"), vmem_limit_bytes=VMEM_LIMIT),
        name="s5_readout",
    )(uflat, xstart, mu, mv)


def _window_sum(u, w):
    assert w // 2 <= SUBLANES
    rows, lanes = u.shape
    nb = rows // GRID_W
    pad = jnp.zeros((nb, SUBLANES, lanes), _F32)
    z = jnp.concatenate([pad, u.reshape(nb, GRID_W, lanes), pad], axis=1)
    n = nb * (GRID_W + 2 * SUBLANES)
    z = z.reshape(n, lanes)
    acc = z + pltpu.roll(z, 1, 0)
    m = 2
    while m < w:
        acc = pltpu.roll(acc, m // 2, 0) + pltpu.roll(acc, n - m // 2, 0)
        m *= 2
    return acc.reshape(nb, GRID_W + 2 * SUBLANES, lanes)[:, SUBLANES:SUBLANES + GRID_W, :].reshape(rows, lanes)


def _mix_kernel(x_ref, y_ref, sh_ref, sc_ref, gt_ref, g_ref, bglu_ref, pinv_ref, ps_ref, pw_ref,
                win_hbm, wglu_hbm, wa_hbm, wb_hbm, wo_hbm, o_ref,
                wr_ref, wglu_ref, wa_ref, wb_ref, wo_ref, stage_ref, sem):
    @pl.when(_first_step())
    def _load_weights():
        r = stage_ref.shape[1]
        n_rest = D_POOL + 2 * D_MODEL

        def pieces(src3, dst, n_rows, c0, n_cols):
            return [(src3.at[0, pl.ds(k * r, r), pl.ds(c0, n_cols)],
                     [(dst.at[pl.ds(k * r, r), :], slice(0, n_cols))]) for k in range(n_rows // r)]

        jobs = pieces(win_hbm, wr_ref, D_MODEL, D_SSM, n_rest)
        jobs += pieces(wglu_hbm, wglu_ref, D_SSM, 0, D_SSM)
        jobs += pieces(wa_hbm, wa_ref, D_SSM, 0, D_MODEL)
        jobs += pieces(wb_hbm, wb_ref, D_POOL, 0, D_MODEL)
        jobs += pieces(wo_hbm, wo_ref, D_MODEL, 0, D_MODEL)
        _stream_cast(jobs, stage_ref, sem)

    x = x_ref[0]
    h = _rms_mod(x, g_ref[...] * (1.0 + sc_ref[0]), sh_ref[0]).astype(_BF16)
    c_pool, c_ga, c_gb = 0, D_POOL, D_POOL + D_MODEL
    windows = range(len(POOL_WINDOWS))
    group = lambda wi: slice(wi * POOL_GROUP_CH, (wi + 1) * POOL_GROUP_CH)

    ub = _dot(h, wr_ref[:, c_pool:c_ga])
    y = jax.nn.gelu(y_ref[0].astype(_F32))
    glu = _dot(y.astype(_BF16), wglu_ref[...])
    wsums = [_window_sum(ub[:, group(wi)], POOL_WINDOWS[wi]) for wi in windows]
    gate_a = _dot(h, wr_ref[:, c_ga:c_gb])
    z = y * jax.nn.sigmoid(glu + bglu_ref[...])
    ya = _dot(z.astype(_BF16), wa_ref[...])
    outs = [_dot((wsums[wi] * pinv_ref[wi] - ub[:, group(wi)]).astype(_BF16), pw_ref[wi])
            for wi in windows]
    gate_b = _dot(h, wr_ref[:, c_gb:])
    pb = jnp.concatenate(outs, axis=1) * ps_ref[...]
    yb = _dot(pb.astype(_BF16), wb_ref[...])

    merged = jax.nn.sigmoid(gate_a) * ya + jax.nn.sigmoid(gate_b) * yb
    mixed = _dot(merged.astype(_BF16), wo_ref[...])
    o_ref[0] = x + gt_ref[0] * mixed


def _pool_inverse_counts(rows):
    pos = np.arange(rows) % GRID_W
    invs = []
    for w in POOL_WINDOWS:
        lo = np.clip(pos - w // 2, 0, GRID_W - 1)
        hi = np.clip(pos + w - 1 - w // 2, 0, GRID_W - 1) + 1
        invs.append(np.broadcast_to((1.0 / (hi - lo).astype(np.float32))[:, None], (rows, POOL_GROUP_CH)))
    return np.stack(invs)


def _mixer(x, ypre, mod3, norm_g, b_glu, pool_scale, w_in, w_glu, w_a, pool_w, w_b, w_out):
    bsz, n_tok, _ = x.shape
    tm = MIX_ROWS
    pinv = jnp.asarray(_pool_inverse_counts(tm), _F32)
    nw = len(POOL_WINDOWS)
    n_rest = D_POOL + 2 * D_MODEL
    hbm = pl.BlockSpec(memory_space=pl.ANY)
    return pl.pallas_call(
        _mix_kernel,
        grid=(bsz, n_tok // tm),
        in_specs=[
            pl.BlockSpec((1, tm, D_MODEL), lambda b, t: (b, t, 0)),
            pl.BlockSpec((1, tm, D_SSM), lambda b, t: (b, t, 0)),
            pl.BlockSpec((1, 1, D_MODEL), lambda b, t: (b, 0, 0)),
            pl.BlockSpec((1, 1, D_MODEL), lambda b, t: (b, 0, 1)),
            pl.BlockSpec((1, 1, D_MODEL), lambda b, t: (b, 0, 2)),
            _const_spec((1, D_MODEL)),
            _const_spec((1, D_SSM)),
            _const_spec((nw, tm, POOL_GROUP_CH)),
            _const_spec((1, D_POOL)),
            _const_spec((None, nw, POOL_GROUP_CH, POOL_GROUP_CH)),
            hbm, hbm, hbm, hbm, hbm,
        ],
        out_specs=pl.BlockSpec((1, tm, D_MODEL), lambda b, t: (b, t, 0)),
        out_shape=jax.ShapeDtypeStruct(x.shape, _F32),
        scratch_shapes=[
            pltpu.VMEM((D_MODEL, n_rest), _BF16),
            pltpu.VMEM((D_SSM, D_SSM), _BF16),
            pltpu.VMEM((D_SSM, D_MODEL), _BF16),
            pltpu.VMEM((D_POOL, D_MODEL), _BF16),
            pltpu.VMEM((D_MODEL, D_MODEL), _BF16),
            pltpu.VMEM((WEIGHT_STAGE_SLOTS, 2 * WEIGHT_STAGE_ROWS, n_rest), _F32),
            pltpu.SemaphoreType.DMA((WEIGHT_STAGE_SLOTS,)),
        ],
        compiler_params=pltpu.CompilerParams(
            dimension_semantics=("arbitrary", "arbitrary"), vmem_limit_bytes=VMEM_LIMIT),
        name="token_mixer",
    )(x, ypre, mod3, mod3, mod3, norm_g, b_glu, pinv, pool_scale, pool_w.astype(_BF16),
      w_in, w_glu, w_a, w_b, w_out)


def _stream_cast(jobs, stage_ref, sem):
    n_slots = stage_ref.shape[0]

    def copy(i):
        src = jobs[i][0]
        rows, cols = src.shape
        slot = i % n_slots
        return pltpu.make_async_copy(src, stage_ref.at[slot, pl.ds(0, rows), pl.ds(0, cols)], sem.at[slot])

    for i in range(min(n_slots - 1, len(jobs))):
        copy(i).start()
    for i, (src, dsts) in enumerate(jobs):
        if i + n_slots - 1 < len(jobs):
            copy(i + n_slots - 1).start()
        copy(i).wait()
        rows = src.shape[0]
        for dst, cols in dsts:
            dst[...] = stage_ref[i % n_slots, 0:rows, cols].astype(_BF16)


def _first_step():
    return jnp.logical_and(pl.program_id(0) == 0, pl.program_id(1) == 0)


def _ffn_kernel(x_ref, sh_ref, sc_ref, gt_ref, g2_ref, gf_ref, win_hbm, wout_hbm, o_ref,
                wg_ref, wu_ref, wo_ref, stage_in_ref, stage_out_ref, sem_in, sem_out):
    @pl.when(_first_step())
    def _load_weights():
        r = stage_in_ref.shape[1]
        _stream_cast([(win_hbm.at[0, pl.ds(k * r, r), :],
                       [(wg_ref.at[pl.ds(k * r, r), :], slice(0, FFN_HIDDEN)),
                        (wu_ref.at[pl.ds(k * r, r), :], slice(FFN_HIDDEN, 2 * FFN_HIDDEN))])
                      for k in range(D_MODEL // r)], stage_in_ref, sem_in)
        r = stage_out_ref.shape[1]
        _stream_cast([(wout_hbm.at[0, pl.ds(k * r, r), :], [(wo_ref.at[pl.ds(k * r, r), :], slice(0, D_MODEL))])
                      for k in range(FFN_HIDDEN // r)], stage_out_ref, sem_out)

    x = x_ref[0]
    h = _rms_mod(x, g2_ref[...] * (1.0 + sc_ref[0]), sh_ref[0]).astype(_BF16)
    gate = _dot(h, wg_ref[...])
    up = _dot(h, wu_ref[...])
    act = (gate * jax.nn.sigmoid(gate) * up).astype(_BF16)
    y = x + gt_ref[0] * _dot(act, wo_ref[...])
    ms = jnp.mean(y * y, axis=-1, keepdims=True)
    o_ref[0] = (y * lax.rsqrt(ms + RMS_EPS)) * gf_ref[...]


def _ffn(x1, mod3, norm2_g, final_g, w_ffn, w_down):
    bsz, n_tok, _ = x1.shape
    tm = FFN_ROWS
    hbm = pl.BlockSpec(memory_space=pl.ANY)
    return pl.pallas_call(
        _ffn_kernel,
        grid=(bsz, n_tok // tm),
        in_specs=[
            pl.BlockSpec((1, tm, D_MODEL), lambda b, t: (b, t, 0)),
            pl.BlockSpec((1, 1, D_MODEL), lambda b, t: (b, 0, 3)),
            pl.BlockSpec((1, 1, D_MODEL), lambda b, t: (b, 0, 4)),
            pl.BlockSpec((1, 1, D_MODEL), lambda b, t: (b, 0, 5)),
            _const_spec((1, D_MODEL)),
            _const_spec((1, D_MODEL)),
            hbm,
            hbm,
        ],
        out_specs=pl.BlockSpec((1, tm, D_MODEL), lambda b, t: (b, t, 0)),
        out_shape=jax.ShapeDtypeStruct(x1.shape, _F32),
        scratch_shapes=[
            pltpu.VMEM((D_MODEL, FFN_HIDDEN), _BF16),
            pltpu.VMEM((D_MODEL, FFN_HIDDEN), _BF16),
            pltpu.VMEM((FFN_HIDDEN, D_MODEL), _BF16),
            pltpu.VMEM((WEIGHT_STAGE_SLOTS, WEIGHT_STAGE_ROWS, 2 * FFN_HIDDEN), _F32),
            pltpu.VMEM((WEIGHT_STAGE_SLOTS, FFN_HIDDEN // SUBLANES, D_MODEL), _F32),
            pltpu.SemaphoreType.DMA((WEIGHT_STAGE_SLOTS,)),
            pltpu.SemaphoreType.DMA((WEIGHT_STAGE_SLOTS,)),
        ],
        compiler_params=pltpu.CompilerParams(
            dimension_semantics=("arbitrary", "arbitrary"), vmem_limit_bytes=VMEM_LIMIT),
        name="swiglu_ffn",
    )(x1, mod3, mod3, mod3, norm2_g, final_g, w_ffn, w_down)


def _cmul(xr, xi, yr, yi):
    return xr * yr - xi * yi, xr * yi + xi * yr


def _split_bf16(v):
    hi = v.astype(_BF16)
    return hi, (v - hi.astype(_F32)).astype(_BF16)


def _dot3_nt(a, b):
    dims = (((1,), (1,)), ((), ()))
    nt = lambda p, q: lax.dot_general(p, q, dims, preferred_element_type=_F32)
    a_hi, a_lo = _split_bf16(a)
    b_hi, b_lo = _split_bf16(b)
    return nt(a_hi, b_hi) + (nt(a_hi, b_lo) + nt(a_lo, b_hi))


def _tables_kernel(a_ref, bt_ref, c_ref, d_ref,
                   mu_ref, wst_ref, mv_ref, a16r_ref, a16i_ref, kt_ref, khl_ref, vnat_ref, cp_ref):
    def pair_body(q, carry):
        for d in range(2):
            ar = a_ref[0, d, q]
            ai = a_ref[1, d, q]
            dt = jnp.exp(a_ref[2, d, q])
            mag = jnp.exp(ar * dt)
            ang = ai * dt
            abr, abi = mag * jnp.cos(ang), mag * jnp.sin(ang)
            den = ar * ar + ai * ai
            fr = ((abr - 1.0) * ar + abi * ai) / den
            fi = (abi * ar - (abr - 1.0) * ai) / den
            bbr, bbi = _cmul(bt_ref[0, d, q], bt_ref[1, d, q], fr, fi)
            cr = c_ref[0, d, q]
            ci = c_ref[1, d, q]
            pw = [(jnp.ones_like(ar), jnp.zeros_like(ar))]
            for _ in range(CHUNK):
                pw.append(_cmul(pw[-1][0], pw[-1][1], abr, abi))
            lanes_re = slice(2 * d * LANE, (2 * d + 1) * LANE)
            lanes_im = slice((2 * d + 1) * LANE, (2 * d + 2) * LANE)
            for lanes in (lanes_re, lanes_im):
                a16r_ref[q, :, lanes] = pw[CHUNK][0]
                a16i_ref[q, :, lanes] = pw[CHUNK][1]
            for sg in range(CHUNK):
                e = (CHUNK - 1 - sg) if d == 0 else sg
                wr, wi = _cmul(bbr, bbi, pw[e][0], pw[e][1])
                e = (sg + 1) if d == 0 else (CHUNK - sg)
                vr, vi = _cmul(cr, ci, pw[e][0], pw[e][1])
                for gg in range(2):
                    src = slice(gg * GROUP_CH, (gg + 1) * GROUP_CH)
                    dst = slice(gg * FLAT + sg * GROUP_CH, gg * FLAT + (sg + 1) * GROUP_CH)
                    wst_ref[q, dst, lanes_re] = wr[src].astype(_BF16)
                    wst_ref[q, dst, lanes_im] = wi[src].astype(_BF16)
                    vnat_ref[2 * d, dst, :] = vr[src]
                    vnat_ref[2 * d + 1, dst, :] = -vi[src]
            for gg in range(2):
                src = slice(gg * GROUP_CH, (gg + 1) * GROUP_CH)
                for k in range(CHUNK):
                    e = k if d == 0 else (CHUNK - 1 - k)
                    pr, pi = _cmul(cr[src], ci[src], pw[e][0], pw[e][1])
                    cp_ref[0, k * GROUP_CH:(k + 1) * GROUP_CH, :] = pr
                    cp_ref[1, k * GROUP_CH:(k + 1) * GROUP_CH, :] = pi
                kt = _dot3_nt(bbr[src], cp_ref[0]) - _dot3_nt(bbi[src], cp_ref[1])
                r0 = pl.multiple_of((2 * q + gg) * GROUP_CH, GROUP_CH)
                kt_ref[d, pl.ds(r0, GROUP_CH), :] = kt
        for part in range(4):
            mv_ref[q, part * LANE:(part + 1) * LANE, :] = vnat_ref[part].T.astype(_BF16)
        return carry

    lax.fori_loop(0, PAIRS, pair_body, 0)

    for d in range(2):
        khl_ref[2 * d], khl_ref[2 * d + 1] = _split_bf16(kt_ref[d])

    row = lax.broadcasted_iota(jnp.int32, (FLAT, FLAT), 0)
    col = lax.broadcasted_iota(jnp.int32, (FLAT, FLAT), 1)
    same_ch = (row % GROUP_CH) == (col % GROUP_CH)
    row_blk = row // GROUP_CH
    col_blk = col // GROUP_CH
    orow = lax.broadcasted_iota(jnp.int32, (GROUPS * GROUP_CH, FLAT), 0)
    ocol = lax.broadcasted_iota(jnp.int32, (GROUPS * GROUP_CH, FLAT), 1)
    skip_ch = (orow % GROUP_CH) == (ocol % GROUP_CH)
    ocol_blk = ocol // GROUP_CH
    d_col = d_ref[...]

    def toeplitz_body(sg, carry):
        sf = jnp.where(same_ch & (row_blk + sg == col_blk), 1.0, 0.0).astype(_BF16)
        sb = jnp.where(same_ch & (row_blk == col_blk + (CHUNK - 1) - sg), 1.0, 0.0).astype(_BF16)
        out = (_dot(khl_ref[0], sf) + _dot(khl_ref[1], sf)) + (_dot(khl_ref[2], sb) + _dot(khl_ref[3], sb))
        out = out + jnp.where(skip_ch & (ocol_blk == sg), d_col, 0.0)
        r0 = pl.multiple_of(sg * GROUP_CH, GROUP_CH)
        mu_ref[:, pl.ds(r0, GROUP_CH), :] = out.reshape(GROUPS, GROUP_CH, FLAT).astype(_BF16)
        return carry

    lax.fori_loop(0, CHUNK, toeplitz_body, 0)


def _s5_tables(a_re, a_im, log_dt, b_re, b_im, c_re, c_im, d_skip):
    f32 = _F32
    eye2 = jnp.eye(2, dtype=f32)

    def pair_blocks(re, im):
        v = jnp.stack([re, im]).astype(f32).reshape(2, 2, PAIRS, 2, GROUP_CH, 1, STATE)
        v = v * eye2[None, None, None, :, None, :, None]
        return v.reshape(2, 2, PAIRS, 2 * GROUP_CH, 2 * STATE)

    ldt = jnp.broadcast_to(log_dt.astype(f32)[..., None], (2, GROUPS, STATE))
    a_rows = jnp.stack([a_re.astype(f32), a_im.astype(f32), ldt]).reshape(3, 2, PAIRS, 1, 2 * STATE)
    args = (a_rows, pair_blocks(jnp.swapaxes(b_re, 2, 3), jnp.swapaxes(b_im, 2, 3)),
            pair_blocks(c_re, c_im), d_skip.astype(f32).reshape(D_SSM, 1))
    whole = lambda a: pl.BlockSpec(a.shape, lambda i, n=a.ndim: (0,) * n)
    out_shape = [
        jax.ShapeDtypeStruct((GROUPS, FLAT, FLAT), _BF16),
        jax.ShapeDtypeStruct((PAIRS, 2 * FLAT, PAIR_LANES), _BF16),
        jax.ShapeDtypeStruct((PAIRS, PAIR_LANES, 2 * FLAT), _BF16),
        jax.ShapeDtypeStruct((PAIRS, 1, PAIR_LANES), _F32),
        jax.ShapeDtypeStruct((PAIRS, 1, PAIR_LANES), _F32),
    ]
    return pl.pallas_call(
        _tables_kernel,
        grid=(1,),
        in_specs=[whole(a) for a in args],
        out_specs=[whole(s) for s in out_shape],
        out_shape=out_shape,
        scratch_shapes=[
            pltpu.VMEM((2, GROUPS * GROUP_CH, FLAT), _F32),
            pltpu.VMEM((4, GROUPS * GROUP_CH, FLAT), _BF16),
            pltpu.VMEM((4, 2 * FLAT, LANE), _F32),
            pltpu.VMEM((2, FLAT, LANE), _F32),
        ],
        compiler_params=pltpu.CompilerParams(
            dimension_semantics=("arbitrary",), vmem_limit_bytes=VMEM_LIMIT),
        name="s5_tables",
    )(*args)


def kernel(x, c, ctx, c_ctx, w_mod, b_mod, norm1_g, norm2_g, w_in, s5_a_re, s5_a_im, s5_log_dt,
           s5_b_re, s5_b_im, s5_c_re, s5_c_im, s5_d, w_glu, b_glu, pool_w, pool_scale,
           w_branch_a, w_branch_b, w_out, w_ffn_in, w_ffn_out, final_norm_g):
    bsz, n_tok, d = x.shape
    ctx_len = ctx.shape[1]
    assert d == D_MODEL and w_mod.shape[0] == 1 and bsz + 1 <= SUBLANES
    assert n_tok % SCAN_ROWS == 0 and n_tok % MIX_ROWS == 0 and MIX_ROWS % GRID_W == 0
    assert ctx_len % CHUNK == 0 and bsz * ctx_len <= SCAN_ROWS
    n_ctx = ctx_len // CHUNK

    cc_t = jnp.concatenate(
        [c.T, c_ctx[:, None], jnp.zeros((D_MODEL, SUBLANES - bsz - 1), _F32)], axis=1)
    mod3 = _modulation(cc_t, w_mod, b_mod, bsz + 1)

    mu, wst, mv, a16_re, a16_im = _s5_tables(
        s5_a_re[0], s5_a_im[0], s5_log_dt[0], s5_b_re[0], s5_b_im[0], s5_c_re[0], s5_c_im[0], s5_d[0])

    uflat, s_loc = _pass1(x, mod3, 0, norm1_g, w_in, wst)
    _, s_ctx = _pass1(ctx.reshape(1, bsz * ctx_len, D_MODEL), mod3, bsz, norm1_g, w_in, wst)
    xstart = _chunk_scan(s_loc, s_ctx, a16_re, a16_im, n_ctx)
    ypre = _readout(uflat, xstart, mu, mv)

    x1 = _mixer(x, ypre, mod3, norm1_g, b_glu, pool_scale, w_in, w_glu, w_branch_a, pool_w, w_branch_b, w_out)
    return _ffn(x1, mod3, norm2_g, final_norm_g.reshape(1, D_MODEL), w_ffn_in, w_ffn_out)
```

```python
import functools

import numpy as np
import jax
import jax.numpy as jnp
from jax import lax
from jax.experimental import pallas as pl
from jax.experimental.pallas import tpu as pltpu

_F32 = jnp.float32
_BF16 = jnp.bfloat16

D_MODEL = 1024
D_SSM = 512
D_POOL = 512
GROUPS = 32
STATE = 64
GROUP_CH = 16
CHUNK = 16
FLAT = CHUNK * GROUP_CH
PAIRS = GROUPS // 2
PAIR_LANES = 4 * 2 * STATE
LANE = 128
SUBLANES = 8
COL_BLOCKS = D_SSM // LANE
GRID_W = 64
POOL_WINDOWS = (2, 4, 8, 16)
POOL_GROUP_CH = D_POOL // len(POOL_WINDOWS)
FFN_HIDDEN = 2816
RMS_EPS = 1e-6

SCAN_TILE = 128
SCAN_ROWS = SCAN_TILE * CHUNK
NORM_ROWS = 512
MIX_ROWS = 1024
FFN_ROWS = 1024
FFN_CHUNKS = ((0, 1536), (1536, FFN_HIDDEN))
SCAN_PAIRS = 4
RELAYOUT_UNROLL = 8
WEIGHT_STAGE_ROWS = 64
WEIGHT_STAGE_SLOTS = 3
VMEM_LIMIT = 56 * 1024 * 1024


def _rms_mod(x, gain, sh):
    ms = jnp.mean(x * x, axis=-1, keepdims=True)
    return (x * lax.rsqrt(ms + RMS_EPS)) * gain + sh


def _dot(a, b):
    return jnp.dot(a, b, preferred_element_type=_F32)


def _const_spec(shape, index=None):
    index = (0,) * len(shape) if index is None else index
    return pl.BlockSpec(shape, lambda *_: index, pipeline_mode=pl.Buffered(1))


def _mod_kernel(ct_ref, w_ref, b_ref, o_ref, *, n_rows):
    ct = ct_ref[...]
    a = ct * jax.nn.sigmoid(ct)
    w = w_ref[...]
    b = b_ref[...]
    for r in range(n_rows):
        o_ref[r] = jnp.sum(w * a[:, r:r + 1], axis=0, keepdims=True) + b
    for r in range(n_rows, o_ref.shape[0]):
        o_ref[r] = jnp.zeros_like(b)


def _modulation(cc_t, w_mod, b_mod, n_rows):
    n_out = w_mod.shape[-1]
    blk = 1024
    return pl.pallas_call(
        functools.partial(_mod_kernel, n_rows=n_rows),
        grid=(n_out // blk,),
        in_specs=[
            pl.BlockSpec((D_MODEL, SUBLANES), lambda i: (0, 0)),
            pl.BlockSpec((None, D_MODEL, blk), lambda i: (0, 0, i)),
            pl.BlockSpec((1, blk), lambda i: (0, i)),
        ],
        out_specs=pl.BlockSpec((SUBLANES, 1, blk), lambda i: (0, 0, i)),
        out_shape=jax.ShapeDtypeStruct((SUBLANES, 1, n_out), _F32),
        compiler_params=pltpu.CompilerParams(
            dimension_semantics=("arbitrary",), vmem_limit_bytes=VMEM_LIMIT),
        name="adaln_mod",
    )(cc_t, w_mod, b_mod)


def _p1_kernel(x_ref, sh_ref, sc_ref, g_ref, wa_ref, wst_ref, uflat_ref, s_ref, h_ref, u_ref, ut_ref,
               *, rows):
    gain = g_ref[...] * (1.0 + sc_ref[0])
    sh = sh_ref[0]
    wa = wa_ref[...].astype(_BF16)

    def norm_block(i):
        r = slice(i * NORM_ROWS, (i + 1) * NORM_ROWS)
        h_ref[r, :] = _rms_mod(x_ref[0, r, :], gain, sh).astype(_BF16)

    norm_block(0)
    for i in range(rows // NORM_ROWS):
        if (i + 1) * NORM_ROWS < rows:
            norm_block(i + 1)
        r = slice(i * NORM_ROWS, (i + 1) * NORM_ROWS)
        u = _dot(h_ref[r, :], wa)
        for cb in range(COL_BLOCKS):
            u_ref[cb, r, :] = u[:, cb * LANE:(cb + 1) * LANE]
    if rows < SCAN_ROWS:
        for cb in range(COL_BLOCKS):
            u_ref[cb, rows:, :] = jnp.zeros((SCAN_ROWS - rows, LANE), _F32)

    def slab_body(sg, carry):
        r0 = pl.multiple_of(sg * GROUP_CH, GROUP_CH)
        for cb in range(COL_BLOCKS):
            slab = u_ref[cb, pl.ds(sg, SCAN_TILE, stride=CHUNK), :]
            ut_ref[cb * 8:(cb + 1) * 8, pl.ds(r0, GROUP_CH), :] = slab.T.reshape(8, GROUP_CH, SCAN_TILE)
        return carry

    lax.fori_loop(0, CHUNK, slab_body, 0, unroll=RELAYOUT_UNROLL)

    def pair_body(q, carry):
        uf0 = ut_ref[2 * q].T.astype(_BF16)
        uf1 = ut_ref[2 * q + 1].T.astype(_BF16)
        uflat_ref[0, 2 * q] = uf0
        uflat_ref[0, 2 * q + 1] = uf1
        s_ref[0, q] = _dot(uf0, wst_ref[q, :FLAT, :]) + _dot(uf1, wst_ref[q, FLAT:, :])
        return carry

    lax.fori_loop(0, PAIRS, pair_body, 0, unroll=RELAYOUT_UNROLL)


def _pass1(x, mod3, mod_row0, norm_g, w_in, wst):
    bsz, n_tok, _ = x.shape
    rows = min(n_tok, SCAN_ROWS)
    assert n_tok % rows == 0 and rows % NORM_ROWS == 0
    nt = n_tok // rows
    n_chunks = nt * SCAN_TILE
    return pl.pallas_call(
        functools.partial(_p1_kernel, rows=rows),
        grid=(bsz, nt),
        in_specs=[
            pl.BlockSpec((1, rows, D_MODEL), lambda b, t: (b, t, 0)),
            pl.BlockSpec((1, 1, D_MODEL), lambda b, t: (b + mod_row0, 0, 0)),
            pl.BlockSpec((1, 1, D_MODEL), lambda b, t: (b + mod_row0, 0, 1)),
            _const_spec((1, D_MODEL)),
            _const_spec((None, D_MODEL, D_SSM)),
            _const_spec((PAIRS, 2 * FLAT, PAIR_LANES)),
        ],
        out_specs=[
            pl.BlockSpec((1, GROUPS, SCAN_TILE, FLAT), lambda b, t: (b, 0, t, 0)),
            pl.BlockSpec((1, PAIRS, SCAN_TILE, PAIR_LANES), lambda b, t: (b, 0, t, 0)),
        ],
        out_shape=[
            jax.ShapeDtypeStruct((bsz, GROUPS, n_chunks, FLAT), _BF16),
            jax.ShapeDtypeStruct((bsz, PAIRS, n_chunks, PAIR_LANES), _F32),
        ],
        scratch_shapes=[
            pltpu.VMEM((SCAN_ROWS, D_MODEL), _BF16),
            pltpu.VMEM((COL_BLOCKS, SCAN_ROWS, LANE), _F32),
            pltpu.VMEM((GROUPS, FLAT, SCAN_TILE), _F32),
        ],
        compiler_params=pltpu.CompilerParams(
            dimension_semantics=("arbitrary", "arbitrary"), vmem_limit_bytes=VMEM_LIMIT),
        name="s5_chunk_states",
    )(x, mod3, mod3, norm_g, w_in, wst)


def _scan_kernel(s_ref, sc_ref, ar_ref, ai_ref, x_ref, xs_ref, c_ref, *, bsz, n_chunks, n_ctx):
    rb = SUBLANES
    rid = lax.broadcasted_iota(jnp.int32, (rb, LANE), 0)
    chains = [(b, qq) for b in range(bsz) for qq in range(SCAN_PAIRS)]

    for qq in range(SCAN_PAIRS):
        for d in range(2):
            o = d * 2 * LANE
            a_r = ar_ref[qq][:, o:o + LANE]
            a_i = ai_ref[qq][:, o:o + LANE]
            pows = [(a_r, a_i)]
            for _ in range(rb - 1):
                pows.append(_cmul(pows[-1][0], pows[-1][1], a_r, a_i))
            idx = qq * 2 + d
            for k, shift in enumerate((1, 2, 4)):
                keep = (rid >= shift) if d == 0 else (rid < rb - shift)
                c_ref[idx, 2 * k] = jnp.where(keep, pows[shift - 1][0], 0.0)
                c_ref[idx, 2 * k + 1] = jnp.where(keep, pows[shift - 1][1], 0.0)
            p_r = jnp.zeros((rb, LANE), _F32)
            p_i = jnp.zeros((rb, LANE), _F32)
            for r in range(rb):
                e = r if d == 0 else rb - 1 - r
                p_r = jnp.where(rid == r, pows[e][0], p_r)
                p_i = jnp.where(rid == r, pows[e][1], p_i)
            c_ref[idx, 6] = p_r
            c_ref[idx, 7] = p_i

    def block(idx, d, s_re, s_im, xin_re, xin_im):
        def shifted(v, k):
            return pltpu.roll(v, k if d == 0 else rb - k, 0)

        t_re, t_im = s_re, s_im
        for k in range(3):
            a_r = c_ref[idx, 2 * k]
            a_i = c_ref[idx, 2 * k + 1]
            u_re, u_im = shifted(t_re, 1 << k), shifted(t_im, 1 << k)
            t_re, t_im = t_re + (a_r * u_re - a_i * u_im), t_im + (a_r * u_im + a_i * u_re)
        p_r = c_ref[idx, 6]
        p_i = c_ref[idx, 7]
        after_re = t_re + (p_r * xin_re - p_i * xin_im)
        after_im = t_im + (p_r * xin_im + p_i * xin_re)
        first = 0 if d == 0 else rb - 1
        last = rb - 1 - first
        start_re = jnp.where(rid == first, xin_re, shifted(after_re, 1))
        start_im = jnp.where(rid == first, xin_im, shifted(after_im, 1))
        return start_re, start_im, after_re[last:last + 1], after_im[last:last + 1]

    def sweep(src_ref, lead, row0, n_blocks, i, carry, dst_ref):
        out = []
        for ci, (b, qq) in enumerate(chains):
            for d in range(2):
                blk = i if d == 0 else n_blocks - 1 - i
                r0 = row0(b) + blk * rb
                if not isinstance(r0, int):
                    r0 = pl.multiple_of(r0, rb)
                o = d * 2 * LANE
                bi = lead(b)
                s_re = src_ref[bi, qq, pl.ds(r0, rb), o:o + LANE]
                s_im = src_ref[bi, qq, pl.ds(r0, rb), o + LANE:o + 2 * LANE]
                xin_re, xin_im = carry[4 * ci + 2 * d], carry[4 * ci + 2 * d + 1]
                st_re, st_im, xo_re, xo_im = block(qq * 2 + d, d, s_re, s_im, xin_re, xin_im)
                if dst_ref is not None:
                    dst_ref[bi, qq, pl.ds(r0, rb), o:o + LANE] = st_re
                    dst_ref[bi, qq, pl.ds(r0, rb), o + LANE:o + 2 * LANE] = st_im
                out += [xo_re, xo_im]
        return tuple(out)

    carry = tuple(jnp.zeros((1, LANE), _F32) for _ in range(4 * len(chains)))
    ctx_blocks = n_ctx // rb
    for i in range(ctx_blocks):
        carry = sweep(sc_ref, lambda b: 0, lambda b: b * n_ctx, ctx_blocks, i, carry, None)

    n_blocks = n_chunks // rb
    lax.fori_loop(
        0, n_blocks,
        lambda i, c: sweep(s_ref, lambda b: b, lambda b: 0, n_blocks, i, c, xs_ref),
        carry)
    x_ref[...] = xs_ref[...].astype(_BF16)


def _chunk_scan(s_loc, s_ctx, a_re, a_im, n_ctx):
    bsz, _, n_chunks, _ = s_loc.shape
    assert n_chunks % SUBLANES == 0 and n_ctx % SUBLANES == 0
    blk = (bsz, SCAN_PAIRS, n_chunks, PAIR_LANES)
    return pl.pallas_call(
        functools.partial(_scan_kernel, bsz=bsz, n_chunks=n_chunks, n_ctx=n_ctx),
        grid=(PAIRS // SCAN_PAIRS,),
        in_specs=[
            pl.BlockSpec(blk, lambda i: (0, i, 0, 0)),
            pl.BlockSpec((1, SCAN_PAIRS, SCAN_TILE, PAIR_LANES), lambda i: (0, i, 0, 0)),
            pl.BlockSpec((SCAN_PAIRS, 1, PAIR_LANES), lambda i: (i, 0, 0)),
            pl.BlockSpec((SCAN_PAIRS, 1, PAIR_LANES), lambda i: (i, 0, 0)),
        ],
        out_specs=pl.BlockSpec(blk, lambda i: (0, i, 0, 0)),
        out_shape=jax.ShapeDtypeStruct(s_loc.shape, _BF16),
        scratch_shapes=[pltpu.VMEM(blk, _F32),
                        pltpu.VMEM((2 * SCAN_PAIRS, 8, SUBLANES, LANE), _F32)],
        compiler_params=pltpu.CompilerParams(
            dimension_semantics=("arbitrary",), vmem_limit_bytes=VMEM_LIMIT),
        name="s5_chunk_scan",
    )(s_loc, s_ctx, a_re, a_im)


def _readout_kernel(uflat_ref, x_ref, mu_ref, mv_ref, y_ref, yt_ref, ys_ref):
    def pair_body(q, carry):
        y0 = _dot(uflat_ref[0, 2 * q], mu_ref[2 * q])
        y1 = _dot(uflat_ref[0, 2 * q + 1], mu_ref[2 * q + 1])
        yx = _dot(x_ref[0, q], mv_ref[q])
        yt = (jnp.concatenate([y0, y1], axis=1) + yx).T
        yt_ref[2 * q] = yt[:FLAT]
        yt_ref[2 * q + 1] = yt[FLAT:]
        return carry

    lax.fori_loop(0, PAIRS, pair_body, 0, unroll=RELAYOUT_UNROLL)

    def slab_body(sg, carry):
        r0 = pl.multiple_of(sg * GROUP_CH, GROUP_CH)
        for cb in range(COL_BLOCKS):
            yt = yt_ref[cb * 8:(cb + 1) * 8, pl.ds(r0, GROUP_CH), :].reshape(LANE, SCAN_TILE)
            ys_ref[cb, pl.ds(sg, SCAN_TILE, stride=CHUNK), :] = yt.T
        return carry

    lax.fori_loop(0, CHUNK, slab_body, 0, unroll=RELAYOUT_UNROLL)
    for cb in range(COL_BLOCKS):
        y_ref[0, :, cb * LANE:(cb + 1) * LANE] = ys_ref[cb].astype(y_ref.dtype)


def _readout(uflat, xstart, mu, mv):
    bsz, _, n_chunks, _ = uflat.shape
    nt = n_chunks // SCAN_TILE
    return pl.pallas_call(
        _readout_kernel,
        grid=(bsz, nt),
        in_specs=[
            pl.BlockSpec((1, GROUPS, SCAN_TILE, FLAT), lambda b, t: (b, 0, t, 0)),
            pl.BlockSpec((1, PAIRS, SCAN_TILE, PAIR_LANES), lambda b, t: (b, 0, t, 0)),
            _const_spec((GROUPS, FLAT, FLAT)),
            _const_spec((PAIRS, PAIR_LANES, 2 * FLAT)),
        ],
        out_specs=pl.BlockSpec((1, SCAN_ROWS, D_SSM), lambda b, t: (b, t, 0)),
        out_shape=jax.ShapeDtypeStruct((bsz, n_chunks * CHUNK, D_SSM), _BF16),
        scratch_shapes=[
            pltpu.VMEM((GROUPS, FLAT, SCAN_TILE), _F32),
            pltpu.VMEM((COL_BLOCKS, SCAN_ROWS, LANE), _F32),
        ],
        compiler_params=pltpu.CompilerParams(
            dimension_semantics=("arbitrary", "arbitrary"), vmem_limit_bytes=VMEM_LIMIT),
        name="s5_readout",
    )(uflat, xstart, mu, mv)


def _window_sum(u, w):
    assert w // 2 <= SUBLANES
    rows, lanes = u.shape
    nb = rows // GRID_W
    pad = jnp.zeros((nb, SUBLANES, lanes), _F32)
    z = jnp.concatenate([pad, u.reshape(nb, GRID_W, lanes), pad], axis=1)
    n = nb * (GRID_W + 2 * SUBLANES)
    z = z.reshape(n, lanes)
    acc = z + pltpu.roll(z, 1, 0)
    m = 2
    while m < w:
        acc = pltpu.roll(acc, m // 2, 0) + pltpu.roll(acc, n - m // 2, 0)
        m *= 2
    return acc.reshape(nb, GRID_W + 2 * SUBLANES, lanes)[:, SUBLANES:SUBLANES + GRID_W, :].reshape(rows, lanes)


def _mix_kernel(x_ref, y_ref, sh_ref, sc_ref, gt_ref, g_ref, bglu_ref, pinv_ref, ps_ref, pw_ref,
                win_hbm, wglu_hbm, wa_hbm, wb_hbm, wo_hbm, o_ref,
                wr_ref, wglu_ref, wa_ref, wb_ref, wo_ref, stage_ref, sem):
    @pl.when(_first_step())
    def _load_weights():
        r = stage_ref.shape[1]
        n_rest = D_POOL + 2 * D_MODEL

        def pieces(src3, dst, n_rows, c0, n_cols):
            return [(src3.at[0, pl.ds(k * r, r), pl.ds(c0, n_cols)],
                     [(dst.at[pl.ds(k * r, r), :], slice(0, n_cols))]) for k in range(n_rows // r)]

        jobs = pieces(win_hbm, wr_ref, D_MODEL, D_SSM, n_rest)
        jobs += pieces(wglu_hbm, wglu_ref, D_SSM, 0, D_SSM)
        jobs += pieces(wa_hbm, wa_ref, D_SSM, 0, D_MODEL)
        jobs += pieces(wb_hbm, wb_ref, D_POOL, 0, D_MODEL)
        jobs += pieces(wo_hbm, wo_ref, D_MODEL, 0, D_MODEL)
        _stream_cast(jobs, stage_ref, sem)

    x = x_ref[0]
    h = _rms_mod(x, g_ref[...] * (1.0 + sc_ref[0]), sh_ref[0]).astype(_BF16)
    c_pool, c_ga, c_gb = 0, D_POOL, D_POOL + D_MODEL
    windows = range(len(POOL_WINDOWS))
    group = lambda wi: slice(wi * POOL_GROUP_CH, (wi + 1) * POOL_GROUP_CH)

    ub = _dot(h, wr_ref[:, c_pool:c_ga])
    y = jax.nn.gelu(y_ref[0].astype(_F32))
    glu = _dot(y.astype(_BF16), wglu_ref[...])
    wsums = [_window_sum(ub[:, group(wi)], POOL_WINDOWS[wi]) for wi in windows]
    gate_a = _dot(h, wr_ref[:, c_ga:c_gb])
    z = y * jax.nn.sigmoid(glu + bglu_ref[...])
    ya = _dot(z.astype(_BF16), wa_ref[...])
    outs = [_dot((wsums[wi] * pinv_ref[wi] - ub[:, group(wi)]).astype(_BF16), pw_ref[wi])
            for wi in windows]
    gate_b = _dot(h, wr_ref[:, c_gb:])
    pb = jnp.concatenate(outs, axis=1) * ps_ref[...]
    yb = _dot(pb.astype(_BF16), wb_ref[...])

    merged = jax.nn.sigmoid(gate_a) * ya + jax.nn.sigmoid(gate_b) * yb
    mixed = _dot(merged.astype(_BF16), wo_ref[...])
    o_ref[0] = x + gt_ref[0] * mixed


def _pool_inverse_counts(rows):
    pos = np.arange(rows) % GRID_W
    invs = []
    for w in POOL_WINDOWS:
        lo = np.clip(pos - w // 2, 0, GRID_W - 1)
        hi = np.clip(pos + w - 1 - w // 2, 0, GRID_W - 1) + 1
        invs.append(np.broadcast_to((1.0 / (hi - lo).astype(np.float32))[:, None], (rows, POOL_GROUP_CH)))
    return np.stack(invs)


def _mixer(x, ypre, mod3, norm_g, b_glu, pool_scale, w_in, w_glu, w_a, pool_w, w_b, w_out):
    bsz, n_tok, _ = x.shape
    tm = MIX_ROWS
    pinv = jnp.asarray(_pool_inverse_counts(tm), _F32)
    nw = len(POOL_WINDOWS)
    n_rest = D_POOL + 2 * D_MODEL
    hbm = pl.BlockSpec(memory_space=pl.ANY)
    return pl.pallas_call(
        _mix_kernel,
        grid=(bsz, n_tok // tm),
        in_specs=[
            pl.BlockSpec((1, tm, D_MODEL), lambda b, t: (b, t, 0)),
            pl.BlockSpec((1, tm, D_SSM), lambda b, t: (b, t, 0)),
            pl.BlockSpec((1, 1, D_MODEL), lambda b, t: (b, 0, 0)),
            pl.BlockSpec((1, 1, D_MODEL), lambda b, t: (b, 0, 1)),
            pl.BlockSpec((1, 1, D_MODEL), lambda b, t: (b, 0, 2)),
            _const_spec((1, D_MODEL)),
            _const_spec((1, D_SSM)),
            _const_spec((nw, tm, POOL_GROUP_CH)),
            _const_spec((1, D_POOL)),
            _const_spec((None, nw, POOL_GROUP_CH, POOL_GROUP_CH)),
            hbm, hbm, hbm, hbm, hbm,
        ],
        out_specs=pl.BlockSpec((1, tm, D_MODEL), lambda b, t: (b, t, 0)),
        out_shape=jax.ShapeDtypeStruct(x.shape, _F32),
        scratch_shapes=[
            pltpu.VMEM((D_MODEL, n_rest), _BF16),
            pltpu.VMEM((D_SSM, D_SSM), _BF16),
            pltpu.VMEM((D_SSM, D_MODEL), _BF16),
            pltpu.VMEM((D_POOL, D_MODEL), _BF16),
            pltpu.VMEM((D_MODEL, D_MODEL), _BF16),
            pltpu.VMEM((WEIGHT_STAGE_SLOTS, 4 * WEIGHT_STAGE_ROWS, n_rest), _F32),
            pltpu.SemaphoreType.DMA((WEIGHT_STAGE_SLOTS,)),
        ],
        compiler_params=pltpu.CompilerParams(
            dimension_semantics=("arbitrary", "arbitrary"), vmem_limit_bytes=VMEM_LIMIT),
        name="token_mixer",
    )(x, ypre, mod3, mod3, mod3, norm_g, b_glu, pinv, pool_scale, pool_w.astype(_BF16),
      w_in, w_glu, w_a, w_b, w_out)


def _stream_cast(jobs, stage_ref, sem):
    n_slots = stage_ref.shape[0]

    def copy(i):
        src = jobs[i][0]
        rows, cols = src.shape
        slot = i % n_slots
        return pltpu.make_async_copy(src, stage_ref.at[slot, pl.ds(0, rows), pl.ds(0, cols)], sem.at[slot])

    for i in range(min(n_slots - 1, len(jobs))):
        copy(i).start()
    for i, (src, dsts) in enumerate(jobs):
        if i + n_slots - 1 < len(jobs):
            copy(i + n_slots - 1).start()
        copy(i).wait()
        rows = src.shape[0]
        for dst, cols in dsts:
            dst[...] = stage_ref[i % n_slots, 0:rows, cols].astype(_BF16)


def _first_step():
    return jnp.logical_and(pl.program_id(0) == 0, pl.program_id(1) == 0)


def _ffn_kernel(x_ref, sh_ref, sc_ref, gt_ref, g2_ref, gf_ref, win_hbm, wout_hbm, o_ref,
                wg_ref, wu_ref, wo_ref, stage_in_ref, stage_out_ref, sem_in, sem_out):
    @pl.when(_first_step())
    def _load_weights():
        r = stage_in_ref.shape[1]
        _stream_cast([(win_hbm.at[0, pl.ds(k * r, r), :],
                       [(wg_ref.at[pl.ds(k * r, r), :], slice(0, FFN_HIDDEN)),
                        (wu_ref.at[pl.ds(k * r, r), :], slice(FFN_HIDDEN, 2 * FFN_HIDDEN))])
                      for k in range(D_MODEL // r)], stage_in_ref, sem_in)
        r = stage_out_ref.shape[1]
        _stream_cast([(wout_hbm.at[0, pl.ds(k * r, r), :], [(wo_ref.at[pl.ds(k * r, r), :], slice(0, D_MODEL))])
                      for k in range(FFN_HIDDEN // r)], stage_out_ref, sem_out)

    x = x_ref[0]
    h = _rms_mod(x, g2_ref[...] * (1.0 + sc_ref[0]), sh_ref[0]).astype(_BF16)
    down = None
    for c0, c1 in FFN_CHUNKS:
        gate = _dot(h, wg_ref[:, c0:c1])
        up = _dot(h, wu_ref[:, c0:c1])
        part = _dot((gate * jax.nn.sigmoid(gate) * up).astype(_BF16), wo_ref[c0:c1, :])
        down = part if down is None else down + part
    y = x + gt_ref[0] * down
    ms = jnp.mean(y * y, axis=-1, keepdims=True)
    o_ref[0] = (y * lax.rsqrt(ms + RMS_EPS)) * gf_ref[...]


def _ffn(x1, mod3, norm2_g, final_g, w_ffn, w_down):
    bsz, n_tok, _ = x1.shape
    tm = FFN_ROWS
    hbm = pl.BlockSpec(memory_space=pl.ANY)
    return pl.pallas_call(
        _ffn_kernel,
        grid=(bsz, n_tok // tm),
        in_specs=[
            pl.BlockSpec((1, tm, D_MODEL), lambda b, t: (b, t, 0)),
            pl.BlockSpec((1, 1, D_MODEL), lambda b, t: (b, 0, 3)),
            pl.BlockSpec((1, 1, D_MODEL), lambda b, t: (b, 0, 4)),
            pl.BlockSpec((1, 1, D_MODEL), lambda b, t: (b, 0, 5)),
            _const_spec((1, D_MODEL)),
            _const_spec((1, D_MODEL)),
            hbm,
            hbm,
        ],
        out_specs=pl.BlockSpec((1, tm, D_MODEL), lambda b, t: (b, t, 0)),
        out_shape=jax.ShapeDtypeStruct(x1.shape, _F32),
        scratch_shapes=[
            pltpu.VMEM((D_MODEL, FFN_HIDDEN), _BF16),
            pltpu.VMEM((D_MODEL, FFN_HIDDEN), _BF16),
            pltpu.VMEM((FFN_HIDDEN, D_MODEL), _BF16),
            pltpu.VMEM((WEIGHT_STAGE_SLOTS, WEIGHT_STAGE_ROWS, 2 * FFN_HIDDEN), _F32),
            pltpu.VMEM((WEIGHT_STAGE_SLOTS, FFN_HIDDEN // SUBLANES, D_MODEL), _F32),
            pltpu.SemaphoreType.DMA((WEIGHT_STAGE_SLOTS,)),
            pltpu.SemaphoreType.DMA((WEIGHT_STAGE_SLOTS,)),
        ],
        compiler_params=pltpu.CompilerParams(
            dimension_semantics=("arbitrary", "arbitrary"), vmem_limit_bytes=VMEM_LIMIT),
        name="swiglu_ffn",
    )(x1, mod3, mod3, mod3, norm2_g, final_g, w_ffn, w_down)


def _cmul(xr, xi, yr, yi):
    return xr * yr - xi * yi, xr * yi + xi * yr


def _split_bf16(v):
    hi = v.astype(_BF16)
    return hi, (v - hi.astype(_F32)).astype(_BF16)


def _dot3_nt(a, b):
    dims = (((1,), (1,)), ((), ()))
    nt = lambda p, q: lax.dot_general(p, q, dims, preferred_element_type=_F32)
    a_hi, a_lo = _split_bf16(a)
    b_hi, b_lo = _split_bf16(b)
    return nt(a_hi, b_hi) + (nt(a_hi, b_lo) + nt(a_lo, b_hi))


def _tables_kernel(a_ref, bt_ref, c_ref, d_ref,
                   mu_ref, wst_ref, mv_ref, a16r_ref, a16i_ref, kt_ref, khl_ref, vnat_ref, cp_ref):
    def pair_body(q, carry):
        for d in range(2):
            ar = a_ref[0, d, q]
            ai = a_ref[1, d, q]
            dt = jnp.exp(a_ref[2, d, q])
            mag = jnp.exp(ar * dt)
            ang = ai * dt
            abr, abi = mag * jnp.cos(ang), mag * jnp.sin(ang)
            den = ar * ar + ai * ai
            fr = ((abr - 1.0) * ar + abi * ai) / den
            fi = (abi * ar - (abr - 1.0) * ai) / den
            bbr, bbi = _cmul(bt_ref[0, d, q], bt_ref[1, d, q], fr, fi)
            cr = c_ref[0, d, q]
            ci = c_ref[1, d, q]
            pw = [(jnp.ones_like(ar), jnp.zeros_like(ar))]
            for _ in range(CHUNK):
                pw.append(_cmul(pw[-1][0], pw[-1][1], abr, abi))
            lanes_re = slice(2 * d * LANE, (2 * d + 1) * LANE)
            lanes_im = slice((2 * d + 1) * LANE, (2 * d + 2) * LANE)
            for lanes in (lanes_re, lanes_im):
                a16r_ref[q, :, lanes] = pw[CHUNK][0]
                a16i_ref[q, :, lanes] = pw[CHUNK][1]
            for sg in range(CHUNK):
                e = (CHUNK - 1 - sg) if d == 0 else sg
                wr, wi = _cmul(bbr, bbi, pw[e][0], pw[e][1])
                e = (sg + 1) if d == 0 else (CHUNK - sg)
                vr, vi = _cmul(cr, ci, pw[e][0], pw[e][1])
                for gg in range(2):
                    src = slice(gg * GROUP_CH, (gg + 1) * GROUP_CH)
                    dst = slice(gg * FLAT + sg * GROUP_CH, gg * FLAT + (sg + 1) * GROUP_CH)
                    wst_ref[q, dst, lanes_re] = wr[src].astype(_BF16)
                    wst_ref[q, dst, lanes_im] = wi[src].astype(_BF16)
                    vnat_ref[2 * d, dst, :] = vr[src]
                    vnat_ref[2 * d + 1, dst, :] = -vi[src]
            for gg in range(2):
                src = slice(gg * GROUP_CH, (gg + 1) * GROUP_CH)
                for k in range(CHUNK):
                    e = k if d == 0 else (CHUNK - 1 - k)
                    pr, pi = _cmul(cr[src], ci[src], pw[e][0], pw[e][1])
                    cp_ref[0, k * GROUP_CH:(k + 1) * GROUP_CH, :] = pr
                    cp_ref[1, k * GROUP_CH:(k + 1) * GROUP_CH, :] = pi
                kt = _dot3_nt(bbr[src], cp_ref[0]) - _dot3_nt(bbi[src], cp_ref[1])
                r0 = pl.multiple_of((2 * q + gg) * GROUP_CH, GROUP_CH)
                kt_ref[d, pl.ds(r0, GROUP_CH), :] = kt
        for part in range(4):
            mv_ref[q, part * LANE:(part + 1) * LANE, :] = vnat_ref[part].T.astype(_BF16)
        return carry

    lax.fori_loop(0, PAIRS, pair_body, 0)

    for d in range(2):
        khl_ref[2 * d], khl_ref[2 * d + 1] = _split_bf16(kt_ref[d])

    row = lax.broadcasted_iota(jnp.int32, (FLAT, FLAT), 0)
    col = lax.broadcasted_iota(jnp.int32, (FLAT, FLAT), 1)
    same_ch = (row % GROUP_CH) == (col % GROUP_CH)
    row_blk = row // GROUP_CH
    col_blk = col // GROUP_CH
    orow = lax.broadcasted_iota(jnp.int32, (GROUPS * GROUP_CH, FLAT), 0)
    ocol = lax.broadcasted_iota(jnp.int32, (GROUPS * GROUP_CH, FLAT), 1)
    skip_ch = (orow % GROUP_CH) == (ocol % GROUP_CH)
    ocol_blk = ocol // GROUP_CH
    d_col = d_ref[...]

    def toeplitz_body(sg, carry):
        sf = jnp.where(same_ch & (row_blk + sg == col_blk), 1.0, 0.0).astype(_BF16)
        sb = jnp.where(same_ch & (row_blk == col_blk + (CHUNK - 1) - sg), 1.0, 0.0).astype(_BF16)
        out = (_dot(khl_ref[0], sf) + _dot(khl_ref[1], sf)) + (_dot(khl_ref[2], sb) + _dot(khl_ref[3], sb))
        out = out + jnp.where(skip_ch & (ocol_blk == sg), d_col, 0.0)
        r0 = pl.multiple_of(sg * GROUP_CH, GROUP_CH)
        mu_ref[:, pl.ds(r0, GROUP_CH), :] = out.reshape(GROUPS, GROUP_CH, FLAT).astype(_BF16)
        return carry

    lax.fori_loop(0, CHUNK, toeplitz_body, 0)


def _s5_tables(a_re, a_im, log_dt, b_re, b_im, c_re, c_im, d_skip):
    f32 = _F32
    eye2 = jnp.eye(2, dtype=f32)

    def pair_blocks(re, im):
        v = jnp.stack([re, im]).astype(f32).reshape(2, 2, PAIRS, 2, GROUP_CH, 1, STATE)
        v = v * eye2[None, None, None, :, None, :, None]
        return v.reshape(2, 2, PAIRS, 2 * GROUP_CH, 2 * STATE)

    ldt = jnp.broadcast_to(log_dt.astype(f32)[..., None], (2, GROUPS, STATE))
    a_rows = jnp.stack([a_re.astype(f32), a_im.astype(f32), ldt]).reshape(3, 2, PAIRS, 1, 2 * STATE)
    args = (a_rows, pair_blocks(jnp.swapaxes(b_re, 2, 3), jnp.swapaxes(b_im, 2, 3)),
            pair_blocks(c_re, c_im), d_skip.astype(f32).reshape(D_SSM, 1))
    whole = lambda a: pl.BlockSpec(a.shape, lambda i, n=a.ndim: (0,) * n)
    out_shape = [
        jax.ShapeDtypeStruct((GROUPS, FLAT, FLAT), _BF16),
        jax.ShapeDtypeStruct((PAIRS, 2 * FLAT, PAIR_LANES), _BF16),
        jax.ShapeDtypeStruct((PAIRS, PAIR_LANES, 2 * FLAT), _BF16),
        jax.ShapeDtypeStruct((PAIRS, 1, PAIR_LANES), _F32),
        jax.ShapeDtypeStruct((PAIRS, 1, PAIR_LANES), _F32),
    ]
    return pl.pallas_call(
        _tables_kernel,
        grid=(1,),
        in_specs=[whole(a) for a in args],
        out_specs=[whole(s) for s in out_shape],
        out_shape=out_shape,
        scratch_shapes=[
            pltpu.VMEM((2, GROUPS * GROUP_CH, FLAT), _F32),
            pltpu.VMEM((4, GROUPS * GROUP_CH, FLAT), _BF16),
            pltpu.VMEM((4, 2 * FLAT, LANE), _F32),
            pltpu.VMEM((2, FLAT, LANE), _F32),
        ],
        compiler_params=pltpu.CompilerParams(
            dimension_semantics=("arbitrary",), vmem_limit_bytes=VMEM_LIMIT),
        name="s5_tables",
    )(*args)


def kernel(x, c, ctx, c_ctx, w_mod, b_mod, norm1_g, norm2_g, w_in, s5_a_re, s5_a_im, s5_log_dt,
           s5_b_re, s5_b_im, s5_c_re, s5_c_im, s5_d, w_glu, b_glu, pool_w, pool_scale,
           w_branch_a, w_branch_b, w_out, w_ffn_in, w_ffn_out, final_norm_g):
    bsz, n_tok, d = x.shape
    ctx_len = ctx.shape[1]
    assert d == D_MODEL and w_mod.shape[0] == 1 and bsz + 1 <= SUBLANES
    assert n_tok % SCAN_ROWS == 0 and n_tok % MIX_ROWS == 0 and MIX_ROWS % GRID_W == 0
    assert ctx_len % CHUNK == 0 and bsz * ctx_len <= SCAN_ROWS
    n_ctx = ctx_len // CHUNK

    cc_t = jnp.concatenate(
        [c.T, c_ctx[:, None], jnp.zeros((D_MODEL, SUBLANES - bsz - 1), _F32)], axis=1)
    mod3 = _modulation(cc_t, w_mod, b_mod, bsz + 1)

    mu, wst, mv, a16_re, a16_im = _s5_tables(
        s5_a_re[0], s5_a_im[0], s5_log_dt[0], s5_b_re[0], s5_b_im[0], s5_c_re[0], s5_c_im[0], s5_d[0])

    uflat, s_loc = _pass1(x, mod3, 0, norm1_g, w_in, wst)
    _, s_ctx = _pass1(ctx.reshape(1, bsz * ctx_len, D_MODEL), mod3, bsz, norm1_g, w_in, wst)
    xstart = _chunk_scan(s_loc, s_ctx, a16_re, a16_im, n_ctx)
    ypre = _readout(uflat, xstart, mu, mv)

    x1 = _mixer(x, ypre, mod3, norm1_g, b_glu, pool_scale, w_in, w_glu, w_branch_a, pool_w, w_branch_b, w_out)
    return _ffn(x1, mod3, norm2_g, final_norm_g.reshape(1, D_MODEL), w_ffn_in, w_ffn_out)
```

```python
import functools

import numpy as np
import jax
import jax.numpy as jnp
from jax import lax
from jax.experimental import pallas as pl
from jax.experimental.pallas import tpu as pltpu

_F32 = jnp.float32
_BF16 = jnp.bfloat16

D_MODEL = 1024
D_SSM = 512
D_POOL = 512
GROUPS = 32
STATE = 64
GROUP_CH = 16
CHUNK = 16
FLAT = CHUNK * GROUP_CH
PAIRS = GROUPS // 2
PAIR_LANES = 4 * 2 * STATE
LANE = 128
SUBLANES = 8
CHUNK_PITCH = 24
COL_BLOCKS = D_SSM // LANE
GRID_W = 64
POOL_WINDOWS = (2, 4, 8, 16)
POOL_GROUP_CH = D_POOL // len(POOL_WINDOWS)
FFN_HIDDEN = 2816
RMS_EPS = 1e-6

SCAN_TILE = 128
SCAN_ROWS = SCAN_TILE * CHUNK
NORM_ROWS = 512
MIX_ROWS = 1024
FFN_ROWS = 1024
FFN_CHUNKS = ((0, 1536), (1536, FFN_HIDDEN))
SCAN_PAIRS = 4
RELAYOUT_UNROLL = 8
WEIGHT_STAGE_ROWS = 64
WEIGHT_STAGE_SLOTS = 3
VMEM_LIMIT = 56 * 1024 * 1024


def _rms_mod(x, gain, sh):
    ms = jnp.mean(x * x, axis=-1, keepdims=True)
    return (x * lax.rsqrt(ms + RMS_EPS)) * gain + sh


def _dot(a, b):
    return jnp.dot(a, b, preferred_element_type=_F32)


def _const_spec(shape, index=None):
    index = (0,) * len(shape) if index is None else index
    return pl.BlockSpec(shape, lambda *_: index, pipeline_mode=pl.Buffered(1))


def _mod_kernel(ct_ref, w_ref, b_ref, o_ref, *, n_rows):
    ct = ct_ref[...]
    a = ct * jax.nn.sigmoid(ct)
    w = w_ref[...]
    b = b_ref[...]
    for r in range(n_rows):
        o_ref[r] = jnp.sum(w * a[:, r:r + 1], axis=0, keepdims=True) + b
    for r in range(n_rows, o_ref.shape[0]):
        o_ref[r] = jnp.zeros_like(b)


def _modulation(cc_t, w_mod, b_mod, n_rows):
    n_out = w_mod.shape[-1]
    blk = 1024
    return pl.pallas_call(
        functools.partial(_mod_kernel, n_rows=n_rows),
        grid=(n_out // blk,),
        in_specs=[
            pl.BlockSpec((D_MODEL, SUBLANES), lambda i: (0, 0)),
            pl.BlockSpec((None, D_MODEL, blk), lambda i: (0, 0, i)),
            pl.BlockSpec((1, blk), lambda i: (0, i)),
        ],
        out_specs=pl.BlockSpec((SUBLANES, 1, blk), lambda i: (0, 0, i)),
        out_shape=jax.ShapeDtypeStruct((SUBLANES, 1, n_out), _F32),
        compiler_params=pltpu.CompilerParams(
            dimension_semantics=("arbitrary",), vmem_limit_bytes=VMEM_LIMIT),
        name="adaln_mod",
    )(cc_t, w_mod, b_mod)


def _to_pitch(v):
    n = v.shape[0] // CHUNK
    v = v.reshape(n, CHUNK, v.shape[1])
    pad = jnp.zeros((n, CHUNK_PITCH - CHUNK, v.shape[2]), v.dtype)
    return jnp.concatenate([v, pad], axis=1).reshape(n * CHUNK_PITCH, v.shape[2])


def _from_pitch(v):
    n = v.shape[0] // CHUNK_PITCH
    return v.reshape(n, CHUNK_PITCH, v.shape[1])[:, :CHUNK, :].reshape(n * CHUNK, v.shape[1])


def _p1_kernel(x_ref, sh_ref, sc_ref, g_ref, wa_ref, wst_ref, uflat_ref, s_ref, h_ref, u_ref, ut_ref,
               *, rows):
    gain = g_ref[...] * (1.0 + sc_ref[0])
    sh = sh_ref[0]
    wa = wa_ref[...].astype(_BF16)
    block_chunks = NORM_ROWS // CHUNK

    def norm_block(i):
        r = slice(i * NORM_ROWS, (i + 1) * NORM_ROWS)
        h_ref[r, :] = _rms_mod(x_ref[0, r, :], gain, sh).astype(_BF16)

    norm_block(0)
    for i in range(rows // NORM_ROWS):
        if (i + 1) * NORM_ROWS < rows:
            norm_block(i + 1)
        u = _dot(h_ref[i * NORM_ROWS:(i + 1) * NORM_ROWS, :], wa)
        pr = slice(i * block_chunks * CHUNK_PITCH, (i + 1) * block_chunks * CHUNK_PITCH)
        for cb in range(COL_BLOCKS):
            u_ref[cb, pr, :] = _to_pitch(u[:, cb * LANE:(cb + 1) * LANE])
    if rows < SCAN_ROWS:
        first = rows // CHUNK * CHUNK_PITCH
        for cb in range(COL_BLOCKS):
            u_ref[cb, first:, :] = jnp.zeros((SCAN_TILE * CHUNK_PITCH - first, LANE), _F32)

    def slab_body(sg, carry):
        r0 = pl.multiple_of(sg * GROUP_CH, GROUP_CH)
        for cb in range(COL_BLOCKS):
            slab = u_ref[cb, pl.ds(sg, SCAN_TILE, stride=CHUNK_PITCH), :]
            ut_ref[cb * 8:(cb + 1) * 8, pl.ds(r0, GROUP_CH), :] = (
                slab.astype(_BF16).T.reshape(8, GROUP_CH, SCAN_TILE))
        return carry

    lax.fori_loop(0, CHUNK, slab_body, 0, unroll=RELAYOUT_UNROLL)

    def pair_body(q, carry):
        uf0 = ut_ref[2 * q].T
        uf1 = ut_ref[2 * q + 1].T
        uflat_ref[0, 2 * q] = uf0
        uflat_ref[0, 2 * q + 1] = uf1
        s_ref[0, q] = _dot(uf0, wst_ref[q, :FLAT, :]) + _dot(uf1, wst_ref[q, FLAT:, :])
        return carry

    lax.fori_loop(0, PAIRS, pair_body, 0, unroll=RELAYOUT_UNROLL)


def _pass1(x, mod3, mod_row0, norm_g, w_in, wst):
    bsz, n_tok, _ = x.shape
    rows = min(n_tok, SCAN_ROWS)
    assert n_tok % rows == 0 and rows % NORM_ROWS == 0
    nt = n_tok // rows
    n_chunks = nt * SCAN_TILE
    return pl.pallas_call(
        functools.partial(_p1_kernel, rows=rows),
        grid=(bsz, nt),
        in_specs=[
            pl.BlockSpec((1, rows, D_MODEL), lambda b, t: (b, t, 0)),
            pl.BlockSpec((1, 1, D_MODEL), lambda b, t: (b + mod_row0, 0, 0)),
            pl.BlockSpec((1, 1, D_MODEL), lambda b, t: (b + mod_row0, 0, 1)),
            _const_spec((1, D_MODEL)),
            _const_spec((None, D_MODEL, D_SSM)),
            _const_spec((PAIRS, 2 * FLAT, PAIR_LANES)),
        ],
        out_specs=[
            pl.BlockSpec((1, GROUPS, SCAN_TILE, FLAT), lambda b, t: (b, 0, t, 0)),
            pl.BlockSpec((1, PAIRS, SCAN_TILE, PAIR_LANES), lambda b, t: (b, 0, t, 0)),
        ],
        out_shape=[
            jax.ShapeDtypeStruct((bsz, GROUPS, n_chunks, FLAT), _BF16),
            jax.ShapeDtypeStruct((bsz, PAIRS, n_chunks, PAIR_LANES), _F32),
        ],
        scratch_shapes=[
            pltpu.VMEM((SCAN_ROWS, D_MODEL), _BF16),
            pltpu.VMEM((COL_BLOCKS, SCAN_TILE * CHUNK_PITCH, LANE), _F32),
            pltpu.VMEM((GROUPS, FLAT, SCAN_TILE), _BF16),
        ],
        compiler_params=pltpu.CompilerParams(
            dimension_semantics=("arbitrary", "arbitrary"), vmem_limit_bytes=VMEM_LIMIT),
        name="s5_chunk_states",
    )(x, mod3, mod3, norm_g, w_in, wst)


def _scan_kernel(s_ref, sc_ref, ar_ref, ai_ref, x_ref, xs_ref, c_ref, *, bsz, n_chunks, n_ctx):
    rb = SUBLANES
    rid = lax.broadcasted_iota(jnp.int32, (rb, LANE), 0)
    chains = [(b, qq) for b in range(bsz) for qq in range(SCAN_PAIRS)]

    for qq in range(SCAN_PAIRS):
        for d in range(2):
            o = d * 2 * LANE
            a_r = ar_ref[qq][:, o:o + LANE]
            a_i = ai_ref[qq][:, o:o + LANE]
            pows = [(a_r, a_i)]
            for _ in range(rb - 1):
                pows.append(_cmul(pows[-1][0], pows[-1][1], a_r, a_i))
            idx = qq * 2 + d
            for k, shift in enumerate((1, 2, 4)):
                keep = (rid >= shift) if d == 0 else (rid < rb - shift)
                c_ref[idx, 2 * k] = jnp.where(keep, pows[shift - 1][0], 0.0)
                c_ref[idx, 2 * k + 1] = jnp.where(keep, pows[shift - 1][1], 0.0)
            p_r = jnp.zeros((rb, LANE), _F32)
            p_i = jnp.zeros((rb, LANE), _F32)
            for r in range(rb):
                e = r if d == 0 else rb - 1 - r
                p_r = jnp.where(rid == r, pows[e][0], p_r)
                p_i = jnp.where(rid == r, pows[e][1], p_i)
            c_ref[idx, 6] = p_r
            c_ref[idx, 7] = p_i

    def block(idx, d, s_re, s_im, xin_re, xin_im):
        def shifted(v, k):
            return pltpu.roll(v, k if d == 0 else rb - k, 0)

        t_re, t_im = s_re, s_im
        for k in range(3):
            a_r = c_ref[idx, 2 * k]
            a_i = c_ref[idx, 2 * k + 1]
            u_re, u_im = shifted(t_re, 1 << k), shifted(t_im, 1 << k)
            t_re, t_im = t_re + (a_r * u_re - a_i * u_im), t_im + (a_r * u_im + a_i * u_re)
        p_r = c_ref[idx, 6]
        p_i = c_ref[idx, 7]
        after_re = t_re + (p_r * xin_re - p_i * xin_im)
        after_im = t_im + (p_r * xin_im + p_i * xin_re)
        first = 0 if d == 0 else rb - 1
        last = rb - 1 - first
        start_re = jnp.where(rid == first, xin_re, shifted(after_re, 1))
        start_im = jnp.where(rid == first, xin_im, shifted(after_im, 1))
        return start_re, start_im, after_re[last:last + 1], after_im[last:last + 1]

    def sweep(src_ref, lead, row0, n_blocks, i, carry, dst_ref):
        out = []
        for ci, (b, qq) in enumerate(chains):
            for d in range(2):
                blk = i if d == 0 else n_blocks - 1 - i
                r0 = row0(b) + blk * rb
                if not isinstance(r0, int):
                    r0 = pl.multiple_of(r0, rb)
                o = d * 2 * LANE
                bi = lead(b)
                s_re = src_ref[bi, qq, pl.ds(r0, rb), o:o + LANE]
                s_im = src_ref[bi, qq, pl.ds(r0, rb), o + LANE:o + 2 * LANE]
                xin_re, xin_im = carry[4 * ci + 2 * d], carry[4 * ci + 2 * d + 1]
                st_re, st_im, xo_re, xo_im = block(qq * 2 + d, d, s_re, s_im, xin_re, xin_im)
                if dst_ref is not None:
                    dst_ref[bi, qq, pl.ds(r0, rb), o:o + LANE] = st_re
                    dst_ref[bi, qq, pl.ds(r0, rb), o + LANE:o + 2 * LANE] = st_im
                out += [xo_re, xo_im]
        return tuple(out)

    carry = tuple(jnp.zeros((1, LANE), _F32) for _ in range(4 * len(chains)))
    ctx_blocks = n_ctx // rb
    for i in range(ctx_blocks):
        carry = sweep(sc_ref, lambda b: 0, lambda b: b * n_ctx, ctx_blocks, i, carry, None)

    n_blocks = n_chunks // rb
    lax.fori_loop(
        0, n_blocks,
        lambda i, c: sweep(s_ref, lambda b: b, lambda b: 0, n_blocks, i, c, xs_ref),
        carry)
    x_ref[...] = xs_ref[...].astype(_BF16)


def _chunk_scan(s_loc, s_ctx, a_re, a_im, n_ctx):
    bsz, _, n_chunks, _ = s_loc.shape
    assert n_chunks % SUBLANES == 0 and n_ctx % SUBLANES == 0
    blk = (bsz, SCAN_PAIRS, n_chunks, PAIR_LANES)
    return pl.pallas_call(
        functools.partial(_scan_kernel, bsz=bsz, n_chunks=n_chunks, n_ctx=n_ctx),
        grid=(PAIRS // SCAN_PAIRS,),
        in_specs=[
            pl.BlockSpec(blk, lambda i: (0, i, 0, 0)),
            pl.BlockSpec((1, SCAN_PAIRS, SCAN_TILE, PAIR_LANES), lambda i: (0, i, 0, 0)),
            pl.BlockSpec((SCAN_PAIRS, 1, PAIR_LANES), lambda i: (i, 0, 0)),
            pl.BlockSpec((SCAN_PAIRS, 1, PAIR_LANES), lambda i: (i, 0, 0)),
        ],
        out_specs=pl.BlockSpec(blk, lambda i: (0, i, 0, 0)),
        out_shape=jax.ShapeDtypeStruct(s_loc.shape, _BF16),
        scratch_shapes=[pltpu.VMEM(blk, _F32),
                        pltpu.VMEM((2 * SCAN_PAIRS, 8, SUBLANES, LANE), _F32)],
        compiler_params=pltpu.CompilerParams(
            dimension_semantics=("arbitrary",), vmem_limit_bytes=VMEM_LIMIT),
        name="s5_chunk_scan",
    )(s_loc, s_ctx, a_re, a_im)


def _readout_kernel(uflat_ref, x_ref, mu_ref, mv_ref, y_ref, yt_ref, ys_ref):
    def pair_body(q, carry):
        y0 = _dot(uflat_ref[0, 2 * q], mu_ref[2 * q])
        y1 = _dot(uflat_ref[0, 2 * q + 1], mu_ref[2 * q + 1])
        yx = _dot(x_ref[0, q], mv_ref[q])
        y = jnp.concatenate([y0, y1], axis=1) + yx
        yt = y.astype(yt_ref.dtype).T
        yt_ref[2 * q] = yt[:FLAT]
        yt_ref[2 * q + 1] = yt[FLAT:]
        return carry

    lax.fori_loop(0, PAIRS, pair_body, 0, unroll=RELAYOUT_UNROLL)

    def slab_body(sg, carry):
        r0 = pl.multiple_of(sg * GROUP_CH, GROUP_CH)
        for cb in range(COL_BLOCKS):
            yt = yt_ref[cb * 8:(cb + 1) * 8, pl.ds(r0, GROUP_CH), :].reshape(LANE, SCAN_TILE)
            ys_ref[cb, pl.ds(sg, SCAN_TILE, stride=CHUNK_PITCH), :] = yt.T.astype(_F32)
        return carry

    lax.fori_loop(0, CHUNK, slab_body, 0, unroll=RELAYOUT_UNROLL)
    for cb in range(COL_BLOCKS):
        y_ref[0, :, cb * LANE:(cb + 1) * LANE] = _from_pitch(ys_ref[cb]).astype(y_ref.dtype)


def _readout(uflat, xstart, mu, mv):
    bsz, _, n_chunks, _ = uflat.shape
    nt = n_chunks // SCAN_TILE
    return pl.pallas_call(
        _readout_kernel,
        grid=(bsz, nt),
        in_specs=[
            pl.BlockSpec((1, GROUPS, SCAN_TILE, FLAT), lambda b, t: (b, 0, t, 0)),
            pl.BlockSpec((1, PAIRS, SCAN_TILE, PAIR_LANES), lambda b, t: (b, 0, t, 0)),
            _const_spec((GROUPS, FLAT, FLAT)),
            _const_spec((PAIRS, PAIR_LANES, 2 * FLAT)),
        ],
        out_specs=pl.BlockSpec((1, SCAN_ROWS, D_SSM), lambda b, t: (b, t, 0)),
        out_shape=jax.ShapeDtypeStruct((bsz, n_chunks * CHUNK, D_SSM), _BF16),
        scratch_shapes=[
            pltpu.VMEM((GROUPS, FLAT, SCAN_TILE), _BF16),
            pltpu.VMEM((COL_BLOCKS, SCAN_TILE * CHUNK_PITCH, LANE), _F32),
        ],
        compiler_params=pltpu.CompilerParams(
            dimension_semantics=("arbitrary", "arbitrary"), vmem_limit_bytes=VMEM_LIMIT),
        name="s5_readout",
    )(uflat, xstart, mu, mv)


def _window_sum(u, w):
    assert w // 2 <= SUBLANES
    rows, lanes = u.shape
    nb = rows // GRID_W
    pad = jnp.zeros((nb, SUBLANES, lanes), _F32)
    z = jnp.concatenate([pad, u.reshape(nb, GRID_W, lanes), pad], axis=1)
    n = nb * (GRID_W + 2 * SUBLANES)
    z = z.reshape(n, lanes)
    acc = z + pltpu.roll(z, 1, 0)
    m = 2
    while m < w:
        acc = pltpu.roll(acc, m // 2, 0) + pltpu.roll(acc, n - m // 2, 0)
        m *= 2
    return acc.reshape(nb, GRID_W + 2 * SUBLANES, lanes)[:, SUBLANES:SUBLANES + GRID_W, :].reshape(rows, lanes)


def _mix_kernel(x_ref, y_ref, sh_ref, sc_ref, gt_ref, g_ref, bglu_ref, pinv_ref, ps_ref, pw_ref,
                win_hbm, wglu_hbm, wa_hbm, wb_hbm, wo_hbm, o_ref,
                wr_ref, wglu_ref, wa_ref, wb_ref, wo_ref, stage_ref, sem):
    @pl.when(_first_step())
    def _load_weights():
        r = stage_ref.shape[1]
        n_rest = D_POOL + 2 * D_MODEL

        def pieces(src3, dst, n_rows, c0, n_cols):
            return [(src3.at[0, pl.ds(k * r, r), pl.ds(c0, n_cols)],
                     [(dst.at[pl.ds(k * r, r), :], slice(0, n_cols))]) for k in range(n_rows // r)]

        jobs = pieces(win_hbm, wr_ref, D_MODEL, D_SSM, n_rest)
        jobs += pieces(wglu_hbm, wglu_ref, D_SSM, 0, D_SSM)
        jobs += pieces(wa_hbm, wa_ref, D_SSM, 0, D_MODEL)
        jobs += pieces(wb_hbm, wb_ref, D_POOL, 0, D_MODEL)
        jobs += pieces(wo_hbm, wo_ref, D_MODEL, 0, D_MODEL)
        _stream_cast(jobs, stage_ref, sem)

    x = x_ref[0]
    h = _rms_mod(x, g_ref[...] * (1.0 + sc_ref[0]), sh_ref[0]).astype(_BF16)
    c_pool, c_ga, c_gb = 0, D_POOL, D_POOL + D_MODEL
    windows = range(len(POOL_WINDOWS))
    group = lambda wi: slice(wi * POOL_GROUP_CH, (wi + 1) * POOL_GROUP_CH)

    ub = _dot(h, wr_ref[:, c_pool:c_ga])
    y = jax.nn.gelu(y_ref[0].astype(_F32))
    glu = _dot(y.astype(_BF16), wglu_ref[...])
    wsums = [_window_sum(ub[:, group(wi)], POOL_WINDOWS[wi]) for wi in windows]
    gate_a = _dot(h, wr_ref[:, c_ga:c_gb])
    z = y * jax.nn.sigmoid(glu + bglu_ref[...])
    ya = _dot(z.astype(_BF16), wa_ref[...])
    outs = [_dot((wsums[wi] * pinv_ref[wi] - ub[:, group(wi)]).astype(_BF16), pw_ref[wi])
            for wi in windows]
    gate_b = _dot(h, wr_ref[:, c_gb:])
    pb = jnp.concatenate(outs, axis=1) * ps_ref[...]
    yb = _dot(pb.astype(_BF16), wb_ref[...])

    merged = jax.nn.sigmoid(gate_a) * ya + jax.nn.sigmoid(gate_b) * yb
    mixed = _dot(merged.astype(_BF16), wo_ref[...])
    o_ref[0] = x + gt_ref[0] * mixed


def _pool_inverse_counts(rows):
    pos = np.arange(rows) % GRID_W
    invs = []
    for w in POOL_WINDOWS:
        lo = np.clip(pos - w // 2, 0, GRID_W - 1)
        hi = np.clip(pos + w - 1 - w // 2, 0, GRID_W - 1) + 1
        invs.append(np.broadcast_to((1.0 / (hi - lo).astype(np.float32))[:, None], (rows, POOL_GROUP_CH)))
    return np.stack(invs)


def _mixer(x, ypre, mod3, norm_g, b_glu, pool_scale, w_in, w_glu, w_a, pool_w, w_b, w_out):
    bsz, n_tok, _ = x.shape
    tm = MIX_ROWS
    pinv = jnp.asarray(_pool_inverse_counts(tm), _F32)
    nw = len(POOL_WINDOWS)
    n_rest = D_POOL + 2 * D_MODEL
    hbm = pl.BlockSpec(memory_space=pl.ANY)
    return pl.pallas_call(
        _mix_kernel,
        grid=(bsz, n_tok // tm),
        in_specs=[
            pl.BlockSpec((1, tm, D_MODEL), lambda b, t: (b, t, 0)),
            pl.BlockSpec((1, tm, D_SSM), lambda b, t: (b, t, 0)),
            pl.BlockSpec((1, 1, D_MODEL), lambda b, t: (b, 0, 0)),
            pl.BlockSpec((1, 1, D_MODEL), lambda b, t: (b, 0, 1)),
            pl.BlockSpec((1, 1, D_MODEL), lambda b, t: (b, 0, 2)),
            _const_spec((1, D_MODEL)),
            _const_spec((1, D_SSM)),
            _const_spec((nw, tm, POOL_GROUP_CH)),
            _const_spec((1, D_POOL)),
            _const_spec((None, nw, POOL_GROUP_CH, POOL_GROUP_CH)),
            hbm, hbm, hbm, hbm, hbm,
        ],
        out_specs=pl.BlockSpec((1, tm, D_MODEL), lambda b, t: (b, t, 0)),
        out_shape=jax.ShapeDtypeStruct(x.shape, _F32),
        scratch_shapes=[
            pltpu.VMEM((D_MODEL, n_rest), _BF16),
            pltpu.VMEM((D_SSM, D_SSM), _BF16),
            pltpu.VMEM((D_SSM, D_MODEL), _BF16),
            pltpu.VMEM((D_POOL, D_MODEL), _BF16),
            pltpu.VMEM((D_MODEL, D_MODEL), _BF16),
            pltpu.VMEM((WEIGHT_STAGE_SLOTS, 4 * WEIGHT_STAGE_ROWS, n_rest), _F32),
            pltpu.SemaphoreType.DMA((WEIGHT_STAGE_SLOTS,)),
        ],
        compiler_params=pltpu.CompilerParams(
            dimension_semantics=("arbitrary", "arbitrary"), vmem_limit_bytes=VMEM_LIMIT),
        name="token_mixer",
    )(x, ypre, mod3, mod3, mod3, norm_g, b_glu, pinv, pool_scale, pool_w.astype(_BF16),
      w_in, w_glu, w_a, w_b, w_out)


def _stream_cast(jobs, stage_ref, sem):
    n_slots = stage_ref.shape[0]

    def copy(i):
        src = jobs[i][0]
        rows, cols = src.shape
        slot = i % n_slots
        return pltpu.make_async_copy(src, stage_ref.at[slot, pl.ds(0, rows), pl.ds(0, cols)], sem.at[slot])

    for i in range(min(n_slots - 1, len(jobs))):
        copy(i).start()
    for i, (src, dsts) in enumerate(jobs):
        if i + n_slots - 1 < len(jobs):
            copy(i + n_slots - 1).start()
        copy(i).wait()
        rows = src.shape[0]
        for dst, cols in dsts:
            dst[...] = stage_ref[i % n_slots, 0:rows, cols].astype(_BF16)


def _first_step():
    return jnp.logical_and(pl.program_id(0) == 0, pl.program_id(1) == 0)


def _ffn_kernel(x_ref, sh_ref, sc_ref, gt_ref, g2_ref, gf_ref, win_hbm, wout_hbm, o_ref,
                wg_ref, wu_ref, wo_ref, stage_in_ref, stage_out_ref, sem_in, sem_out):
    @pl.when(_first_step())
    def _load_weights():
        r = stage_in_ref.shape[1]
        _stream_cast([(win_hbm.at[0, pl.ds(k * r, r), :],
                       [(wg_ref.at[pl.ds(k * r, r), :], slice(0, FFN_HIDDEN)),
                        (wu_ref.at[pl.ds(k * r, r), :], slice(FFN_HIDDEN, 2 * FFN_HIDDEN))])
                      for k in range(D_MODEL // r)], stage_in_ref, sem_in)
        r = stage_out_ref.shape[1]
        _stream_cast([(wout_hbm.at[0, pl.ds(k * r, r), :], [(wo_ref.at[pl.ds(k * r, r), :], slice(0, D_MODEL))])
                      for k in range(FFN_HIDDEN // r)], stage_out_ref, sem_out)

    x = x_ref[0]
    h = _rms_mod(x, g2_ref[...] * (1.0 + sc_ref[0]), sh_ref[0]).astype(_BF16)
    down = None
    for c0, c1 in FFN_CHUNKS:
        gate = _dot(h, wg_ref[:, c0:c1])
        up = _dot(h, wu_ref[:, c0:c1])
        part = _dot((gate * jax.nn.sigmoid(gate) * up).astype(_BF16), wo_ref[c0:c1, :])
        down = part if down is None else down + part
    y = x + gt_ref[0] * down
    ms = jnp.mean(y * y, axis=-1, keepdims=True)
    o_ref[0] = (y * lax.rsqrt(ms + RMS_EPS)) * gf_ref[...]


def _ffn(x1, mod3, norm2_g, final_g, w_ffn, w_down):
    bsz, n_tok, _ = x1.shape
    tm = FFN_ROWS
    hbm = pl.BlockSpec(memory_space=pl.ANY)
    return pl.pallas_call(
        _ffn_kernel,
        grid=(bsz, n_tok // tm),
        in_specs=[
            pl.BlockSpec((1, tm, D_MODEL), lambda b, t: (b, t, 0)),
            pl.BlockSpec((1, 1, D_MODEL), lambda b, t: (b, 0, 3)),
            pl.BlockSpec((1, 1, D_MODEL), lambda b, t: (b, 0, 4)),
            pl.BlockSpec((1, 1, D_MODEL), lambda b, t: (b, 0, 5)),
            _const_spec((1, D_MODEL)),
            _const_spec((1, D_MODEL)),
            hbm,
            hbm,
        ],
        out_specs=pl.BlockSpec((1, tm, D_MODEL), lambda b, t: (b, t, 0)),
        out_shape=jax.ShapeDtypeStruct(x1.shape, _F32),
        scratch_shapes=[
            pltpu.VMEM((D_MODEL, FFN_HIDDEN), _BF16),
            pltpu.VMEM((D_MODEL, FFN_HIDDEN), _BF16),
            pltpu.VMEM((FFN_HIDDEN, D_MODEL), _BF16),
            pltpu.VMEM((WEIGHT_STAGE_SLOTS, WEIGHT_STAGE_ROWS, 2 * FFN_HIDDEN), _F32),
            pltpu.VMEM((WEIGHT_STAGE_SLOTS, FFN_HIDDEN // SUBLANES, D_MODEL), _F32),
            pltpu.SemaphoreType.DMA((WEIGHT_STAGE_SLOTS,)),
            pltpu.SemaphoreType.DMA((WEIGHT_STAGE_SLOTS,)),
        ],
        compiler_params=pltpu.CompilerParams(
            dimension_semantics=("arbitrary", "arbitrary"), vmem_limit_bytes=VMEM_LIMIT),
        name="swiglu_ffn",
    )(x1, mod3, mod3, mod3, norm2_g, final_g, w_ffn, w_down)


def _cmul(xr, xi, yr, yi):
    return xr * yr - xi * yi, xr * yi + xi * yr


def _split_bf16(v):
    hi = v.astype(_BF16)
    return hi, (v - hi.astype(_F32)).astype(_BF16)


def _dot3_nt(a, b):
    dims = (((1,), (1,)), ((), ()))
    nt = lambda p, q: lax.dot_general(p, q, dims, preferred_element_type=_F32)
    a_hi, a_lo = _split_bf16(a)
    b_hi, b_lo = _split_bf16(b)
    return nt(a_hi, b_hi) + (nt(a_hi, b_lo) + nt(a_lo, b_hi))


def _tables_kernel(a_ref, bt_ref, c_ref, d_ref,
                   mu_ref, wst_ref, mv_ref, a16r_ref, a16i_ref, kt_ref, khl_ref, vnat_ref, cp_ref):
    def pair_body(q, carry):
        for d in range(2):
            ar = a_ref[0, d, q]
            ai = a_ref[1, d, q]
            dt = jnp.exp(a_ref[2, d, q])
            mag = jnp.exp(ar * dt)
            ang = ai * dt
            abr, abi = mag * jnp.cos(ang), mag * jnp.sin(ang)
            den = ar * ar + ai * ai
            fr = ((abr - 1.0) * ar + abi * ai) / den
            fi = (abi * ar - (abr - 1.0) * ai) / den
            bbr, bbi = _cmul(bt_ref[0, d, q], bt_ref[1, d, q], fr, fi)
            cr = c_ref[0, d, q]
            ci = c_ref[1, d, q]
            pw = [(jnp.ones_like(ar), jnp.zeros_like(ar))]
            for _ in range(CHUNK):
                pw.append(_cmul(pw[-1][0], pw[-1][1], abr, abi))
            lanes_re = slice(2 * d * LANE, (2 * d + 1) * LANE)
            lanes_im = slice((2 * d + 1) * LANE, (2 * d + 2) * LANE)
            for lanes in (lanes_re, lanes_im):
                a16r_ref[q, :, lanes] = pw[CHUNK][0]
                a16i_ref[q, :, lanes] = pw[CHUNK][1]
            for sg in range(CHUNK):
                e = (CHUNK - 1 - sg) if d == 0 else sg
                wr, wi = _cmul(bbr, bbi, pw[e][0], pw[e][1])
                e = (sg + 1) if d == 0 else (CHUNK - sg)
                vr, vi = _cmul(cr, ci, pw[e][0], pw[e][1])
                for gg in range(2):
                    src = slice(gg * GROUP_CH, (gg + 1) * GROUP_CH)
                    dst = slice(gg * FLAT + sg * GROUP_CH, gg * FLAT + (sg + 1) * GROUP_CH)
                    wst_ref[q, dst, lanes_re] = wr[src].astype(_BF16)
                    wst_ref[q, dst, lanes_im] = wi[src].astype(_BF16)
                    vnat_ref[2 * d, dst, :] = vr[src]
                    vnat_ref[2 * d + 1, dst, :] = -vi[src]
            for gg in range(2):
                src = slice(gg * GROUP_CH, (gg + 1) * GROUP_CH)
                for k in range(CHUNK):
                    e = k if d == 0 else (CHUNK - 1 - k)
                    pr, pi = _cmul(cr[src], ci[src], pw[e][0], pw[e][1])
                    cp_ref[0, k * GROUP_CH:(k + 1) * GROUP_CH, :] = pr
                    cp_ref[1, k * GROUP_CH:(k + 1) * GROUP_CH, :] = pi
                kt = _dot3_nt(bbr[src], cp_ref[0]) - _dot3_nt(bbi[src], cp_ref[1])
                r0 = pl.multiple_of((2 * q + gg) * GROUP_CH, GROUP_CH)
                kt_ref[d, pl.ds(r0, GROUP_CH), :] = kt
        for part in range(4):
            mv_ref[q, part * LANE:(part + 1) * LANE, :] = vnat_ref[part].T.astype(_BF16)
        return carry

    lax.fori_loop(0, PAIRS, pair_body, 0)

    for d in range(2):
        khl_ref[2 * d], khl_ref[2 * d + 1] = _split_bf16(kt_ref[d])

    row = lax.broadcasted_iota(jnp.int32, (FLAT, FLAT), 0)
    col = lax.broadcasted_iota(jnp.int32, (FLAT, FLAT), 1)
    same_ch = (row % GROUP_CH) == (col % GROUP_CH)
    row_blk = row // GROUP_CH
    col_blk = col // GROUP_CH
    orow = lax.broadcasted_iota(jnp.int32, (GROUPS * GROUP_CH, FLAT), 0)
    ocol = lax.broadcasted_iota(jnp.int32, (GROUPS * GROUP_CH, FLAT), 1)
    skip_ch = (orow % GROUP_CH) == (ocol % GROUP_CH)
    ocol_blk = ocol // GROUP_CH
    d_col = d_ref[...]

    def toeplitz_body(sg, carry):
        sf = jnp.where(same_ch & (row_blk + sg == col_blk), 1.0, 0.0).astype(_BF16)
        sb = jnp.where(same_ch & (row_blk == col_blk + (CHUNK - 1) - sg), 1.0, 0.0).astype(_BF16)
        out = (_dot(khl_ref[0], sf) + _dot(khl_ref[1], sf)) + (_dot(khl_ref[2], sb) + _dot(khl_ref[3], sb))
        out = out + jnp.where(skip_ch & (ocol_blk == sg), d_col, 0.0)
        r0 = pl.multiple_of(sg * GROUP_CH, GROUP_CH)
        mu_ref[:, pl.ds(r0, GROUP_CH), :] = out.reshape(GROUPS, GROUP_CH, FLAT).astype(_BF16)
        return carry

    lax.fori_loop(0, CHUNK, toeplitz_body, 0)


def _s5_tables(a_re, a_im, log_dt, b_re, b_im, c_re, c_im, d_skip):
    f32 = _F32
    eye2 = jnp.eye(2, dtype=f32)

    def pair_blocks(re, im):
        v = jnp.stack([re, im]).astype(f32).reshape(2, 2, PAIRS, 2, GROUP_CH, 1, STATE)
        v = v * eye2[None, None, None, :, None, :, None]
        return v.reshape(2, 2, PAIRS, 2 * GROUP_CH, 2 * STATE)

    ldt = jnp.broadcast_to(log_dt.astype(f32)[..., None], (2, GROUPS, STATE))
    a_rows = jnp.stack([a_re.astype(f32), a_im.astype(f32), ldt]).reshape(3, 2, PAIRS, 1, 2 * STATE)
    args = (a_rows, pair_blocks(jnp.swapaxes(b_re, 2, 3), jnp.swapaxes(b_im, 2, 3)),
            pair_blocks(c_re, c_im), d_skip.astype(f32).reshape(D_SSM, 1))
    whole = lambda a: pl.BlockSpec(a.shape, lambda i, n=a.ndim: (0,) * n)
    out_shape = [
        jax.ShapeDtypeStruct((GROUPS, FLAT, FLAT), _BF16),
        jax.ShapeDtypeStruct((PAIRS, 2 * FLAT, PAIR_LANES), _BF16),
        jax.ShapeDtypeStruct((PAIRS, PAIR_LANES, 2 * FLAT), _BF16),
        jax.ShapeDtypeStruct((PAIRS, 1, PAIR_LANES), _F32),
        jax.ShapeDtypeStruct((PAIRS, 1, PAIR_LANES), _F32),
    ]
    return pl.pallas_call(
        _tables_kernel,
        grid=(1,),
        in_specs=[whole(a) for a in args],
        out_specs=[whole(s) for s in out_shape],
        out_shape=out_shape,
        scratch_shapes=[
            pltpu.VMEM((2, GROUPS * GROUP_CH, FLAT), _F32),
            pltpu.VMEM((4, GROUPS * GROUP_CH, FLAT), _BF16),
            pltpu.VMEM((4, 2 * FLAT, LANE), _F32),
            pltpu.VMEM((2, FLAT, LANE), _F32),
        ],
        compiler_params=pltpu.CompilerParams(
            dimension_semantics=("arbitrary",), vmem_limit_bytes=VMEM_LIMIT),
        name="s5_tables",
    )(*args)


def kernel(x, c, ctx, c_ctx, w_mod, b_mod, norm1_g, norm2_g, w_in, s5_a_re, s5_a_im, s5_log_dt,
           s5_b_re, s5_b_im, s5_c_re, s5_c_im, s5_d, w_glu, b_glu, pool_w, pool_scale,
           w_branch_a, w_branch_b, w_out, w_ffn_in, w_ffn_out, final_norm_g):
    bsz, n_tok, d = x.shape
    ctx_len = ctx.shape[1]
    assert d == D_MODEL and w_mod.shape[0] == 1 and bsz + 1 <= SUBLANES
    assert n_tok % SCAN_ROWS == 0 and n_tok % MIX_ROWS == 0 and MIX_ROWS % GRID_W == 0
    assert ctx_len % CHUNK == 0 and bsz * ctx_len <= SCAN_ROWS
    n_ctx = ctx_len // CHUNK

    cc_t = jnp.concatenate(
        [c.T, c_ctx[:, None], jnp.zeros((D_MODEL, SUBLANES - bsz - 1), _F32)], axis=1)
    mod3 = _modulation(cc_t, w_mod, b_mod, bsz + 1)

    mu, wst, mv, a16_re, a16_im = _s5_tables(
        s5_a_re[0], s5_a_im[0], s5_log_dt[0], s5_b_re[0], s5_b_im[0], s5_c_re[0], s5_c_im[0], s5_d[0])

    uflat, s_loc = _pass1(x, mod3, 0, norm1_g, w_in, wst)
    _, s_ctx = _pass1(ctx.reshape(1, bsz * ctx_len, D_MODEL), mod3, bsz, norm1_g, w_in, wst)
    xstart = _chunk_scan(s_loc, s_ctx, a16_re, a16_im, n_ctx)
    ypre = _readout(uflat, xstart, mu, mv)

    x1 = _mixer(x, ypre, mod3, norm1_g, b_glu, pool_scale, w_in, w_glu, w_branch_a, pool_w, w_branch_b, w_out)
    return _ffn(x1, mod3, norm2_g, final_norm_g.reshape(1, D_MODEL), w_ffn_in, w_ffn_out)
```

```python
import functools

import numpy as np
import jax
import jax.numpy as jnp
from jax import lax
from jax.experimental import pallas as pl
from jax.experimental.pallas import tpu as pltpu

_F32 = jnp.float32
_BF16 = jnp.bfloat16

D_MODEL = 1024
D_SSM = 512
D_POOL = 512
GROUPS = 32
STATE = 64
GROUP_CH = 16
CHUNK = 16
FLAT = CHUNK * GROUP_CH
PAIRS = GROUPS // 2
PAIR_LANES = 4 * 2 * STATE
LANE = 128
SUBLANES = 8
CHUNK_PITCH = 24
COL_BLOCKS = D_SSM // LANE
GRID_W = 64
POOL_WINDOWS = (2, 4, 8, 16)
POOL_GROUP_CH = D_POOL // len(POOL_WINDOWS)
FFN_HIDDEN = 2816
RMS_EPS = 1e-6

SCAN_TILE = 128
SCAN_ROWS = SCAN_TILE * CHUNK
NORM_ROWS = 512
MIX_ROWS = 1024
FFN_ROWS = 1024
FFN_CHUNKS = ((0, 1536), (1536, FFN_HIDDEN))
RELAYOUT_UNROLL = 8
WEIGHT_STAGE_ROWS = 64
WEIGHT_STAGE_SLOTS = 3
VMEM_LIMIT = 56 * 1024 * 1024


def _rms_mod(x, gain, sh):
    ms = jnp.mean(x * x, axis=-1, keepdims=True)
    return (x * lax.rsqrt(ms + RMS_EPS)) * gain + sh


def _dot(a, b):
    return jnp.dot(a, b, preferred_element_type=_F32)


def _const_spec(shape, index=None):
    index = (0,) * len(shape) if index is None else index
    return pl.BlockSpec(shape, lambda *_: index, pipeline_mode=pl.Buffered(1))


def _mod_kernel(ct_ref, w_ref, b_ref, o_ref, *, n_rows):
    ct = ct_ref[...]
    a = ct * jax.nn.sigmoid(ct)
    w = w_ref[...]
    b = b_ref[...]
    for r in range(n_rows):
        o_ref[r] = jnp.sum(w * a[:, r:r + 1], axis=0, keepdims=True) + b
    for r in range(n_rows, o_ref.shape[0]):
        o_ref[r] = jnp.zeros_like(b)


def _modulation(cc_t, w_mod, b_mod, n_rows):
    n_out = w_mod.shape[-1]
    blk = 1024
    return pl.pallas_call(
        functools.partial(_mod_kernel, n_rows=n_rows),
        grid=(n_out // blk,),
        in_specs=[
            pl.BlockSpec((D_MODEL, SUBLANES), lambda i: (0, 0)),
            pl.BlockSpec((None, D_MODEL, blk), lambda i: (0, 0, i)),
            pl.BlockSpec((1, blk), lambda i: (0, i)),
        ],
        out_specs=pl.BlockSpec((SUBLANES, 1, blk), lambda i: (0, 0, i)),
        out_shape=jax.ShapeDtypeStruct((SUBLANES, 1, n_out), _F32),
        compiler_params=pltpu.CompilerParams(
            dimension_semantics=("arbitrary",), vmem_limit_bytes=VMEM_LIMIT),
        name="adaln_mod",
    )(cc_t, w_mod, b_mod)


def _to_pitch(v):
    n = v.shape[0] // CHUNK
    v = v.reshape(n, CHUNK, v.shape[1])
    pad = jnp.zeros((n, CHUNK_PITCH - CHUNK, v.shape[2]), v.dtype)
    return jnp.concatenate([v, pad], axis=1).reshape(n * CHUNK_PITCH, v.shape[2])


def _from_pitch(v):
    n = v.shape[0] // CHUNK_PITCH
    return v.reshape(n, CHUNK_PITCH, v.shape[1])[:, :CHUNK, :].reshape(n * CHUNK, v.shape[1])


def _scan_constants(ar_ref, ai_ref, c_ref, dirs):
    rb = SUBLANES
    rid = lax.broadcasted_iota(jnp.int32, (rb, LANE), 0)

    def body(q, carry):
        for d in dirs:
            o = d * 2 * LANE
            a_r = ar_ref[q][:, o:o + LANE]
            a_i = ai_ref[q][:, o:o + LANE]
            pows = [(a_r, a_i)]
            for _ in range(rb - 1):
                pows.append(_cmul(pows[-1][0], pows[-1][1], a_r, a_i))
            for k, shift in enumerate((1, 2, 4)):
                keep = (rid >= shift) if d == 0 else (rid < rb - shift)
                c_ref[d, q, 2 * k] = jnp.where(keep, pows[shift - 1][0], 0.0)
                c_ref[d, q, 2 * k + 1] = jnp.where(keep, pows[shift - 1][1], 0.0)
            p_r = jnp.zeros((rb, LANE), _F32)
            p_i = jnp.zeros((rb, LANE), _F32)
            for r in range(rb):
                e = r if d == 0 else rb - 1 - r
                p_r = jnp.where(rid == r, pows[e][0], p_r)
                p_i = jnp.where(rid == r, pows[e][1], p_i)
            c_ref[d, q, 6] = p_r
            c_ref[d, q, 7] = p_i
        return carry

    lax.fori_loop(0, PAIRS, body, 0)


def _scan_block(c_ref, d, q, s_re, s_im, xin_re, xin_im):
    rb = SUBLANES
    rid = lax.broadcasted_iota(jnp.int32, (rb, LANE), 0)

    def shifted(v, k):
        return pltpu.roll(v, k if d == 0 else rb - k, 0)

    t_re, t_im = s_re, s_im
    for k in range(3):
        a_r = c_ref[d, q, 2 * k]
        a_i = c_ref[d, q, 2 * k + 1]
        u_re, u_im = shifted(t_re, 1 << k), shifted(t_im, 1 << k)
        t_re, t_im = t_re + (a_r * u_re - a_i * u_im), t_im + (a_r * u_im + a_i * u_re)
    p_r = c_ref[d, q, 6]
    p_i = c_ref[d, q, 7]
    after_re = t_re + (p_r * xin_re - p_i * xin_im)
    after_im = t_im + (p_r * xin_im + p_i * xin_re)
    first = 0 if d == 0 else rb - 1
    last = rb - 1 - first
    start_re = jnp.where(rid == first, xin_re, shifted(after_re, 1))
    start_im = jnp.where(rid == first, xin_im, shifted(after_im, 1))
    return start_re, start_im, after_re[last:last + 1], after_im[last:last + 1]


def _scan_rows(c_ref, d, q, s, xin, blocks):
    rb = SUBLANES
    x_re, x_im = xin[:, :LANE], xin[:, LANE:]
    starts = {}
    for blk in (blocks if d == 0 else blocks[::-1]):
        r = slice(blk * rb, (blk + 1) * rb)
        st_re, st_im, x_re, x_im = _scan_block(c_ref, d, q, s[r, :LANE], s[r, LANE:], x_re, x_im)
        starts[blk] = (st_re, st_im)
    return starts, jnp.concatenate([x_re, x_im], axis=1)


def _starts_to_rows(starts, blocks):
    return jnp.concatenate(
        [jnp.concatenate([starts[b][0] for b in blocks], axis=0),
         jnp.concatenate([starts[b][1] for b in blocks], axis=0)], axis=1)


def _p1_project(x_ref, sh_ref, sc_ref, g_ref, wa_ref, h_ref, u_ref, ut_ref, rows):
    gain = g_ref[...] * (1.0 + sc_ref[0])
    sh = sh_ref[0]
    wa = wa_ref[...].astype(_BF16)
    block_chunks = NORM_ROWS // CHUNK

    def norm_block(i):
        r = slice(i * NORM_ROWS, (i + 1) * NORM_ROWS)
        h_ref[r, :] = _rms_mod(x_ref[0, r, :], gain, sh).astype(_BF16)

    norm_block(0)
    for i in range(rows // NORM_ROWS):
        if (i + 1) * NORM_ROWS < rows:
            norm_block(i + 1)
        u = _dot(h_ref[i * NORM_ROWS:(i + 1) * NORM_ROWS, :], wa)
        pr = slice(i * block_chunks * CHUNK_PITCH, (i + 1) * block_chunks * CHUNK_PITCH)
        for cb in range(COL_BLOCKS):
            u_ref[cb, pr, :] = _to_pitch(u[:, cb * LANE:(cb + 1) * LANE])
    if rows < SCAN_ROWS:
        first = rows // CHUNK * CHUNK_PITCH
        for cb in range(COL_BLOCKS):
            u_ref[cb, first:, :] = jnp.zeros((SCAN_TILE * CHUNK_PITCH - first, LANE), _F32)

    def slab_body(sg, carry):
        r0 = pl.multiple_of(sg * GROUP_CH, GROUP_CH)
        for cb in range(COL_BLOCKS):
            slab = u_ref[cb, pl.ds(sg, SCAN_TILE, stride=CHUNK_PITCH), :]
            ut_ref[cb * 8:(cb + 1) * 8, pl.ds(r0, GROUP_CH), :] = (
                slab.astype(_BF16).T.reshape(8, GROUP_CH, SCAN_TILE))
        return carry

    lax.fori_loop(0, CHUNK, slab_body, 0, unroll=RELAYOUT_UNROLL)


def _pair_states(ut_ref, wst_ref, q):
    uf0 = ut_ref[2 * q].T
    uf1 = ut_ref[2 * q + 1].T
    return uf0, uf1, _dot(uf0, wst_ref[q, :FLAT, :]) + _dot(uf1, wst_ref[q, FLAT:, :])


def _p1_kernel(x_ref, sh_ref, sc_ref, g_ref, wa_ref, wst_ref, ar_ref, ai_ref, seed_ref,
               uflat_ref, xf_ref, sb_ref, h_ref, u_ref, ut_ref, c_ref, carry_ref):
    @pl.when(_first_step())
    def _constants():
        _scan_constants(ar_ref, ai_ref, c_ref, (0,))

    @pl.when(pl.program_id(1) == 0)
    def _seed():
        carry_ref[...] = seed_ref[0]

    _p1_project(x_ref, sh_ref, sc_ref, g_ref, wa_ref, h_ref, u_ref, ut_ref, SCAN_ROWS)
    half = PAIR_LANES // 2
    blocks = list(range(SCAN_TILE // SUBLANES))

    def pair_body(q, carry):
        uf0, uf1, s = _pair_states(ut_ref, wst_ref, q)
        uflat_ref[0, 2 * q] = uf0
        uflat_ref[0, 2 * q + 1] = uf1
        starts, x_out = _scan_rows(c_ref, 0, q, s[:, :half], carry_ref[q], blocks)
        xf_ref[0, q] = _starts_to_rows(starts, blocks).astype(_BF16)
        carry_ref[q] = x_out
        sb_ref[0, q] = s[:, half:]
        return carry

    lax.fori_loop(0, PAIRS, pair_body, 0, unroll=RELAYOUT_UNROLL)


def _p1_ctx_kernel(x_ref, sh_ref, sc_ref, g_ref, wa_ref, wst_ref, ar_ref, ai_ref,
                   seedf_ref, seedb_ref, h_ref, u_ref, ut_ref, c_ref, *, rows, bsz):
    _scan_constants(ar_ref, ai_ref, c_ref, (0, 1))
    _p1_project(x_ref, sh_ref, sc_ref, g_ref, wa_ref, h_ref, u_ref, ut_ref, rows)
    half = PAIR_LANES // 2
    per_seq = rows // CHUNK // SUBLANES // bsz
    zero = jnp.zeros((1, half), _F32)

    def pair_body(q, carry):
        _, _, s = _pair_states(ut_ref, wst_ref, q)
        for b in range(bsz):
            blocks = list(range(b * per_seq, (b + 1) * per_seq))
            _, seedf_ref[b, q] = _scan_rows(c_ref, 0, q, s[:, :half], zero, blocks)
            _, seedb_ref[b, q] = _scan_rows(c_ref, 1, q, s[:, half:], zero, blocks)
        return carry

    lax.fori_loop(0, PAIRS, pair_body, 0, unroll=RELAYOUT_UNROLL)


def _p1_scratch():
    return [
        pltpu.VMEM((SCAN_ROWS, D_MODEL), _BF16),
        pltpu.VMEM((COL_BLOCKS, SCAN_TILE * CHUNK_PITCH, LANE), _F32),
        pltpu.VMEM((GROUPS, FLAT, SCAN_TILE), _BF16),
        pltpu.VMEM((2, PAIRS, 8, SUBLANES, LANE), _F32),
    ]


def _pass1_ctx(ctx_rows, mod3, mod_row, norm_g, w_in, wst, a16_re, a16_im, bsz):
    rows = ctx_rows.shape[1]
    assert rows % NORM_ROWS == 0 and rows <= SCAN_ROWS and rows % (bsz * CHUNK * SUBLANES) == 0
    half = PAIR_LANES // 2
    seed = jax.ShapeDtypeStruct((bsz, PAIRS, 1, half), _F32)
    return pl.pallas_call(
        functools.partial(_p1_ctx_kernel, rows=rows, bsz=bsz),
        grid=(1, 1),
        in_specs=[
            pl.BlockSpec((1, rows, D_MODEL), lambda b, t: (0, 0, 0)),
            pl.BlockSpec((1, 1, D_MODEL), lambda b, t: (mod_row, 0, 0)),
            pl.BlockSpec((1, 1, D_MODEL), lambda b, t: (mod_row, 0, 1)),
            _const_spec((1, D_MODEL)),
            _const_spec((None, D_MODEL, D_SSM)),
            _const_spec((PAIRS, 2 * FLAT, PAIR_LANES)),
            _const_spec((PAIRS, 1, PAIR_LANES)),
            _const_spec((PAIRS, 1, PAIR_LANES)),
        ],
        out_specs=[pl.BlockSpec((bsz, PAIRS, 1, half), lambda b, t: (0, 0, 0, 0))] * 2,
        out_shape=[seed, seed],
        scratch_shapes=_p1_scratch(),
        compiler_params=pltpu.CompilerParams(
            dimension_semantics=("arbitrary", "arbitrary"), vmem_limit_bytes=VMEM_LIMIT),
        name="s5_context_states",
    )(ctx_rows, mod3, mod3, norm_g, w_in, wst, a16_re, a16_im)


def _pass1(x, mod3, norm_g, w_in, wst, a16_re, a16_im, seed_f):
    bsz, n_tok, _ = x.shape
    assert n_tok % SCAN_ROWS == 0
    nt = n_tok // SCAN_ROWS
    n_chunks = nt * SCAN_TILE
    half = PAIR_LANES // 2
    return pl.pallas_call(
        _p1_kernel,
        grid=(bsz, nt),
        in_specs=[
            pl.BlockSpec((1, SCAN_ROWS, D_MODEL), lambda b, t: (b, t, 0)),
            pl.BlockSpec((1, 1, D_MODEL), lambda b, t: (b, 0, 0)),
            pl.BlockSpec((1, 1, D_MODEL), lambda b, t: (b, 0, 1)),
            _const_spec((1, D_MODEL)),
            _const_spec((None, D_MODEL, D_SSM)),
            _const_spec((PAIRS, 2 * FLAT, PAIR_LANES)),
            _const_spec((PAIRS, 1, PAIR_LANES)),
            _const_spec((PAIRS, 1, PAIR_LANES)),
            pl.BlockSpec((1, PAIRS, 1, half), lambda b, t: (b, 0, 0, 0)),
        ],
        out_specs=[
            pl.BlockSpec((1, GROUPS, SCAN_TILE, FLAT), lambda b, t: (b, 0, t, 0)),
            pl.BlockSpec((1, PAIRS, SCAN_TILE, half), lambda b, t: (b, 0, t, 0)),
            pl.BlockSpec((1, PAIRS, SCAN_TILE, half), lambda b, t: (b, 0, t, 0)),
        ],
        out_shape=[
            jax.ShapeDtypeStruct((bsz, GROUPS, n_chunks, FLAT), _BF16),
            jax.ShapeDtypeStruct((bsz, PAIRS, n_chunks, half), _BF16),
            jax.ShapeDtypeStruct((bsz, PAIRS, n_chunks, half), _F32),
        ],
        scratch_shapes=_p1_scratch() + [pltpu.VMEM((PAIRS, 1, half), _F32)],
        compiler_params=pltpu.CompilerParams(
            dimension_semantics=("arbitrary", "arbitrary"), vmem_limit_bytes=VMEM_LIMIT),
        name="s5_chunk_states",
    )(x, mod3, mod3, norm_g, w_in, wst, a16_re, a16_im, seed_f)


def _readout_kernel(uflat_ref, xf_ref, sb_ref, seed_ref, ar_ref, ai_ref, mu_ref, mv_ref, y_ref,
                    yt_ref, ys_ref, c_ref, carry_ref):
    @pl.when(_first_step())
    def _constants():
        _scan_constants(ar_ref, ai_ref, c_ref, (1,))

    @pl.when(pl.program_id(1) == 0)
    def _seed():
        carry_ref[...] = seed_ref[0]

    blocks = list(range(SCAN_TILE // SUBLANES))

    def pair_body(q, carry):
        y0 = _dot(uflat_ref[0, 2 * q], mu_ref[2 * q])
        y1 = _dot(uflat_ref[0, 2 * q + 1], mu_ref[2 * q + 1])
        starts, x_out = _scan_rows(c_ref, 1, q, sb_ref[0, q], carry_ref[q], blocks)
        carry_ref[q] = x_out
        xs = jnp.concatenate([xf_ref[0, q], _starts_to_rows(starts, blocks).astype(_BF16)], axis=1)
        yx = _dot(xs, mv_ref[q])
        y = jnp.concatenate([y0, y1], axis=1) + yx
        yt = y.astype(yt_ref.dtype).T
        yt_ref[2 * q] = yt[:FLAT]
        yt_ref[2 * q + 1] = yt[FLAT:]
        return carry

    lax.fori_loop(0, PAIRS, pair_body, 0, unroll=RELAYOUT_UNROLL)

    def slab_body(sg, carry):
        r0 = pl.multiple_of(sg * GROUP_CH, GROUP_CH)
        for cb in range(COL_BLOCKS):
            yt = yt_ref[cb * 8:(cb + 1) * 8, pl.ds(r0, GROUP_CH), :].reshape(LANE, SCAN_TILE)
            ys_ref[cb, pl.ds(sg, SCAN_TILE, stride=CHUNK_PITCH), :] = yt.T.astype(_F32)
        return carry

    lax.fori_loop(0, CHUNK, slab_body, 0, unroll=RELAYOUT_UNROLL)
    for cb in range(COL_BLOCKS):
        y_ref[0, :, cb * LANE:(cb + 1) * LANE] = _from_pitch(ys_ref[cb]).astype(y_ref.dtype)


def _readout(uflat, xf, sb, seed_b, a16_re, a16_im, mu, mv):
    bsz, _, n_chunks, _ = uflat.shape
    nt = n_chunks // SCAN_TILE
    half = PAIR_LANES // 2
    rev = lambda b, t: (b, 0, nt - 1 - t, 0)
    return pl.pallas_call(
        _readout_kernel,
        grid=(bsz, nt),
        in_specs=[
            pl.BlockSpec((1, GROUPS, SCAN_TILE, FLAT), rev),
            pl.BlockSpec((1, PAIRS, SCAN_TILE, half), rev),
            pl.BlockSpec((1, PAIRS, SCAN_TILE, half), rev),
            pl.BlockSpec((1, PAIRS, 1, half), lambda b, t: (b, 0, 0, 0)),
            _const_spec((PAIRS, 1, PAIR_LANES)),
            _const_spec((PAIRS, 1, PAIR_LANES)),
            _const_spec((GROUPS, FLAT, FLAT)),
            _const_spec((PAIRS, PAIR_LANES, 2 * FLAT)),
        ],
        out_specs=pl.BlockSpec((1, SCAN_ROWS, D_SSM), lambda b, t: (b, nt - 1 - t, 0)),
        out_shape=jax.ShapeDtypeStruct((bsz, n_chunks * CHUNK, D_SSM), _BF16),
        scratch_shapes=[
            pltpu.VMEM((GROUPS, FLAT, SCAN_TILE), _BF16),
            pltpu.VMEM((COL_BLOCKS, SCAN_TILE * CHUNK_PITCH, LANE), _F32),
            pltpu.VMEM((2, PAIRS, 8, SUBLANES, LANE), _F32),
            pltpu.VMEM((PAIRS, 1, half), _F32),
        ],
        compiler_params=pltpu.CompilerParams(
            dimension_semantics=("arbitrary", "arbitrary"), vmem_limit_bytes=VMEM_LIMIT),
        name="s5_readout",
    )(uflat, xf, sb, seed_b, a16_re, a16_im, mu, mv)


def _window_sum(u, w):
    assert w // 2 <= SUBLANES
    rows, lanes = u.shape
    nb = rows // GRID_W
    pad = jnp.zeros((nb, SUBLANES, lanes), _F32)
    z = jnp.concatenate([pad, u.reshape(nb, GRID_W, lanes), pad], axis=1)
    n = nb * (GRID_W + 2 * SUBLANES)
    z = z.reshape(n, lanes)
    acc = z + pltpu.roll(z, 1, 0)
    m = 2
    while m < w:
        acc = pltpu.roll(acc, m // 2, 0) + pltpu.roll(acc, n - m // 2, 0)
        m *= 2
    return acc.reshape(nb, GRID_W + 2 * SUBLANES, lanes)[:, SUBLANES:SUBLANES + GRID_W, :].reshape(rows, lanes)


def _mix_kernel(x_ref, y_ref, sh_ref, sc_ref, gt_ref, g_ref, bglu_ref, pinv_ref, ps_ref, pw_ref,
                win_hbm, wglu_hbm, wa_hbm, wb_hbm, wo_hbm, o_ref,
                wr_ref, wglu_ref, wa_ref, wb_ref, wo_ref, stage_ref, sem):
    @pl.when(_first_step())
    def _load_weights():
        r = stage_ref.shape[1]
        n_rest = D_POOL + 2 * D_MODEL

        def pieces(src3, dst, n_rows, c0, n_cols):
            return [(src3.at[0, pl.ds(k * r, r), pl.ds(c0, n_cols)],
                     [(dst.at[pl.ds(k * r, r), :], slice(0, n_cols))]) for k in range(n_rows // r)]

        jobs = pieces(win_hbm, wr_ref, D_MODEL, D_SSM, n_rest)
        jobs += pieces(wglu_hbm, wglu_ref, D_SSM, 0, D_SSM)
        jobs += pieces(wa_hbm, wa_ref, D_SSM, 0, D_MODEL)
        jobs += pieces(wb_hbm, wb_ref, D_POOL, 0, D_MODEL)
        jobs += pieces(wo_hbm, wo_ref, D_MODEL, 0, D_MODEL)
        _stream_cast(jobs, stage_ref, sem)

    x = x_ref[0]
    h = _rms_mod(x, g_ref[...] * (1.0 + sc_ref[0]), sh_ref[0]).astype(_BF16)
    c_pool, c_ga, c_gb = 0, D_POOL, D_POOL + D_MODEL
    windows = range(len(POOL_WINDOWS))
    group = lambda wi: slice(wi * POOL_GROUP_CH, (wi + 1) * POOL_GROUP_CH)

    ub = _dot(h, wr_ref[:, c_pool:c_ga])
    y = jax.nn.gelu(y_ref[0].astype(_F32))
    glu = _dot(y.astype(_BF16), wglu_ref[...])
    wsums = [_window_sum(ub[:, group(wi)], POOL_WINDOWS[wi]) for wi in windows]
    gate_a = _dot(h, wr_ref[:, c_ga:c_gb])
    z = y * jax.nn.sigmoid(glu + bglu_ref[...])
    ya = _dot(z.astype(_BF16), wa_ref[...])
    outs = [_dot((wsums[wi] * pinv_ref[wi] - ub[:, group(wi)]).astype(_BF16), pw_ref[wi])
            for wi in windows]
    gate_b = _dot(h, wr_ref[:, c_gb:])
    pb = jnp.concatenate(outs, axis=1) * ps_ref[...]
    yb = _dot(pb.astype(_BF16), wb_ref[...])

    merged = jax.nn.sigmoid(gate_a) * ya + jax.nn.sigmoid(gate_b) * yb
    mixed = _dot(merged.astype(_BF16), wo_ref[...])
    o_ref[0] = x + gt_ref[0] * mixed


def _pool_inverse_counts(rows):
    pos = np.arange(rows) % GRID_W
    invs = []
    for w in POOL_WINDOWS:
        lo = np.clip(pos - w // 2, 0, GRID_W - 1)
        hi = np.clip(pos + w - 1 - w // 2, 0, GRID_W - 1) + 1
        invs.append(np.broadcast_to((1.0 / (hi - lo).astype(np.float32))[:, None], (rows, POOL_GROUP_CH)))
    return np.stack(invs)


def _mixer(x, ypre, mod3, norm_g, b_glu, pool_scale, w_in, w_glu, w_a, pool_w, w_b, w_out):
    bsz, n_tok, _ = x.shape
    tm = MIX_ROWS
    pinv = jnp.asarray(_pool_inverse_counts(tm), _F32)
    nw = len(POOL_WINDOWS)
    n_rest = D_POOL + 2 * D_MODEL
    hbm = pl.BlockSpec(memory_space=pl.ANY)
    return pl.pallas_call(
        _mix_kernel,
        grid=(bsz, n_tok // tm),
        in_specs=[
            pl.BlockSpec((1, tm, D_MODEL), lambda b, t: (b, t, 0)),
            pl.BlockSpec((1, tm, D_SSM), lambda b, t: (b, t, 0)),
            pl.BlockSpec((1, 1, D_MODEL), lambda b, t: (b, 0, 0)),
            pl.BlockSpec((1, 1, D_MODEL), lambda b, t: (b, 0, 1)),
            pl.BlockSpec((1, 1, D_MODEL), lambda b, t: (b, 0, 2)),
            _const_spec((1, D_MODEL)),
            _const_spec((1, D_SSM)),
            _const_spec((nw, tm, POOL_GROUP_CH)),
            _const_spec((1, D_POOL)),
            _const_spec((None, nw, POOL_GROUP_CH, POOL_GROUP_CH)),
            hbm, hbm, hbm, hbm, hbm,
        ],
        out_specs=pl.BlockSpec((1, tm, D_MODEL), lambda b, t: (b, t, 0)),
        out_shape=jax.ShapeDtypeStruct(x.shape, _F32),
        scratch_shapes=[
            pltpu.VMEM((D_MODEL, n_rest), _BF16),
            pltpu.VMEM((D_SSM, D_SSM), _BF16),
            pltpu.VMEM((D_SSM, D_MODEL), _BF16),
            pltpu.VMEM((D_POOL, D_MODEL), _BF16),
            pltpu.VMEM((D_MODEL, D_MODEL), _BF16),
            pltpu.VMEM((WEIGHT_STAGE_SLOTS, 4 * WEIGHT_STAGE_ROWS, n_rest), _F32),
            pltpu.SemaphoreType.DMA((WEIGHT_STAGE_SLOTS,)),
        ],
        compiler_params=pltpu.CompilerParams(
            dimension_semantics=("arbitrary", "arbitrary"), vmem_limit_bytes=VMEM_LIMIT),
        name="token_mixer",
    )(x, ypre, mod3, mod3, mod3, norm_g, b_glu, pinv, pool_scale, pool_w.astype(_BF16),
      w_in, w_glu, w_a, w_b, w_out)


def _stream_cast(jobs, stage_ref, sem):
    n_slots = stage_ref.shape[0]

    def copy(i):
        src = jobs[i][0]
        rows, cols = src.shape
        slot = i % n_slots
        return pltpu.make_async_copy(src, stage_ref.at[slot, pl.ds(0, rows), pl.ds(0, cols)], sem.at[slot])

    for i in range(min(n_slots - 1, len(jobs))):
        copy(i).start()
    for i, (src, dsts) in enumerate(jobs):
        if i + n_slots - 1 < len(jobs):
            copy(i + n_slots - 1).start()
        copy(i).wait()
        rows = src.shape[0]
        for dst, cols in dsts:
            dst[...] = stage_ref[i % n_slots, 0:rows, cols].astype(_BF16)


def _first_step():
    return jnp.logical_and(pl.program_id(0) == 0, pl.program_id(1) == 0)


def _ffn_kernel(x_ref, sh_ref, sc_ref, gt_ref, g2_ref, gf_ref, win_hbm, wout_hbm, o_ref,
                wg_ref, wu_ref, wo_ref, stage_in_ref, stage_out_ref, sem_in, sem_out):
    @pl.when(_first_step())
    def _load_weights():
        r = stage_in_ref.shape[1]
        _stream_cast([(win_hbm.at[0, pl.ds(k * r, r), :],
                       [(wg_ref.at[pl.ds(k * r, r), :], slice(0, FFN_HIDDEN)),
                        (wu_ref.at[pl.ds(k * r, r), :], slice(FFN_HIDDEN, 2 * FFN_HIDDEN))])
                      for k in range(D_MODEL // r)], stage_in_ref, sem_in)
        r = stage_out_ref.shape[1]
        _stream_cast([(wout_hbm.at[0, pl.ds(k * r, r), :], [(wo_ref.at[pl.ds(k * r, r), :], slice(0, D_MODEL))])
                      for k in range(FFN_HIDDEN // r)], stage_out_ref, sem_out)

    x = x_ref[0]
    h = _rms_mod(x, g2_ref[...] * (1.0 + sc_ref[0]), sh_ref[0]).astype(_BF16)
    down = None
    for c0, c1 in FFN_CHUNKS:
        gate = _dot(h, wg_ref[:, c0:c1])
        up = _dot(h, wu_ref[:, c0:c1])
        part = _dot((gate * jax.nn.sigmoid(gate) * up).astype(_BF16), wo_ref[c0:c1, :])
        down = part if down is None else down + part
    y = x + gt_ref[0] * down
    ms = jnp.mean(y * y, axis=-1, keepdims=True)
    o_ref[0] = (y * lax.rsqrt(ms + RMS_EPS)) * gf_ref[...]


def _ffn(x1, mod3, norm2_g, final_g, w_ffn, w_down):
    bsz, n_tok, _ = x1.shape
    tm = FFN_ROWS
    hbm = pl.BlockSpec(memory_space=pl.ANY)
    return pl.pallas_call(
        _ffn_kernel,
        grid=(bsz, n_tok // tm),
        in_specs=[
            pl.BlockSpec((1, tm, D_MODEL), lambda b, t: (b, t, 0)),
            pl.BlockSpec((1, 1, D_MODEL), lambda b, t: (b, 0, 3)),
            pl.BlockSpec((1, 1, D_MODEL), lambda b, t: (b, 0, 4)),
            pl.BlockSpec((1, 1, D_MODEL), lambda b, t: (b, 0, 5)),
            _const_spec((1, D_MODEL)),
            _const_spec((1, D_MODEL)),
            hbm,
            hbm,
        ],
        out_specs=pl.BlockSpec((1, tm, D_MODEL), lambda b, t: (b, t, 0)),
        out_shape=jax.ShapeDtypeStruct(x1.shape, _F32),
        scratch_shapes=[
            pltpu.VMEM((D_MODEL, FFN_HIDDEN), _BF16),
            pltpu.VMEM((D_MODEL, FFN_HIDDEN), _BF16),
            pltpu.VMEM((FFN_HIDDEN, D_MODEL), _BF16),
            pltpu.VMEM((WEIGHT_STAGE_SLOTS, WEIGHT_STAGE_ROWS, 2 * FFN_HIDDEN), _F32),
            pltpu.VMEM((WEIGHT_STAGE_SLOTS, FFN_HIDDEN // SUBLANES, D_MODEL), _F32),
            pltpu.SemaphoreType.DMA((WEIGHT_STAGE_SLOTS,)),
            pltpu.SemaphoreType.DMA((WEIGHT_STAGE_SLOTS,)),
        ],
        compiler_params=pltpu.CompilerParams(
            dimension_semantics=("arbitrary", "arbitrary"), vmem_limit_bytes=VMEM_LIMIT),
        name="swiglu_ffn",
    )(x1, mod3, mod3, mod3, norm2_g, final_g, w_ffn, w_down)


def _cmul(xr, xi, yr, yi):
    return xr * yr - xi * yi, xr * yi + xi * yr


def _split_bf16(v):
    hi = v.astype(_BF16)
    return hi, (v - hi.astype(_F32)).astype(_BF16)


def _dot3_nt(a, b):
    dims = (((1,), (1,)), ((), ()))
    nt = lambda p, q: lax.dot_general(p, q, dims, preferred_element_type=_F32)
    a_hi, a_lo = _split_bf16(a)
    b_hi, b_lo = _split_bf16(b)
    return nt(a_hi, b_hi) + (nt(a_hi, b_lo) + nt(a_lo, b_hi))


def _tables_kernel(a_ref, bt_ref, c_ref, d_ref,
                   mu_ref, wst_ref, mv_ref, a16r_ref, a16i_ref, kt_ref, khl_ref, vnat_ref, cp_ref):
    def pair_body(q, carry):
        for d in range(2):
            ar = a_ref[0, d, q]
            ai = a_ref[1, d, q]
            dt = jnp.exp(a_ref[2, d, q])
            mag = jnp.exp(ar * dt)
            ang = ai * dt
            abr, abi = mag * jnp.cos(ang), mag * jnp.sin(ang)
            den = ar * ar + ai * ai
            fr = ((abr - 1.0) * ar + abi * ai) / den
            fi = (abi * ar - (abr - 1.0) * ai) / den
            bbr, bbi = _cmul(bt_ref[0, d, q], bt_ref[1, d, q], fr, fi)
            cr = c_ref[0, d, q]
            ci = c_ref[1, d, q]
            pw = [(jnp.ones_like(ar), jnp.zeros_like(ar))]
            for _ in range(CHUNK):
                pw.append(_cmul(pw[-1][0], pw[-1][1], abr, abi))
            lanes_re = slice(2 * d * LANE, (2 * d + 1) * LANE)
            lanes_im = slice((2 * d + 1) * LANE, (2 * d + 2) * LANE)
            for lanes in (lanes_re, lanes_im):
                a16r_ref[q, :, lanes] = pw[CHUNK][0]
                a16i_ref[q, :, lanes] = pw[CHUNK][1]
            for sg in range(CHUNK):
                e = (CHUNK - 1 - sg) if d == 0 else sg
                wr, wi = _cmul(bbr, bbi, pw[e][0], pw[e][1])
                e = (sg + 1) if d == 0 else (CHUNK - sg)
                vr, vi = _cmul(cr, ci, pw[e][0], pw[e][1])
                for gg in range(2):
                    src = slice(gg * GROUP_CH, (gg + 1) * GROUP_CH)
                    dst = slice(gg * FLAT + sg * GROUP_CH, gg * FLAT + (sg + 1) * GROUP_CH)
                    wst_ref[q, dst, lanes_re] = wr[src].astype(_BF16)
                    wst_ref[q, dst, lanes_im] = wi[src].astype(_BF16)
                    vnat_ref[2 * d, dst, :] = vr[src]
                    vnat_ref[2 * d + 1, dst, :] = -vi[src]
            for gg in range(2):
                src = slice(gg * GROUP_CH, (gg + 1) * GROUP_CH)
                for k in range(CHUNK):
                    e = k if d == 0 else (CHUNK - 1 - k)
                    pr, pi = _cmul(cr[src], ci[src], pw[e][0], pw[e][1])
                    cp_ref[0, k * GROUP_CH:(k + 1) * GROUP_CH, :] = pr
                    cp_ref[1, k * GROUP_CH:(k + 1) * GROUP_CH, :] = pi
                kt = _dot3_nt(bbr[src], cp_ref[0]) - _dot3_nt(bbi[src], cp_ref[1])
                r0 = pl.multiple_of((2 * q + gg) * GROUP_CH, GROUP_CH)
                kt_ref[d, pl.ds(r0, GROUP_CH), :] = kt
        for part in range(4):
            mv_ref[q, part * LANE:(part + 1) * LANE, :] = vnat_ref[part].T.astype(_BF16)
        return carry

    lax.fori_loop(0, PAIRS, pair_body, 0)

    for d in range(2):
        khl_ref[2 * d], khl_ref[2 * d + 1] = _split_bf16(kt_ref[d])

    row = lax.broadcasted_iota(jnp.int32, (FLAT, FLAT), 0)
    col = lax.broadcasted_iota(jnp.int32, (FLAT, FLAT), 1)
    same_ch = (row % GROUP_CH) == (col % GROUP_CH)
    row_blk = row // GROUP_CH
    col_blk = col // GROUP_CH
    orow = lax.broadcasted_iota(jnp.int32, (GROUPS * GROUP_CH, FLAT), 0)
    ocol = lax.broadcasted_iota(jnp.int32, (GROUPS * GROUP_CH, FLAT), 1)
    skip_ch = (orow % GROUP_CH) == (ocol % GROUP_CH)
    ocol_blk = ocol // GROUP_CH
    d_col = d_ref[...]

    def toeplitz_body(sg, carry):
        sf = jnp.where(same_ch & (row_blk + sg == col_blk), 1.0, 0.0).astype(_BF16)
        sb = jnp.where(same_ch & (row_blk == col_blk + (CHUNK - 1) - sg), 1.0, 0.0).astype(_BF16)
        out = (_dot(khl_ref[0], sf) + _dot(khl_ref[1], sf)) + (_dot(khl_ref[2], sb) + _dot(khl_ref[3], sb))
        out = out + jnp.where(skip_ch & (ocol_blk == sg), d_col, 0.0)
        r0 = pl.multiple_of(sg * GROUP_CH, GROUP_CH)
        mu_ref[:, pl.ds(r0, GROUP_CH), :] = out.reshape(GROUPS, GROUP_CH, FLAT).astype(_BF16)
        return carry

    lax.fori_loop(0, CHUNK, toeplitz_body, 0)


def _s5_tables(a_re, a_im, log_dt, b_re, b_im, c_re, c_im, d_skip):
    f32 = _F32
    eye2 = jnp.eye(2, dtype=f32)

    def pair_blocks(re, im):
        v = jnp.stack([re, im]).astype(f32).reshape(2, 2, PAIRS, 2, GROUP_CH, 1, STATE)
        v = v * eye2[None, None, None, :, None, :, None]
        return v.reshape(2, 2, PAIRS, 2 * GROUP_CH, 2 * STATE)

    ldt = jnp.broadcast_to(log_dt.astype(f32)[..., None], (2, GROUPS, STATE))
    a_rows = jnp.stack([a_re.astype(f32), a_im.astype(f32), ldt]).reshape(3, 2, PAIRS, 1, 2 * STATE)
    args = (a_rows, pair_blocks(jnp.swapaxes(b_re, 2, 3), jnp.swapaxes(b_im, 2, 3)),
            pair_blocks(c_re, c_im), d_skip.astype(f32).reshape(D_SSM, 1))
    whole = lambda a: pl.BlockSpec(a.shape, lambda i, n=a.ndim: (0,) * n)
    out_shape = [
        jax.ShapeDtypeStruct((GROUPS, FLAT, FLAT), _BF16),
        jax.ShapeDtypeStruct((PAIRS, 2 * FLAT, PAIR_LANES), _BF16),
        jax.ShapeDtypeStruct((PAIRS, PAIR_LANES, 2 * FLAT), _BF16),
        jax.ShapeDtypeStruct((PAIRS, 1, PAIR_LANES), _F32),
        jax.ShapeDtypeStruct((PAIRS, 1, PAIR_LANES), _F32),
    ]
    return pl.pallas_call(
        _tables_kernel,
        grid=(1,),
        in_specs=[whole(a) for a in args],
        out_specs=[whole(s) for s in out_shape],
        out_shape=out_shape,
        scratch_shapes=[
            pltpu.VMEM((2, GROUPS * GROUP_CH, FLAT), _F32),
            pltpu.VMEM((4, GROUPS * GROUP_CH, FLAT), _BF16),
            pltpu.VMEM((4, 2 * FLAT, LANE), _F32),
            pltpu.VMEM((2, FLAT, LANE), _F32),
        ],
        compiler_params=pltpu.CompilerParams(
            dimension_semantics=("arbitrary",), vmem_limit_bytes=VMEM_LIMIT),
        name="s5_tables",
    )(*args)


def kernel(x, c, ctx, c_ctx, w_mod, b_mod, norm1_g, norm2_g, w_in, s5_a_re, s5_a_im, s5_log_dt,
           s5_b_re, s5_b_im, s5_c_re, s5_c_im, s5_d, w_glu, b_glu, pool_w, pool_scale,
           w_branch_a, w_branch_b, w_out, w_ffn_in, w_ffn_out, final_norm_g):
    bsz, n_tok, d = x.shape
    ctx_len = ctx.shape[1]
    assert d == D_MODEL and w_mod.shape[0] == 1 and bsz + 1 <= SUBLANES
    assert n_tok % SCAN_ROWS == 0 and n_tok % MIX_ROWS == 0 and MIX_ROWS % GRID_W == 0
    assert bsz * ctx_len <= SCAN_ROWS

    cc_t = jnp.concatenate(
        [c.T, c_ctx[:, None], jnp.zeros((D_MODEL, SUBLANES - bsz - 1), _F32)], axis=1)
    mod3 = _modulation(cc_t, w_mod, b_mod, bsz + 1)

    mu, wst, mv, a16_re, a16_im = _s5_tables(
        s5_a_re[0], s5_a_im[0], s5_log_dt[0], s5_b_re[0], s5_b_im[0], s5_c_re[0], s5_c_im[0], s5_d[0])

    seed_f, seed_b = _pass1_ctx(ctx.reshape(1, bsz * ctx_len, D_MODEL), mod3, bsz, norm1_g, w_in, wst,
                                a16_re, a16_im, bsz)
    uflat, x_fwd, s_bwd = _pass1(x, mod3, norm1_g, w_in, wst, a16_re, a16_im, seed_f)
    ypre = _readout(uflat, x_fwd, s_bwd, seed_b, a16_re, a16_im, mu, mv)

    x1 = _mixer(x, ypre, mod3, norm1_g, b_glu, pool_scale, w_in, w_glu, w_branch_a, pool_w, w_branch_b, w_out)
    return _ffn(x1, mod3, norm2_g, final_norm_g.reshape(1, D_MODEL), w_ffn_in, w_ffn_out)
```

```python
import functools

import numpy as np
import jax
import jax.numpy as jnp
from jax import lax
from jax.experimental import pallas as pl
from jax.experimental.pallas import tpu as pltpu

_F32 = jnp.float32
_BF16 = jnp.bfloat16

D_MODEL = 1024
D_SSM = 512
D_POOL = 512
GROUPS = 32
STATE = 64
GROUP_CH = 16
CHUNK = 16
FLAT = CHUNK * GROUP_CH
PAIRS = GROUPS // 2
PAIR_LANES = 4 * 2 * STATE
LANE = 128
SUBLANES = 8
CHUNK_PITCH = 24
COL_BLOCKS = D_SSM // LANE
GRID_W = 64
POOL_WINDOWS = (2, 4, 8, 16)
POOL_GROUP_CH = D_POOL // len(POOL_WINDOWS)
FFN_HIDDEN = 2816
RMS_EPS = 1e-6

SCAN_TILE = 128
SCAN_ROWS = SCAN_TILE * CHUNK
NORM_ROWS = 512
MIX_ROWS = 1024
FFN_ROWS = 1024
FFN_CHUNKS = ((0, 1536), (1536, FFN_HIDDEN))
RELAYOUT_UNROLL = 16
WEIGHT_STAGE_ROWS = 64
WEIGHT_STAGE_SLOTS = 3
VMEM_LIMIT = 56 * 1024 * 1024


def _rms_mod(x, gain, sh):
    ms = jnp.mean(x * x, axis=-1, keepdims=True)
    return (x * lax.rsqrt(ms + RMS_EPS)) * gain + sh


def _dot(a, b):
    return jnp.dot(a, b, preferred_element_type=_F32)


def _const_spec(shape, index=None):
    index = (0,) * len(shape) if index is None else index
    return pl.BlockSpec(shape, lambda *_: index, pipeline_mode=pl.Buffered(1))


def _mod_kernel(ct_ref, w_ref, b_ref, o_ref, *, n_rows):
    ct = ct_ref[...]
    a = ct * jax.nn.sigmoid(ct)
    w = w_ref[...]
    b = b_ref[...]
    for r in range(n_rows):
        o_ref[r] = jnp.sum(w * a[:, r:r + 1], axis=0, keepdims=True) + b
    for r in range(n_rows, o_ref.shape[0]):
        o_ref[r] = jnp.zeros_like(b)


def _modulation(cc_t, w_mod, b_mod, n_rows):
    n_out = w_mod.shape[-1]
    blk = 1024
    return pl.pallas_call(
        functools.partial(_mod_kernel, n_rows=n_rows),
        grid=(n_out // blk,),
        in_specs=[
            pl.BlockSpec((D_MODEL, SUBLANES), lambda i: (0, 0)),
            pl.BlockSpec((None, D_MODEL, blk), lambda i: (0, 0, i)),
            pl.BlockSpec((1, blk), lambda i: (0, i)),
        ],
        out_specs=pl.BlockSpec((SUBLANES, 1, blk), lambda i: (0, 0, i)),
        out_shape=jax.ShapeDtypeStruct((SUBLANES, 1, n_out), _F32),
        compiler_params=pltpu.CompilerParams(
            dimension_semantics=("arbitrary",), vmem_limit_bytes=VMEM_LIMIT),
        name="adaln_mod",
    )(cc_t, w_mod, b_mod)


def _to_pitch(v):
    n = v.shape[0] // CHUNK
    v = v.reshape(n, CHUNK, v.shape[1])
    pad = jnp.zeros((n, CHUNK_PITCH - CHUNK, v.shape[2]), v.dtype)
    return jnp.concatenate([v, pad], axis=1).reshape(n * CHUNK_PITCH, v.shape[2])


def _from_pitch(v):
    n = v.shape[0] // CHUNK_PITCH
    return v.reshape(n, CHUNK_PITCH, v.shape[1])[:, :CHUNK, :].reshape(n * CHUNK, v.shape[1])


def _scan_constants(ar_ref, ai_ref, c_ref, dirs):
    rb = SUBLANES
    rid = lax.broadcasted_iota(jnp.int32, (rb, LANE), 0)

    def body(q, carry):
        for d in dirs:
            o = d * 2 * LANE
            a_r = ar_ref[q][:, o:o + LANE]
            a_i = ai_ref[q][:, o:o + LANE]
            pows = [(a_r, a_i)]
            for _ in range(rb - 1):
                pows.append(_cmul(pows[-1][0], pows[-1][1], a_r, a_i))
            for k, shift in enumerate((1, 2, 4)):
                keep = (rid >= shift) if d == 0 else (rid < rb - shift)
                c_ref[d, q, 2 * k] = jnp.where(keep, pows[shift - 1][0], 0.0)
                c_ref[d, q, 2 * k + 1] = jnp.where(keep, pows[shift - 1][1], 0.0)
            p_r = jnp.zeros((rb, LANE), _F32)
            p_i = jnp.zeros((rb, LANE), _F32)
            for r in range(rb):
                e = r if d == 0 else rb - 1 - r
                p_r = jnp.where(rid == r, pows[e][0], p_r)
                p_i = jnp.where(rid == r, pows[e][1], p_i)
            c_ref[d, q, 6] = p_r
            c_ref[d, q, 7] = p_i
        return carry

    lax.fori_loop(0, PAIRS, body, 0)


def _scan_block(c_ref, d, q, s_re, s_im, xin_re, xin_im):
    rb = SUBLANES
    rid = lax.broadcasted_iota(jnp.int32, (rb, LANE), 0)

    def shifted(v, k):
        return pltpu.roll(v, k if d == 0 else rb - k, 0)

    t_re, t_im = s_re, s_im
    for k in range(3):
        a_r = c_ref[d, q, 2 * k]
        a_i = c_ref[d, q, 2 * k + 1]
        u_re, u_im = shifted(t_re, 1 << k), shifted(t_im, 1 << k)
        t_re, t_im = t_re + (a_r * u_re - a_i * u_im), t_im + (a_r * u_im + a_i * u_re)
    p_r = c_ref[d, q, 6]
    p_i = c_ref[d, q, 7]
    after_re = t_re + (p_r * xin_re - p_i * xin_im)
    after_im = t_im + (p_r * xin_im + p_i * xin_re)
    first = 0 if d == 0 else rb - 1
    last = rb - 1 - first
    start_re = jnp.where(rid == first, xin_re, shifted(after_re, 1))
    start_im = jnp.where(rid == first, xin_im, shifted(after_im, 1))
    return start_re, start_im, after_re[last:last + 1], after_im[last:last + 1]


def _scan_rows(c_ref, d, q, s, xin, blocks):
    rb = SUBLANES
    x_re, x_im = xin[:, :LANE], xin[:, LANE:]
    starts = {}
    for blk in (blocks if d == 0 else blocks[::-1]):
        r = slice(blk * rb, (blk + 1) * rb)
        st_re, st_im, x_re, x_im = _scan_block(c_ref, d, q, s[r, :LANE], s[r, LANE:], x_re, x_im)
        starts[blk] = (st_re, st_im)
    return starts, jnp.concatenate([x_re, x_im], axis=1)


def _starts_to_rows(starts, blocks):
    return jnp.concatenate(
        [jnp.concatenate([starts[b][0] for b in blocks], axis=0),
         jnp.concatenate([starts[b][1] for b in blocks], axis=0)], axis=1)


def _p1_project(x_ref, sh_ref, sc_ref, g_ref, wa_ref, h_ref, u_ref, ut_ref, rows):
    gain = g_ref[...] * (1.0 + sc_ref[0])
    sh = sh_ref[0]
    wa = wa_ref[...].astype(_BF16)
    block_chunks = NORM_ROWS // CHUNK

    def norm_block(i):
        r = slice(i * NORM_ROWS, (i + 1) * NORM_ROWS)
        h_ref[r, :] = _rms_mod(x_ref[0, r, :], gain, sh).astype(_BF16)

    norm_block(0)
    for i in range(rows // NORM_ROWS):
        if (i + 1) * NORM_ROWS < rows:
            norm_block(i + 1)
        u = _dot(h_ref[i * NORM_ROWS:(i + 1) * NORM_ROWS, :], wa)
        pr = slice(i * block_chunks * CHUNK_PITCH, (i + 1) * block_chunks * CHUNK_PITCH)
        for cb in range(COL_BLOCKS):
            u_ref[cb, pr, :] = _to_pitch(u[:, cb * LANE:(cb + 1) * LANE])
    if rows < SCAN_ROWS:
        first = rows // CHUNK * CHUNK_PITCH
        for cb in range(COL_BLOCKS):
            u_ref[cb, first:, :] = jnp.zeros((SCAN_TILE * CHUNK_PITCH - first, LANE), _F32)

    def slab_body(sg, carry):
        r0 = pl.multiple_of(sg * GROUP_CH, GROUP_CH)
        for cb in range(COL_BLOCKS):
            slab = u_ref[cb, pl.ds(sg, SCAN_TILE, stride=CHUNK_PITCH), :]
            ut_ref[cb * 8:(cb + 1) * 8, pl.ds(r0, GROUP_CH), :] = (
                slab.astype(_BF16).T.reshape(8, GROUP_CH, SCAN_TILE))
        return carry

    lax.fori_loop(0, CHUNK, slab_body, 0, unroll=RELAYOUT_UNROLL)


def _pair_states(ut_ref, wst_ref, q):
    uf0 = ut_ref[2 * q].T
    uf1 = ut_ref[2 * q + 1].T
    return uf0, uf1, _dot(uf0, wst_ref[q, :FLAT, :]) + _dot(uf1, wst_ref[q, FLAT:, :])


def _p1_kernel(x_ref, sh_ref, sc_ref, g_ref, wa_ref, wst_ref, ar_ref, ai_ref, seed_ref,
               uflat_ref, xf_ref, sb_ref, h_ref, u_ref, ut_ref, c_ref, carry_ref):
    @pl.when(_first_step())
    def _constants():
        _scan_constants(ar_ref, ai_ref, c_ref, (0,))

    @pl.when(pl.program_id(1) == 0)
    def _seed():
        carry_ref[...] = seed_ref[0]

    _p1_project(x_ref, sh_ref, sc_ref, g_ref, wa_ref, h_ref, u_ref, ut_ref, SCAN_ROWS)
    half = PAIR_LANES // 2
    blocks = list(range(SCAN_TILE // SUBLANES))

    def pair_body(q, carry):
        uf0, uf1, s = _pair_states(ut_ref, wst_ref, q)
        uflat_ref[0, 2 * q] = uf0
        uflat_ref[0, 2 * q + 1] = uf1
        starts, x_out = _scan_rows(c_ref, 0, q, s[:, :half], carry_ref[q], blocks)
        xf_ref[0, q] = _starts_to_rows(starts, blocks).astype(_BF16)
        carry_ref[q] = x_out
        sb_ref[0, q] = s[:, half:]
        return carry

    lax.fori_loop(0, PAIRS, pair_body, 0, unroll=RELAYOUT_UNROLL)


def _p1_ctx_kernel(x_ref, sh_ref, sc_ref, g_ref, wa_ref, wst_ref, ar_ref, ai_ref,
                   seedf_ref, seedb_ref, h_ref, u_ref, ut_ref, c_ref, *, rows, bsz):
    _scan_constants(ar_ref, ai_ref, c_ref, (0, 1))
    _p1_project(x_ref, sh_ref, sc_ref, g_ref, wa_ref, h_ref, u_ref, ut_ref, rows)
    half = PAIR_LANES // 2
    per_seq = rows // CHUNK // SUBLANES // bsz
    zero = jnp.zeros((1, half), _F32)

    def pair_body(q, carry):
        _, _, s = _pair_states(ut_ref, wst_ref, q)
        for b in range(bsz):
            blocks = list(range(b * per_seq, (b + 1) * per_seq))
            _, seedf_ref[b, q] = _scan_rows(c_ref, 0, q, s[:, :half], zero, blocks)
            _, seedb_ref[b, q] = _scan_rows(c_ref, 1, q, s[:, half:], zero, blocks)
        return carry

    lax.fori_loop(0, PAIRS, pair_body, 0, unroll=RELAYOUT_UNROLL)


def _p1_scratch():
    return [
        pltpu.VMEM((SCAN_ROWS, D_MODEL), _BF16),
        pltpu.VMEM((COL_BLOCKS, SCAN_TILE * CHUNK_PITCH, LANE), _F32),
        pltpu.VMEM((GROUPS, FLAT, SCAN_TILE), _BF16),
        pltpu.VMEM((2, PAIRS, 8, SUBLANES, LANE), _F32),
    ]


def _pass1_ctx(ctx_rows, mod3, mod_row, norm_g, w_in, wst, a16_re, a16_im, bsz):
    rows = ctx_rows.shape[1]
    assert rows % NORM_ROWS == 0 and rows <= SCAN_ROWS and rows % (bsz * CHUNK * SUBLANES) == 0
    half = PAIR_LANES // 2
    seed = jax.ShapeDtypeStruct((bsz, PAIRS, 1, half), _F32)
    return pl.pallas_call(
        functools.partial(_p1_ctx_kernel, rows=rows, bsz=bsz),
        grid=(1, 1),
        in_specs=[
            pl.BlockSpec((1, rows, D_MODEL), lambda b, t: (0, 0, 0)),
            pl.BlockSpec((1, 1, D_MODEL), lambda b, t: (mod_row, 0, 0)),
            pl.BlockSpec((1, 1, D_MODEL), lambda b, t: (mod_row, 0, 1)),
            _const_spec((1, D_MODEL)),
            _const_spec((None, D_MODEL, D_SSM)),
            _const_spec((PAIRS, 2 * FLAT, PAIR_LANES)),
            _const_spec((PAIRS, 1, PAIR_LANES)),
            _const_spec((PAIRS, 1, PAIR_LANES)),
        ],
        out_specs=[pl.BlockSpec((bsz, PAIRS, 1, half), lambda b, t: (0, 0, 0, 0))] * 2,
        out_shape=[seed, seed],
        scratch_shapes=_p1_scratch(),
        compiler_params=pltpu.CompilerParams(
            dimension_semantics=("arbitrary", "arbitrary"), vmem_limit_bytes=VMEM_LIMIT),
        name="s5_context_states",
    )(ctx_rows, mod3, mod3, norm_g, w_in, wst, a16_re, a16_im)


def _pass1(x, mod3, norm_g, w_in, wst, a16_re, a16_im, seed_f):
    bsz, n_tok, _ = x.shape
    assert n_tok % SCAN_ROWS == 0
    nt = n_tok // SCAN_ROWS
    n_chunks = nt * SCAN_TILE
    half = PAIR_LANES // 2
    return pl.pallas_call(
        _p1_kernel,
        grid=(bsz, nt),
        in_specs=[
            pl.BlockSpec((1, SCAN_ROWS, D_MODEL), lambda b, t: (b, t, 0)),
            pl.BlockSpec((1, 1, D_MODEL), lambda b, t: (b, 0, 0)),
            pl.BlockSpec((1, 1, D_MODEL), lambda b, t: (b, 0, 1)),
            _const_spec((1, D_MODEL)),
            _const_spec((None, D_MODEL, D_SSM)),
            _const_spec((PAIRS, 2 * FLAT, PAIR_LANES)),
            _const_spec((PAIRS, 1, PAIR_LANES)),
            _const_spec((PAIRS, 1, PAIR_LANES)),
            pl.BlockSpec((1, PAIRS, 1, half), lambda b, t: (b, 0, 0, 0)),
        ],
        out_specs=[
            pl.BlockSpec((1, GROUPS, SCAN_TILE, FLAT), lambda b, t: (b, 0, t, 0)),
            pl.BlockSpec((1, PAIRS, SCAN_TILE, half), lambda b, t: (b, 0, t, 0)),
            pl.BlockSpec((1, PAIRS, SCAN_TILE, half), lambda b, t: (b, 0, t, 0)),
        ],
        out_shape=[
            jax.ShapeDtypeStruct((bsz, GROUPS, n_chunks, FLAT), _BF16),
            jax.ShapeDtypeStruct((bsz, PAIRS, n_chunks, half), _BF16),
            jax.ShapeDtypeStruct((bsz, PAIRS, n_chunks, half), _F32),
        ],
        scratch_shapes=_p1_scratch() + [pltpu.VMEM((PAIRS, 1, half), _F32)],
        compiler_params=pltpu.CompilerParams(
            dimension_semantics=("arbitrary", "arbitrary"), vmem_limit_bytes=VMEM_LIMIT),
        name="s5_chunk_states",
    )(x, mod3, mod3, norm_g, w_in, wst, a16_re, a16_im, seed_f)


def _readout_kernel(uflat_ref, xf_ref, sb_ref, seed_ref, ar_ref, ai_ref, mu_ref, mv_ref, y_ref,
                    yt_ref, ys_ref, c_ref, carry_ref):
    @pl.when(_first_step())
    def _constants():
        _scan_constants(ar_ref, ai_ref, c_ref, (1,))

    @pl.when(pl.program_id(1) == 0)
    def _seed():
        carry_ref[...] = seed_ref[0]

    blocks = list(range(SCAN_TILE // SUBLANES))

    def pair_body(q, carry):
        y0 = _dot(uflat_ref[0, 2 * q], mu_ref[2 * q])
        y1 = _dot(uflat_ref[0, 2 * q + 1], mu_ref[2 * q + 1])
        starts, x_out = _scan_rows(c_ref, 1, q, sb_ref[0, q], carry_ref[q], blocks)
        carry_ref[q] = x_out
        xs = jnp.concatenate([xf_ref[0, q], _starts_to_rows(starts, blocks).astype(_BF16)], axis=1)
        yx = _dot(xs, mv_ref[q])
        y = jnp.concatenate([y0, y1], axis=1) + yx
        yt = y.astype(yt_ref.dtype).T
        yt_ref[2 * q] = yt[:FLAT]
        yt_ref[2 * q + 1] = yt[FLAT:]
        return carry

    lax.fori_loop(0, PAIRS, pair_body, 0, unroll=RELAYOUT_UNROLL)

    def slab_body(sg, carry):
        r0 = pl.multiple_of(sg * GROUP_CH, GROUP_CH)
        for cb in range(COL_BLOCKS):
            yt = yt_ref[cb * 8:(cb + 1) * 8, pl.ds(r0, GROUP_CH), :].reshape(LANE, SCAN_TILE)
            ys_ref[cb, pl.ds(sg, SCAN_TILE, stride=CHUNK_PITCH), :] = yt.T.astype(_F32)
        return carry

    lax.fori_loop(0, CHUNK, slab_body, 0, unroll=RELAYOUT_UNROLL)
    for cb in range(COL_BLOCKS):
        y_ref[0, :, cb * LANE:(cb + 1) * LANE] = _from_pitch(ys_ref[cb]).astype(y_ref.dtype)


def _readout(uflat, xf, sb, seed_b, a16_re, a16_im, mu, mv):
    bsz, _, n_chunks, _ = uflat.shape
    nt = n_chunks // SCAN_TILE
    half = PAIR_LANES // 2
    rev = lambda b, t: (b, 0, nt - 1 - t, 0)
    return pl.pallas_call(
        _readout_kernel,
        grid=(bsz, nt),
        in_specs=[
            pl.BlockSpec((1, GROUPS, SCAN_TILE, FLAT), rev),
            pl.BlockSpec((1, PAIRS, SCAN_TILE, half), rev),
            pl.BlockSpec((1, PAIRS, SCAN_TILE, half), rev),
            pl.BlockSpec((1, PAIRS, 1, half), lambda b, t: (b, 0, 0, 0)),
            _const_spec((PAIRS, 1, PAIR_LANES)),
            _const_spec((PAIRS, 1, PAIR_LANES)),
            _const_spec((GROUPS, FLAT, FLAT)),
            _const_spec((PAIRS, PAIR_LANES, 2 * FLAT)),
        ],
        out_specs=pl.BlockSpec((1, SCAN_ROWS, D_SSM), lambda b, t: (b, nt - 1 - t, 0)),
        out_shape=jax.ShapeDtypeStruct((bsz, n_chunks * CHUNK, D_SSM), _BF16),
        scratch_shapes=[
            pltpu.VMEM((GROUPS, FLAT, SCAN_TILE), _BF16),
            pltpu.VMEM((COL_BLOCKS, SCAN_TILE * CHUNK_PITCH, LANE), _F32),
            pltpu.VMEM((2, PAIRS, 8, SUBLANES, LANE), _F32),
            pltpu.VMEM((PAIRS, 1, half), _F32),
        ],
        compiler_params=pltpu.CompilerParams(
            dimension_semantics=("arbitrary", "arbitrary"), vmem_limit_bytes=VMEM_LIMIT),
        name="s5_readout",
    )(uflat, xf, sb, seed_b, a16_re, a16_im, mu, mv)


def _window_sum(u, w):
    assert w // 2 <= SUBLANES
    rows, lanes = u.shape
    nb = rows // GRID_W
    pad = jnp.zeros((nb, SUBLANES, lanes), _F32)
    z = jnp.concatenate([pad, u.reshape(nb, GRID_W, lanes), pad], axis=1)
    n = nb * (GRID_W + 2 * SUBLANES)
    z = z.reshape(n, lanes)
    acc = z + pltpu.roll(z, 1, 0)
    m = 2
    while m < w:
        acc = pltpu.roll(acc, m // 2, 0) + pltpu.roll(acc, n - m // 2, 0)
        m *= 2
    return acc.reshape(nb, GRID_W + 2 * SUBLANES, lanes)[:, SUBLANES:SUBLANES + GRID_W, :].reshape(rows, lanes)


def _mix_kernel(x_ref, y_ref, sh_ref, sc_ref, gt_ref, g_ref, bglu_ref, pinv_ref, ps_ref, pw_ref,
                win_hbm, wglu_hbm, wa_hbm, wb_hbm, wo_hbm, o_ref,
                wr_ref, wglu_ref, wa_ref, wb_ref, wo_ref, stage_ref, sem):
    @pl.when(_first_step())
    def _load_weights():
        r = stage_ref.shape[1]
        n_rest = D_POOL + 2 * D_MODEL

        def pieces(src3, dst, n_rows, c0, n_cols):
            return [(src3.at[0, pl.ds(k * r, r), pl.ds(c0, n_cols)],
                     [(dst.at[pl.ds(k * r, r), :], slice(0, n_cols))]) for k in range(n_rows // r)]

        jobs = pieces(win_hbm, wr_ref, D_MODEL, D_SSM, n_rest)
        jobs += pieces(wglu_hbm, wglu_ref, D_SSM, 0, D_SSM)
        jobs += pieces(wa_hbm, wa_ref, D_SSM, 0, D_MODEL)
        jobs += pieces(wb_hbm, wb_ref, D_POOL, 0, D_MODEL)
        jobs += pieces(wo_hbm, wo_ref, D_MODEL, 0, D_MODEL)
        _stream_cast(jobs, stage_ref, sem)

    x = x_ref[0]
    h = _rms_mod(x, g_ref[...] * (1.0 + sc_ref[0]), sh_ref[0]).astype(_BF16)
    c_pool, c_ga, c_gb = 0, D_POOL, D_POOL + D_MODEL
    windows = range(len(POOL_WINDOWS))
    group = lambda wi: slice(wi * POOL_GROUP_CH, (wi + 1) * POOL_GROUP_CH)

    ub = _dot(h, wr_ref[:, c_pool:c_ga])
    y = jax.nn.gelu(y_ref[0].astype(_F32))
    glu = _dot(y.astype(_BF16), wglu_ref[...])
    wsums = [_window_sum(ub[:, group(wi)], POOL_WINDOWS[wi]) for wi in windows]
    gate_a = _dot(h, wr_ref[:, c_ga:c_gb])
    z = y * jax.nn.sigmoid(glu + bglu_ref[...])
    ya = _dot(z.astype(_BF16), wa_ref[...])
    outs = [_dot((wsums[wi] * pinv_ref[wi] - ub[:, group(wi)]).astype(_BF16), pw_ref[wi])
            for wi in windows]
    gate_b = _dot(h, wr_ref[:, c_gb:])
    pb = jnp.concatenate(outs, axis=1) * ps_ref[...]
    yb = _dot(pb.astype(_BF16), wb_ref[...])

    merged = jax.nn.sigmoid(gate_a) * ya + jax.nn.sigmoid(gate_b) * yb
    mixed = _dot(merged.astype(_BF16), wo_ref[...])
    o_ref[0] = x + gt_ref[0] * mixed


def _pool_inverse_counts(rows):
    pos = np.arange(rows) % GRID_W
    invs = []
    for w in POOL_WINDOWS:
        lo = np.clip(pos - w // 2, 0, GRID_W - 1)
        hi = np.clip(pos + w - 1 - w // 2, 0, GRID_W - 1) + 1
        invs.append(np.broadcast_to((1.0 / (hi - lo).astype(np.float32))[:, None], (rows, POOL_GROUP_CH)))
    return np.stack(invs)


def _mixer(x, ypre, mod3, norm_g, b_glu, pool_scale, w_in, w_glu, w_a, pool_w, w_b, w_out):
    bsz, n_tok, _ = x.shape
    tm = MIX_ROWS
    pinv = jnp.asarray(_pool_inverse_counts(tm), _F32)
    nw = len(POOL_WINDOWS)
    n_rest = D_POOL + 2 * D_MODEL
    hbm = pl.BlockSpec(memory_space=pl.ANY)
    return pl.pallas_call(
        _mix_kernel,
        grid=(bsz, n_tok // tm),
        in_specs=[
            pl.BlockSpec((1, tm, D_MODEL), lambda b, t: (b, t, 0)),
            pl.BlockSpec((1, tm, D_SSM), lambda b, t: (b, t, 0)),
            pl.BlockSpec((1, 1, D_MODEL), lambda b, t: (b, 0, 0)),
            pl.BlockSpec((1, 1, D_MODEL), lambda b, t: (b, 0, 1)),
            pl.BlockSpec((1, 1, D_MODEL), lambda b, t: (b, 0, 2)),
            _const_spec((1, D_MODEL)),
            _const_spec((1, D_SSM)),
            _const_spec((nw, tm, POOL_GROUP_CH)),
            _const_spec((1, D_POOL)),
            _const_spec((None, nw, POOL_GROUP_CH, POOL_GROUP_CH)),
            hbm, hbm, hbm, hbm, hbm,
        ],
        out_specs=pl.BlockSpec((1, tm, D_MODEL), lambda b, t: (b, t, 0)),
        out_shape=jax.ShapeDtypeStruct(x.shape, _F32),
        scratch_shapes=[
            pltpu.VMEM((D_MODEL, n_rest), _BF16),
            pltpu.VMEM((D_SSM, D_SSM), _BF16),
            pltpu.VMEM((D_SSM, D_MODEL), _BF16),
            pltpu.VMEM((D_POOL, D_MODEL), _BF16),
            pltpu.VMEM((D_MODEL, D_MODEL), _BF16),
            pltpu.VMEM((WEIGHT_STAGE_SLOTS, 4 * WEIGHT_STAGE_ROWS, n_rest), _F32),
            pltpu.SemaphoreType.DMA((WEIGHT_STAGE_SLOTS,)),
        ],
        compiler_params=pltpu.CompilerParams(
            dimension_semantics=("arbitrary", "arbitrary"), vmem_limit_bytes=VMEM_LIMIT),
        name="token_mixer",
    )(x, ypre, mod3, mod3, mod3, norm_g, b_glu, pinv, pool_scale, pool_w.astype(_BF16),
      w_in, w_glu, w_a, w_b, w_out)


def _stream_cast(jobs, stage_ref, sem):
    n_slots = stage_ref.shape[0]

    def copy(i):
        src = jobs[i][0]
        rows, cols = src.shape
        slot = i % n_slots
        return pltpu.make_async_copy(src, stage_ref.at[slot, pl.ds(0, rows), pl.ds(0, cols)], sem.at[slot])

    for i in range(min(n_slots - 1, len(jobs))):
        copy(i).start()
    for i, (src, dsts) in enumerate(jobs):
        if i + n_slots - 1 < len(jobs):
            copy(i + n_slots - 1).start()
        copy(i).wait()
        rows = src.shape[0]
        for dst, cols in dsts:
            dst[...] = stage_ref[i % n_slots, 0:rows, cols].astype(_BF16)


def _first_step():
    return jnp.logical_and(pl.program_id(0) == 0, pl.program_id(1) == 0)


def _ffn_kernel(x_ref, sh_ref, sc_ref, gt_ref, g2_ref, gf_ref, win_hbm, wout_hbm, o_ref,
                wg_ref, wu_ref, wo_ref, stage_in_ref, stage_out_ref, sem_in, sem_out):
    @pl.when(_first_step())
    def _load_weights():
        r = stage_in_ref.shape[1]
        _stream_cast([(win_hbm.at[0, pl.ds(k * r, r), :],
                       [(wg_ref.at[pl.ds(k * r, r), :], slice(0, FFN_HIDDEN)),
                        (wu_ref.at[pl.ds(k * r, r), :], slice(FFN_HIDDEN, 2 * FFN_HIDDEN))])
                      for k in range(D_MODEL // r)], stage_in_ref, sem_in)
        r = stage_out_ref.shape[1]
        _stream_cast([(wout_hbm.at[0, pl.ds(k * r, r), :], [(wo_ref.at[pl.ds(k * r, r), :], slice(0, D_MODEL))])
                      for k in range(FFN_HIDDEN // r)], stage_out_ref, sem_out)

    x = x_ref[0]
    h = _rms_mod(x, g2_ref[...] * (1.0 + sc_ref[0]), sh_ref[0]).astype(_BF16)
    down = None
    for c0, c1 in FFN_CHUNKS:
        gate = _dot(h, wg_ref[:, c0:c1])
        up = _dot(h, wu_ref[:, c0:c1])
        part = _dot((gate * jax.nn.sigmoid(gate) * up).astype(_BF16), wo_ref[c0:c1, :])
        down = part if down is None else down + part
    y = x + gt_ref[0] * down
    ms = jnp.mean(y * y, axis=-1, keepdims=True)
    o_ref[0] = (y * lax.rsqrt(ms + RMS_EPS)) * gf_ref[...]


def _ffn(x1, mod3, norm2_g, final_g, w_ffn, w_down):
    bsz, n_tok, _ = x1.shape
    tm = FFN_ROWS
    hbm = pl.BlockSpec(memory_space=pl.ANY)
    return pl.pallas_call(
        _ffn_kernel,
        grid=(bsz, n_tok // tm),
        in_specs=[
            pl.BlockSpec((1, tm, D_MODEL), lambda b, t: (b, t, 0)),
            pl.BlockSpec((1, 1, D_MODEL), lambda b, t: (b, 0, 3)),
            pl.BlockSpec((1, 1, D_MODEL), lambda b, t: (b, 0, 4)),
            pl.BlockSpec((1, 1, D_MODEL), lambda b, t: (b, 0, 5)),
            _const_spec((1, D_MODEL)),
            _const_spec((1, D_MODEL)),
            hbm,
            hbm,
        ],
        out_specs=pl.BlockSpec((1, tm, D_MODEL), lambda b, t: (b, t, 0)),
        out_shape=jax.ShapeDtypeStruct(x1.shape, _F32),
        scratch_shapes=[
            pltpu.VMEM((D_MODEL, FFN_HIDDEN), _BF16),
            pltpu.VMEM((D_MODEL, FFN_HIDDEN), _BF16),
            pltpu.VMEM((FFN_HIDDEN, D_MODEL), _BF16),
            pltpu.VMEM((WEIGHT_STAGE_SLOTS, WEIGHT_STAGE_ROWS, 2 * FFN_HIDDEN), _F32),
            pltpu.VMEM((WEIGHT_STAGE_SLOTS, FFN_HIDDEN // SUBLANES, D_MODEL), _F32),
            pltpu.SemaphoreType.DMA((WEIGHT_STAGE_SLOTS,)),
            pltpu.SemaphoreType.DMA((WEIGHT_STAGE_SLOTS,)),
        ],
        compiler_params=pltpu.CompilerParams(
            dimension_semantics=("arbitrary", "arbitrary"), vmem_limit_bytes=VMEM_LIMIT),
        name="swiglu_ffn",
    )(x1, mod3, mod3, mod3, norm2_g, final_g, w_ffn, w_down)


def _cmul(xr, xi, yr, yi):
    return xr * yr - xi * yi, xr * yi + xi * yr


def _split_bf16(v):
    hi = v.astype(_BF16)
    return hi, (v - hi.astype(_F32)).astype(_BF16)


def _dot3_nt(a, b):
    dims = (((1,), (1,)), ((), ()))
    nt = lambda p, q: lax.dot_general(p, q, dims, preferred_element_type=_F32)
    a_hi, a_lo = _split_bf16(a)
    b_hi, b_lo = _split_bf16(b)
    return nt(a_hi, b_hi) + (nt(a_hi, b_lo) + nt(a_lo, b_hi))


def _tables_kernel(a_ref, bt_ref, c_ref, d_ref,
                   mu_ref, wst_ref, mv_ref, a16r_ref, a16i_ref, kt_ref, khl_ref, vnat_ref, cp_ref):
    def pair_body(q, carry):
        for d in range(2):
            ar = a_ref[0, d, q]
            ai = a_ref[1, d, q]
            dt = jnp.exp(a_ref[2, d, q])
            mag = jnp.exp(ar * dt)
            ang = ai * dt
            abr, abi = mag * jnp.cos(ang), mag * jnp.sin(ang)
            den = ar * ar + ai * ai
            fr = ((abr - 1.0) * ar + abi * ai) / den
            fi = (abi * ar - (abr - 1.0) * ai) / den
            bbr, bbi = _cmul(bt_ref[0, d, q], bt_ref[1, d, q], fr, fi)
            cr = c_ref[0, d, q]
            ci = c_ref[1, d, q]
            pw = [(jnp.ones_like(ar), jnp.zeros_like(ar))]
            for _ in range(CHUNK):
                pw.append(_cmul(pw[-1][0], pw[-1][1], abr, abi))
            lanes_re = slice(2 * d * LANE, (2 * d + 1) * LANE)
            lanes_im = slice((2 * d + 1) * LANE, (2 * d + 2) * LANE)
            for lanes in (lanes_re, lanes_im):
                a16r_ref[q, :, lanes] = pw[CHUNK][0]
                a16i_ref[q, :, lanes] = pw[CHUNK][1]
            for sg in range(CHUNK):
                e = (CHUNK - 1 - sg) if d == 0 else sg
                wr, wi = _cmul(bbr, bbi, pw[e][0], pw[e][1])
                e = (sg + 1) if d == 0 else (CHUNK - sg)
                vr, vi = _cmul(cr, ci, pw[e][0], pw[e][1])
                for gg in range(2):
                    src = slice(gg * GROUP_CH, (gg + 1) * GROUP_CH)
                    dst = slice(gg * FLAT + sg * GROUP_CH, gg * FLAT + (sg + 1) * GROUP_CH)
                    wst_ref[q, dst, lanes_re] = wr[src].astype(_BF16)
                    wst_ref[q, dst, lanes_im] = wi[src].astype(_BF16)
                    vnat_ref[2 * d, dst, :] = vr[src]
                    vnat_ref[2 * d + 1, dst, :] = -vi[src]
            for gg in range(2):
                src = slice(gg * GROUP_CH, (gg + 1) * GROUP_CH)
                for k in range(CHUNK):
                    e = k if d == 0 else (CHUNK - 1 - k)
                    pr, pi = _cmul(cr[src], ci[src], pw[e][0], pw[e][1])
                    cp_ref[0, k * GROUP_CH:(k + 1) * GROUP_CH, :] = pr
                    cp_ref[1, k * GROUP_CH:(k + 1) * GROUP_CH, :] = pi
                kt = _dot3_nt(bbr[src], cp_ref[0]) - _dot3_nt(bbi[src], cp_ref[1])
                r0 = pl.multiple_of((2 * q + gg) * GROUP_CH, GROUP_CH)
                kt_ref[d, pl.ds(r0, GROUP_CH), :] = kt
        for part in range(4):
            mv_ref[q, part * LANE:(part + 1) * LANE, :] = vnat_ref[part].T.astype(_BF16)
        return carry

    lax.fori_loop(0, PAIRS, pair_body, 0, unroll=2)

    for d in range(2):
        khl_ref[2 * d], khl_ref[2 * d + 1] = _split_bf16(kt_ref[d])

    row = lax.broadcasted_iota(jnp.int32, (FLAT, FLAT), 0)
    col = lax.broadcasted_iota(jnp.int32, (FLAT, FLAT), 1)
    same_ch = (row % GROUP_CH) == (col % GROUP_CH)
    row_blk = row // GROUP_CH
    col_blk = col // GROUP_CH
    orow = lax.broadcasted_iota(jnp.int32, (GROUPS * GROUP_CH, FLAT), 0)
    ocol = lax.broadcasted_iota(jnp.int32, (GROUPS * GROUP_CH, FLAT), 1)
    skip_ch = (orow % GROUP_CH) == (ocol % GROUP_CH)
    ocol_blk = ocol // GROUP_CH
    d_col = d_ref[...]

    def toeplitz_body(sg, carry):
        sf = jnp.where(same_ch & (row_blk + sg == col_blk), 1.0, 0.0).astype(_BF16)
        sb = jnp.where(same_ch & (row_blk == col_blk + (CHUNK - 1) - sg), 1.0, 0.0).astype(_BF16)
        out = (_dot(khl_ref[0], sf) + _dot(khl_ref[1], sf)) + (_dot(khl_ref[2], sb) + _dot(khl_ref[3], sb))
        out = out + jnp.where(skip_ch & (ocol_blk == sg), d_col, 0.0)
        r0 = pl.multiple_of(sg * GROUP_CH, GROUP_CH)
        mu_ref[:, pl.ds(r0, GROUP_CH), :] = out.reshape(GROUPS, GROUP_CH, FLAT).astype(_BF16)
        return carry

    lax.fori_loop(0, CHUNK, toeplitz_body, 0, unroll=4)


def _s5_tables(a_re, a_im, log_dt, b_re, b_im, c_re, c_im, d_skip):
    f32 = _F32
    eye2 = jnp.eye(2, dtype=f32)

    def pair_blocks(re, im):
        v = jnp.stack([re, im]).astype(f32).reshape(2, 2, PAIRS, 2, GROUP_CH, 1, STATE)
        v = v * eye2[None, None, None, :, None, :, None]
        return v.reshape(2, 2, PAIRS, 2 * GROUP_CH, 2 * STATE)

    ldt = jnp.broadcast_to(log_dt.astype(f32)[..., None], (2, GROUPS, STATE))
    a_rows = jnp.stack([a_re.astype(f32), a_im.astype(f32), ldt]).reshape(3, 2, PAIRS, 1, 2 * STATE)
    args = (a_rows, pair_blocks(jnp.swapaxes(b_re, 2, 3), jnp.swapaxes(b_im, 2, 3)),
            pair_blocks(c_re, c_im), d_skip.astype(f32).reshape(D_SSM, 1))
    whole = lambda a: pl.BlockSpec(a.shape, lambda i, n=a.ndim: (0,) * n)
    out_shape = [
        jax.ShapeDtypeStruct((GROUPS, FLAT, FLAT), _BF16),
        jax.ShapeDtypeStruct((PAIRS, 2 * FLAT, PAIR_LANES), _BF16),
        jax.ShapeDtypeStruct((PAIRS, PAIR_LANES, 2 * FLAT), _BF16),
        jax.ShapeDtypeStruct((PAIRS, 1, PAIR_LANES), _F32),
        jax.ShapeDtypeStruct((PAIRS, 1, PAIR_LANES), _F32),
    ]
    return pl.pallas_call(
        _tables_kernel,
        grid=(1,),
        in_specs=[whole(a) for a in args],
        out_specs=[whole(s) for s in out_shape],
        out_shape=out_shape,
        scratch_shapes=[
            pltpu.VMEM((2, GROUPS * GROUP_CH, FLAT), _F32),
            pltpu.VMEM((4, GROUPS * GROUP_CH, FLAT), _BF16),
            pltpu.VMEM((4, 2 * FLAT, LANE), _F32),
            pltpu.VMEM((2, FLAT, LANE), _F32),
        ],
        compiler_params=pltpu.CompilerParams(
            dimension_semantics=("arbitrary",), vmem_limit_bytes=VMEM_LIMIT),
        name="s5_tables",
    )(*args)


def kernel(x, c, ctx, c_ctx, w_mod, b_mod, norm1_g, norm2_g, w_in, s5_a_re, s5_a_im, s5_log_dt,
           s5_b_re, s5_b_im, s5_c_re, s5_c_im, s5_d, w_glu, b_glu, pool_w, pool_scale,
           w_branch_a, w_branch_b, w_out, w_ffn_in, w_ffn_out, final_norm_g):
    bsz, n_tok, d = x.shape
    ctx_len = ctx.shape[1]
    assert d == D_MODEL and w_mod.shape[0] == 1 and bsz + 1 <= SUBLANES
    assert n_tok % SCAN_ROWS == 0 and n_tok % MIX_ROWS == 0 and MIX_ROWS % GRID_W == 0
    assert bsz * ctx_len <= SCAN_ROWS

    cc_t = jnp.concatenate(
        [c.T, c_ctx[:, None], jnp.zeros((D_MODEL, SUBLANES - bsz - 1), _F32)], axis=1)
    mod3 = _modulation(cc_t, w_mod, b_mod, bsz + 1)

    mu, wst, mv, a16_re, a16_im = _s5_tables(
        s5_a_re[0], s5_a_im[0], s5_log_dt[0], s5_b_re[0], s5_b_im[0], s5_c_re[0], s5_c_im[0], s5_d[0])

    seed_f, seed_b = _pass1_ctx(ctx.reshape(1, bsz * ctx_len, D_MODEL), mod3, bsz, norm1_g, w_in, wst,
                                a16_re, a16_im, bsz)
    uflat, x_fwd, s_bwd = _pass1(x, mod3, norm1_g, w_in, wst, a16_re, a16_im, seed_f)
    ypre = _readout(uflat, x_fwd, s_bwd, seed_b, a16_re, a16_im, mu, mv)

    x1 = _mixer(x, ypre, mod3, norm1_g, b_glu, pool_scale, w_in, w_glu, w_branch_a, pool_w, w_branch_b, w_out)
    return _ffn(x1, mod3, norm2_g, final_norm_g.reshape(1, D_MODEL), w_ffn_in, w_ffn_out)
```

```python
import functools

import numpy as np
import jax
import jax.numpy as jnp
from jax import lax
from jax.experimental import pallas as pl
from jax.experimental.pallas import tpu as pltpu

_F32 = jnp.float32
_BF16 = jnp.bfloat16

D_MODEL = 1024
D_SSM = 512
D_POOL = 512
GROUPS = 32
STATE = 64
GROUP_CH = 16
CHUNK = 16
FLAT = CHUNK * GROUP_CH
PAIRS = GROUPS // 2
PAIR_LANES = 4 * 2 * STATE
LANE = 128
SUBLANES = 8
CHUNK_PITCH = 24
COL_BLOCKS = D_SSM // LANE
GRID_W = 64
POOL_WINDOWS = (2, 4, 8, 16)
POOL_GROUP_CH = D_POOL // len(POOL_WINDOWS)
FFN_HIDDEN = 2816
RMS_EPS = 1e-6

SCAN_TILE = 128
SCAN_ROWS = SCAN_TILE * CHUNK
NORM_ROWS = 512
MIX_ROWS = 1024
FFN_ROWS = 1024
FFN_CHUNKS = ((0, 1536), (1536, FFN_HIDDEN))
RELAYOUT_UNROLL = 16
WEIGHT_STAGE_ROWS = 64
WEIGHT_STAGE_SLOTS = 3
VMEM_LIMIT = 56 * 1024 * 1024


def _rms_mod(x, gain, sh):
    ms = jnp.mean(x * x, axis=-1, keepdims=True)
    return (x * lax.rsqrt(ms + RMS_EPS)) * gain + sh


def _dot(a, b):
    return jnp.dot(a, b, preferred_element_type=_F32)


def _const_spec(shape, index=None):
    index = (0,) * len(shape) if index is None else index
    return pl.BlockSpec(shape, lambda *_: index, pipeline_mode=pl.Buffered(1))


def _to_pitch(v):
    n = v.shape[0] // CHUNK
    v = v.reshape(n, CHUNK, v.shape[1])
    pad = jnp.zeros((n, CHUNK_PITCH - CHUNK, v.shape[2]), v.dtype)
    return jnp.concatenate([v, pad], axis=1).reshape(n * CHUNK_PITCH, v.shape[2])


def _from_pitch(v):
    n = v.shape[0] // CHUNK_PITCH
    return v.reshape(n, CHUNK_PITCH, v.shape[1])[:, :CHUNK, :].reshape(n * CHUNK, v.shape[1])


def _scan_constants(ar_ref, ai_ref, c_ref, dirs):
    rb = SUBLANES
    rid = lax.broadcasted_iota(jnp.int32, (rb, LANE), 0)

    def body(q, carry):
        for d in dirs:
            o = d * 2 * LANE
            a_r = ar_ref[q][:, o:o + LANE]
            a_i = ai_ref[q][:, o:o + LANE]
            pows = [(a_r, a_i)]
            for _ in range(rb - 1):
                pows.append(_cmul(pows[-1][0], pows[-1][1], a_r, a_i))
            for k, shift in enumerate((1, 2, 4)):
                keep = (rid >= shift) if d == 0 else (rid < rb - shift)
                c_ref[d, q, 2 * k] = jnp.where(keep, pows[shift - 1][0], 0.0)
                c_ref[d, q, 2 * k + 1] = jnp.where(keep, pows[shift - 1][1], 0.0)
            p_r = jnp.zeros((rb, LANE), _F32)
            p_i = jnp.zeros((rb, LANE), _F32)
            for r in range(rb):
                e = r if d == 0 else rb - 1 - r
                p_r = jnp.where(rid == r, pows[e][0], p_r)
                p_i = jnp.where(rid == r, pows[e][1], p_i)
            c_ref[d, q, 6] = p_r
            c_ref[d, q, 7] = p_i
        return carry

    lax.fori_loop(0, PAIRS, body, 0)


def _scan_block(c_ref, d, q, s_re, s_im, xin_re, xin_im):
    rb = SUBLANES
    rid = lax.broadcasted_iota(jnp.int32, (rb, LANE), 0)

    def shifted(v, k):
        return pltpu.roll(v, k if d == 0 else rb - k, 0)

    t_re, t_im = s_re, s_im
    for k in range(3):
        a_r = c_ref[d, q, 2 * k]
        a_i = c_ref[d, q, 2 * k + 1]
        u_re, u_im = shifted(t_re, 1 << k), shifted(t_im, 1 << k)
        t_re, t_im = t_re + (a_r * u_re - a_i * u_im), t_im + (a_r * u_im + a_i * u_re)
    p_r = c_ref[d, q, 6]
    p_i = c_ref[d, q, 7]
    after_re = t_re + (p_r * xin_re - p_i * xin_im)
    after_im = t_im + (p_r * xin_im + p_i * xin_re)
    first = 0 if d == 0 else rb - 1
    last = rb - 1 - first
    start_re = jnp.where(rid == first, xin_re, shifted(after_re, 1))
    start_im = jnp.where(rid == first, xin_im, shifted(after_im, 1))
    return start_re, start_im, after_re[last:last + 1], after_im[last:last + 1]


def _scan_rows(c_ref, d, q, s, xin, blocks):
    rb = SUBLANES
    x_re, x_im = xin[:, :LANE], xin[:, LANE:]
    starts = {}
    for blk in (blocks if d == 0 else blocks[::-1]):
        r = slice(blk * rb, (blk + 1) * rb)
        st_re, st_im, x_re, x_im = _scan_block(c_ref, d, q, s[r, :LANE], s[r, LANE:], x_re, x_im)
        starts[blk] = (st_re, st_im)
    return starts, jnp.concatenate([x_re, x_im], axis=1)


def _starts_to_rows(starts, blocks):
    return jnp.concatenate(
        [jnp.concatenate([starts[b][0] for b in blocks], axis=0),
         jnp.concatenate([starts[b][1] for b in blocks], axis=0)], axis=1)


def _p1_project(x_ref, sh_ref, sc_ref, g_ref, wa_ref, h_ref, u_ref, ut_ref, rows):
    gain = g_ref[...] * (1.0 + sc_ref[0])
    sh = sh_ref[0]
    wa = wa_ref[...].astype(_BF16)
    block_chunks = NORM_ROWS // CHUNK

    def norm_block(i):
        r = slice(i * NORM_ROWS, (i + 1) * NORM_ROWS)
        h_ref[r, :] = _rms_mod(x_ref[0, r, :], gain, sh).astype(_BF16)

    norm_block(0)
    for i in range(rows // NORM_ROWS):
        if (i + 1) * NORM_ROWS < rows:
            norm_block(i + 1)
        u = _dot(h_ref[i * NORM_ROWS:(i + 1) * NORM_ROWS, :], wa)
        pr = slice(i * block_chunks * CHUNK_PITCH, (i + 1) * block_chunks * CHUNK_PITCH)
        for cb in range(COL_BLOCKS):
            u_ref[cb, pr, :] = _to_pitch(u[:, cb * LANE:(cb + 1) * LANE])
    if rows < SCAN_ROWS:
        first = rows // CHUNK * CHUNK_PITCH
        for cb in range(COL_BLOCKS):
            u_ref[cb, first:, :] = jnp.zeros((SCAN_TILE * CHUNK_PITCH - first, LANE), _F32)

    def slab_body(sg, carry):
        r0 = pl.multiple_of(sg * GROUP_CH, GROUP_CH)
        for cb in range(COL_BLOCKS):
            slab = u_ref[cb, pl.ds(sg, SCAN_TILE, stride=CHUNK_PITCH), :]
            ut_ref[cb * 8:(cb + 1) * 8, pl.ds(r0, GROUP_CH), :] = (
                slab.astype(_BF16).T.reshape(8, GROUP_CH, SCAN_TILE))
        return carry

    lax.fori_loop(0, CHUNK, slab_body, 0, unroll=RELAYOUT_UNROLL)


def _pair_states(ut_ref, wst_ref, q):
    uf0 = ut_ref[2 * q].T
    uf1 = ut_ref[2 * q + 1].T
    return uf0, uf1, _dot(uf0, wst_ref[q, :FLAT, :]) + _dot(uf1, wst_ref[q, FLAT:, :])


def _p1_kernel(x_ref, sh_ref, sc_ref, g_ref, wa_ref, wst_ref, ar_ref, ai_ref, seed_ref,
               uflat_ref, xf_ref, sb_ref, h_ref, u_ref, ut_ref, c_ref, carry_ref):
    @pl.when(_first_step())
    def _constants():
        _scan_constants(ar_ref, ai_ref, c_ref, (0,))

    @pl.when(pl.program_id(1) == 0)
    def _seed():
        carry_ref[...] = seed_ref[0]

    _p1_project(x_ref, sh_ref, sc_ref, g_ref, wa_ref, h_ref, u_ref, ut_ref, SCAN_ROWS)
    half = PAIR_LANES // 2
    blocks = list(range(SCAN_TILE // SUBLANES))

    def pair_body(q, carry):
        uf0, uf1, s = _pair_states(ut_ref, wst_ref, q)
        uflat_ref[0, 2 * q] = uf0
        uflat_ref[0, 2 * q + 1] = uf1
        starts, x_out = _scan_rows(c_ref, 0, q, s[:, :half], carry_ref[q], blocks)
        xf_ref[0, q] = _starts_to_rows(starts, blocks).astype(_BF16)
        carry_ref[q] = x_out
        sb_ref[0, q] = s[:, half:]
        return carry

    lax.fori_loop(0, PAIRS, pair_body, 0, unroll=RELAYOUT_UNROLL)


def _p1_ctx_kernel(x_ref, sh_ref, sc_ref, g_ref, wa_ref, wst_ref, ar_ref, ai_ref,
                   seedf_ref, seedb_ref, h_ref, u_ref, ut_ref, c_ref, *, rows, bsz):
    _scan_constants(ar_ref, ai_ref, c_ref, (0, 1))
    _p1_project(x_ref, sh_ref, sc_ref, g_ref, wa_ref, h_ref, u_ref, ut_ref, rows)
    half = PAIR_LANES // 2
    per_seq = rows // CHUNK // SUBLANES // bsz
    zero = jnp.zeros((1, half), _F32)

    def pair_body(q, carry):
        _, _, s = _pair_states(ut_ref, wst_ref, q)
        for b in range(bsz):
            blocks = list(range(b * per_seq, (b + 1) * per_seq))
            _, seedf_ref[b, q] = _scan_rows(c_ref, 0, q, s[:, :half], zero, blocks)
            _, seedb_ref[b, q] = _scan_rows(c_ref, 1, q, s[:, half:], zero, blocks)
        return carry

    lax.fori_loop(0, PAIRS, pair_body, 0, unroll=RELAYOUT_UNROLL)


def _p1_scratch():
    return [
        pltpu.VMEM((SCAN_ROWS, D_MODEL), _BF16),
        pltpu.VMEM((COL_BLOCKS, SCAN_TILE * CHUNK_PITCH, LANE), _F32),
        pltpu.VMEM((GROUPS, FLAT, SCAN_TILE), _BF16),
        pltpu.VMEM((2, PAIRS, 8, SUBLANES, LANE), _F32),
    ]


def _pass1_ctx(ctx_rows, mod3, mod_row, norm_g, w_in, wst, a16_re, a16_im, bsz):
    rows = ctx_rows.shape[1]
    assert rows % NORM_ROWS == 0 and rows <= SCAN_ROWS and rows % (bsz * CHUNK * SUBLANES) == 0
    half = PAIR_LANES // 2
    seed = jax.ShapeDtypeStruct((bsz, PAIRS, 1, half), _F32)
    return pl.pallas_call(
        functools.partial(_p1_ctx_kernel, rows=rows, bsz=bsz),
        grid=(1, 1),
        in_specs=[
            pl.BlockSpec((1, rows, D_MODEL), lambda b, t: (0, 0, 0)),
            pl.BlockSpec((1, 1, D_MODEL), lambda b, t: (mod_row, 0, 0)),
            pl.BlockSpec((1, 1, D_MODEL), lambda b, t: (mod_row, 0, 1)),
            _const_spec((1, D_MODEL)),
            _const_spec((None, D_MODEL, D_SSM)),
            _const_spec((PAIRS, 2 * FLAT, PAIR_LANES)),
            _const_spec((PAIRS, 1, PAIR_LANES)),
            _const_spec((PAIRS, 1, PAIR_LANES)),
        ],
        out_specs=[pl.BlockSpec((bsz, PAIRS, 1, half), lambda b, t: (0, 0, 0, 0))] * 2,
        out_shape=[seed, seed],
        scratch_shapes=_p1_scratch(),
        compiler_params=pltpu.CompilerParams(
            dimension_semantics=("arbitrary", "arbitrary"), vmem_limit_bytes=VMEM_LIMIT),
        name="s5_context_states",
    )(ctx_rows, mod3, mod3, norm_g, w_in, wst, a16_re, a16_im)


def _pass1(x, mod3, norm_g, w_in, wst, a16_re, a16_im, seed_f):
    bsz, n_tok, _ = x.shape
    assert n_tok % SCAN_ROWS == 0
    nt = n_tok // SCAN_ROWS
    n_chunks = nt * SCAN_TILE
    half = PAIR_LANES // 2
    return pl.pallas_call(
        _p1_kernel,
        grid=(bsz, nt),
        in_specs=[
            pl.BlockSpec((1, SCAN_ROWS, D_MODEL), lambda b, t: (b, t, 0)),
            pl.BlockSpec((1, 1, D_MODEL), lambda b, t: (b, 0, 0)),
            pl.BlockSpec((1, 1, D_MODEL), lambda b, t: (b, 0, 1)),
            _const_spec((1, D_MODEL)),
            _const_spec((None, D_MODEL, D_SSM)),
            _const_spec((PAIRS, 2 * FLAT, PAIR_LANES)),
            _const_spec((PAIRS, 1, PAIR_LANES)),
            _const_spec((PAIRS, 1, PAIR_LANES)),
            pl.BlockSpec((1, PAIRS, 1, half), lambda b, t: (b, 0, 0, 0)),
        ],
        out_specs=[
            pl.BlockSpec((1, GROUPS, SCAN_TILE, FLAT), lambda b, t: (b, 0, t, 0)),
            pl.BlockSpec((1, PAIRS, SCAN_TILE, half), lambda b, t: (b, 0, t, 0)),
            pl.BlockSpec((1, PAIRS, SCAN_TILE, half), lambda b, t: (b, 0, t, 0)),
        ],
        out_shape=[
            jax.ShapeDtypeStruct((bsz, GROUPS, n_chunks, FLAT), _BF16),
            jax.ShapeDtypeStruct((bsz, PAIRS, n_chunks, half), _BF16),
            jax.ShapeDtypeStruct((bsz, PAIRS, n_chunks, half), _F32),
        ],
        scratch_shapes=_p1_scratch() + [pltpu.VMEM((PAIRS, 1, half), _F32)],
        compiler_params=pltpu.CompilerParams(
            dimension_semantics=("arbitrary", "arbitrary"), vmem_limit_bytes=VMEM_LIMIT),
        name="s5_chunk_states",
    )(x, mod3, mod3, norm_g, w_in, wst, a16_re, a16_im, seed_f)


def _readout_kernel(uflat_ref, xf_ref, sb_ref, seed_ref, ar_ref, ai_ref, mu_ref, mv_ref, y_ref,
                    yt_ref, ys_ref, c_ref, carry_ref):
    @pl.when(_first_step())
    def _constants():
        _scan_constants(ar_ref, ai_ref, c_ref, (1,))

    @pl.when(pl.program_id(1) == 0)
    def _seed():
        carry_ref[...] = seed_ref[0]

    blocks = list(range(SCAN_TILE // SUBLANES))

    def pair_body(q, carry):
        y0 = _dot(uflat_ref[0, 2 * q], mu_ref[2 * q])
        y1 = _dot(uflat_ref[0, 2 * q + 1], mu_ref[2 * q + 1])
        starts, x_out = _scan_rows(c_ref, 1, q, sb_ref[0, q], carry_ref[q], blocks)
        carry_ref[q] = x_out
        xs = jnp.concatenate([xf_ref[0, q], _starts_to_rows(starts, blocks).astype(_BF16)], axis=1)
        yx = _dot(xs, mv_ref[q])
        y = jnp.concatenate([y0, y1], axis=1) + yx
        yt = y.astype(yt_ref.dtype).T
        yt_ref[2 * q] = yt[:FLAT]
        yt_ref[2 * q + 1] = yt[FLAT:]
        return carry

    lax.fori_loop(0, PAIRS, pair_body, 0, unroll=RELAYOUT_UNROLL)

    def slab_body(sg, carry):
        r0 = pl.multiple_of(sg * GROUP_CH, GROUP_CH)
        for cb in range(COL_BLOCKS):
            yt = yt_ref[cb * 8:(cb + 1) * 8, pl.ds(r0, GROUP_CH), :].reshape(LANE, SCAN_TILE)
            ys_ref[cb, pl.ds(sg, SCAN_TILE, stride=CHUNK_PITCH), :] = yt.T.astype(_F32)
        return carry

    lax.fori_loop(0, CHUNK, slab_body, 0, unroll=RELAYOUT_UNROLL)
    for cb in range(COL_BLOCKS):
        y_ref[0, :, cb * LANE:(cb + 1) * LANE] = _from_pitch(ys_ref[cb]).astype(y_ref.dtype)


def _readout(uflat, xf, sb, seed_b, a16_re, a16_im, mu, mv):
    bsz, _, n_chunks, _ = uflat.shape
    nt = n_chunks // SCAN_TILE
    half = PAIR_LANES // 2
    rev = lambda b, t: (b, 0, nt - 1 - t, 0)
    return pl.pallas_call(
        _readout_kernel,
        grid=(bsz, nt),
        in_specs=[
            pl.BlockSpec((1, GROUPS, SCAN_TILE, FLAT), rev),
            pl.BlockSpec((1, PAIRS, SCAN_TILE, half), rev),
            pl.BlockSpec((1, PAIRS, SCAN_TILE, half), rev),
            pl.BlockSpec((1, PAIRS, 1, half), lambda b, t: (b, 0, 0, 0)),
            _const_spec((PAIRS, 1, PAIR_LANES)),
            _const_spec((PAIRS, 1, PAIR_LANES)),
            _const_spec((GROUPS, FLAT, FLAT)),
            _const_spec((PAIRS, PAIR_LANES, 2 * FLAT)),
        ],
        out_specs=pl.BlockSpec((1, SCAN_ROWS, D_SSM), lambda b, t: (b, nt - 1 - t, 0)),
        out_shape=jax.ShapeDtypeStruct((bsz, n_chunks * CHUNK, D_SSM), _BF16),
        scratch_shapes=[
            pltpu.VMEM((GROUPS, FLAT, SCAN_TILE), _BF16),
            pltpu.VMEM((COL_BLOCKS, SCAN_TILE * CHUNK_PITCH, LANE), _F32),
            pltpu.VMEM((2, PAIRS, 8, SUBLANES, LANE), _F32),
            pltpu.VMEM((PAIRS, 1, half), _F32),
        ],
        compiler_params=pltpu.CompilerParams(
            dimension_semantics=("arbitrary", "arbitrary"), vmem_limit_bytes=VMEM_LIMIT),
        name="s5_readout",
    )(uflat, xf, sb, seed_b, a16_re, a16_im, mu, mv)


def _window_sum(u, w):
    assert w // 2 <= SUBLANES
    rows, lanes = u.shape
    nb = rows // GRID_W
    pad = jnp.zeros((nb, SUBLANES, lanes), _F32)
    z = jnp.concatenate([pad, u.reshape(nb, GRID_W, lanes), pad], axis=1)
    n = nb * (GRID_W + 2 * SUBLANES)
    z = z.reshape(n, lanes)
    acc = z + pltpu.roll(z, 1, 0)
    m = 2
    while m < w:
        acc = pltpu.roll(acc, m // 2, 0) + pltpu.roll(acc, n - m // 2, 0)
        m *= 2
    return acc.reshape(nb, GRID_W + 2 * SUBLANES, lanes)[:, SUBLANES:SUBLANES + GRID_W, :].reshape(rows, lanes)


def _mix_kernel(x_ref, y_ref, sh_ref, sc_ref, gt_ref, g_ref, bglu_ref, pinv_ref, ps_ref, pw_ref,
                win_hbm, wglu_hbm, wa_hbm, wb_hbm, wo_hbm, o_ref,
                wr_ref, wglu_ref, wa_ref, wb_ref, wo_ref, stage_ref, sem):
    @pl.when(_first_step())
    def _load_weights():
        r = stage_ref.shape[1]
        n_rest = D_POOL + 2 * D_MODEL

        def pieces(src3, dst, n_rows, c0, n_cols):
            return [(src3.at[0, pl.ds(k * r, r), pl.ds(c0, n_cols)],
                     [(dst.at[pl.ds(k * r, r), :], slice(0, n_cols))]) for k in range(n_rows // r)]

        jobs = pieces(win_hbm, wr_ref, D_MODEL, D_SSM, n_rest)
        jobs += pieces(wglu_hbm, wglu_ref, D_SSM, 0, D_SSM)
        jobs += pieces(wa_hbm, wa_ref, D_SSM, 0, D_MODEL)
        jobs += pieces(wb_hbm, wb_ref, D_POOL, 0, D_MODEL)
        jobs += pieces(wo_hbm, wo_ref, D_MODEL, 0, D_MODEL)
        _stream_cast(jobs, stage_ref, sem)

    x = x_ref[0]
    h = _rms_mod(x, g_ref[...] * (1.0 + sc_ref[0]), sh_ref[0]).astype(_BF16)
    c_pool, c_ga, c_gb = 0, D_POOL, D_POOL + D_MODEL
    windows = range(len(POOL_WINDOWS))
    group = lambda wi: slice(wi * POOL_GROUP_CH, (wi + 1) * POOL_GROUP_CH)

    ub = _dot(h, wr_ref[:, c_pool:c_ga])
    y = jax.nn.gelu(y_ref[0].astype(_F32))
    glu = _dot(y.astype(_BF16), wglu_ref[...])
    wsums = [_window_sum(ub[:, group(wi)], POOL_WINDOWS[wi]) for wi in windows]
    gate_a = _dot(h, wr_ref[:, c_ga:c_gb])
    z = y * jax.nn.sigmoid(glu + bglu_ref[...])
    ya = _dot(z.astype(_BF16), wa_ref[...])
    outs = [_dot((wsums[wi] * pinv_ref[wi] - ub[:, group(wi)]).astype(_BF16), pw_ref[wi])
            for wi in windows]
    gate_b = _dot(h, wr_ref[:, c_gb:])
    pb = jnp.concatenate(outs, axis=1) * ps_ref[...]
    yb = _dot(pb.astype(_BF16), wb_ref[...])

    merged = jax.nn.sigmoid(gate_a) * ya + jax.nn.sigmoid(gate_b) * yb
    mixed = _dot(merged.astype(_BF16), wo_ref[...])
    o_ref[0] = x + gt_ref[0] * mixed


def _pool_inverse_counts(rows):
    pos = np.arange(rows) % GRID_W
    invs = []
    for w in POOL_WINDOWS:
        lo = np.clip(pos - w // 2, 0, GRID_W - 1)
        hi = np.clip(pos + w - 1 - w // 2, 0, GRID_W - 1) + 1
        invs.append(np.broadcast_to((1.0 / (hi - lo).astype(np.float32))[:, None], (rows, POOL_GROUP_CH)))
    return np.stack(invs)


def _mixer(x, ypre, mod3, norm_g, b_glu, pool_scale, w_in, w_glu, w_a, pool_w, w_b, w_out):
    bsz, n_tok, _ = x.shape
    tm = MIX_ROWS
    pinv = jnp.asarray(_pool_inverse_counts(tm), _F32)
    nw = len(POOL_WINDOWS)
    n_rest = D_POOL + 2 * D_MODEL
    hbm = pl.BlockSpec(memory_space=pl.ANY)
    return pl.pallas_call(
        _mix_kernel,
        grid=(bsz, n_tok // tm),
        in_specs=[
            pl.BlockSpec((1, tm, D_MODEL), lambda b, t: (b, t, 0)),
            pl.BlockSpec((1, tm, D_SSM), lambda b, t: (b, t, 0)),
            pl.BlockSpec((1, 1, D_MODEL), lambda b, t: (b, 0, 0)),
            pl.BlockSpec((1, 1, D_MODEL), lambda b, t: (b, 0, 1)),
            pl.BlockSpec((1, 1, D_MODEL), lambda b, t: (b, 0, 2)),
            _const_spec((1, D_MODEL)),
            _const_spec((1, D_SSM)),
            _const_spec((nw, tm, POOL_GROUP_CH)),
            _const_spec((1, D_POOL)),
            _const_spec((None, nw, POOL_GROUP_CH, POOL_GROUP_CH)),
            hbm, hbm, hbm, hbm, hbm,
        ],
        out_specs=pl.BlockSpec((1, tm, D_MODEL), lambda b, t: (b, t, 0)),
        out_shape=jax.ShapeDtypeStruct(x.shape, _F32),
        scratch_shapes=[
            pltpu.VMEM((D_MODEL, n_rest), _BF16),
            pltpu.VMEM((D_SSM, D_SSM), _BF16),
            pltpu.VMEM((D_SSM, D_MODEL), _BF16),
            pltpu.VMEM((D_POOL, D_MODEL), _BF16),
            pltpu.VMEM((D_MODEL, D_MODEL), _BF16),
            pltpu.VMEM((WEIGHT_STAGE_SLOTS, 4 * WEIGHT_STAGE_ROWS, n_rest), _F32),
            pltpu.SemaphoreType.DMA((WEIGHT_STAGE_SLOTS,)),
        ],
        compiler_params=pltpu.CompilerParams(
            dimension_semantics=("arbitrary", "arbitrary"), vmem_limit_bytes=VMEM_LIMIT),
        name="token_mixer",
    )(x, ypre, mod3, mod3, mod3, norm_g, b_glu, pinv, pool_scale, pool_w.astype(_BF16),
      w_in, w_glu, w_a, w_b, w_out)


def _stream_cast(jobs, stage_ref, sem):
    n_slots = stage_ref.shape[0]

    def copy(i):
        src = jobs[i][0]
        rows, cols = src.shape
        slot = i % n_slots
        return pltpu.make_async_copy(src, stage_ref.at[slot, pl.ds(0, rows), pl.ds(0, cols)], sem.at[slot])

    for i in range(min(n_slots - 1, len(jobs))):
        copy(i).start()
    for i, (src, dsts) in enumerate(jobs):
        if i + n_slots - 1 < len(jobs):
            copy(i + n_slots - 1).start()
        copy(i).wait()
        rows = src.shape[0]
        for dst, cols in dsts:
            dst[...] = stage_ref[i % n_slots, 0:rows, cols].astype(_BF16)


def _first_step():
    return jnp.logical_and(pl.program_id(0) == 0, pl.program_id(1) == 0)


def _ffn_kernel(x_ref, sh_ref, sc_ref, gt_ref, g2_ref, gf_ref, win_hbm, wout_hbm, o_ref,
                wg_ref, wu_ref, wo_ref, stage_in_ref, stage_out_ref, sem_in, sem_out):
    @pl.when(_first_step())
    def _load_weights():
        r = stage_in_ref.shape[1]
        _stream_cast([(win_hbm.at[0, pl.ds(k * r, r), :],
                       [(wg_ref.at[pl.ds(k * r, r), :], slice(0, FFN_HIDDEN)),
                        (wu_ref.at[pl.ds(k * r, r), :], slice(FFN_HIDDEN, 2 * FFN_HIDDEN))])
                      for k in range(D_MODEL // r)], stage_in_ref, sem_in)
        r = stage_out_ref.shape[1]
        _stream_cast([(wout_hbm.at[0, pl.ds(k * r, r), :], [(wo_ref.at[pl.ds(k * r, r), :], slice(0, D_MODEL))])
                      for k in range(FFN_HIDDEN // r)], stage_out_ref, sem_out)

    x = x_ref[0]
    h = _rms_mod(x, g2_ref[...] * (1.0 + sc_ref[0]), sh_ref[0]).astype(_BF16)
    down = None
    for c0, c1 in FFN_CHUNKS:
        gate = _dot(h, wg_ref[:, c0:c1])
        up = _dot(h, wu_ref[:, c0:c1])
        part = _dot((gate * jax.nn.sigmoid(gate) * up).astype(_BF16), wo_ref[c0:c1, :])
        down = part if down is None else down + part
    y = x + gt_ref[0] * down
    ms = jnp.mean(y * y, axis=-1, keepdims=True)
    o_ref[0] = (y * lax.rsqrt(ms + RMS_EPS)) * gf_ref[...]


def _ffn(x1, mod3, norm2_g, final_g, w_ffn, w_down):
    bsz, n_tok, _ = x1.shape
    tm = FFN_ROWS
    hbm = pl.BlockSpec(memory_space=pl.ANY)
    return pl.pallas_call(
        _ffn_kernel,
        grid=(bsz, n_tok // tm),
        in_specs=[
            pl.BlockSpec((1, tm, D_MODEL), lambda b, t: (b, t, 0)),
            pl.BlockSpec((1, 1, D_MODEL), lambda b, t: (b, 0, 3)),
            pl.BlockSpec((1, 1, D_MODEL), lambda b, t: (b, 0, 4)),
            pl.BlockSpec((1, 1, D_MODEL), lambda b, t: (b, 0, 5)),
            _const_spec((1, D_MODEL)),
            _const_spec((1, D_MODEL)),
            hbm,
            hbm,
        ],
        out_specs=pl.BlockSpec((1, tm, D_MODEL), lambda b, t: (b, t, 0)),
        out_shape=jax.ShapeDtypeStruct(x1.shape, _F32),
        scratch_shapes=[
            pltpu.VMEM((D_MODEL, FFN_HIDDEN), _BF16),
            pltpu.VMEM((D_MODEL, FFN_HIDDEN), _BF16),
            pltpu.VMEM((FFN_HIDDEN, D_MODEL), _BF16),
            pltpu.VMEM((WEIGHT_STAGE_SLOTS, WEIGHT_STAGE_ROWS, 2 * FFN_HIDDEN), _F32),
            pltpu.VMEM((WEIGHT_STAGE_SLOTS, FFN_HIDDEN // SUBLANES, D_MODEL), _F32),
            pltpu.SemaphoreType.DMA((WEIGHT_STAGE_SLOTS,)),
            pltpu.SemaphoreType.DMA((WEIGHT_STAGE_SLOTS,)),
        ],
        compiler_params=pltpu.CompilerParams(
            dimension_semantics=("arbitrary", "arbitrary"), vmem_limit_bytes=VMEM_LIMIT),
        name="swiglu_ffn",
    )(x1, mod3, mod3, mod3, norm2_g, final_g, w_ffn, w_down)


def _cmul(xr, xi, yr, yi):
    return xr * yr - xi * yi, xr * yi + xi * yr


def _split_bf16(v):
    hi = v.astype(_BF16)
    return hi, (v - hi.astype(_F32)).astype(_BF16)


def _dot3_nt(a, b):
    dims = (((1,), (1,)), ((), ()))
    nt = lambda p, q: lax.dot_general(p, q, dims, preferred_element_type=_F32)
    a_hi, a_lo = _split_bf16(a)
    b_hi, b_lo = _split_bf16(b)
    return nt(a_hi, b_hi) + (nt(a_hi, b_lo) + nt(a_lo, b_hi))


def _tables_kernel(a_ref, bt_ref, c_ref, d_ref, ct_ref, bmod_ref, wmod_hbm,
                   mu_ref, wst_ref, mv_ref, a16r_ref, a16i_ref, mod_ref,
                   kt_ref, khl_ref, vnat_ref, cp_ref, act_ref, wstage_ref, sem, *, n_rows):
    n_slots = wstage_ref.shape[0]
    mod_blk = wstage_ref.shape[2]
    ct = ct_ref[...]
    act_ref[...] = ct * jax.nn.sigmoid(ct)

    def mod_copy(i):
        c0 = pl.multiple_of(i * mod_blk, LANE)
        slot = i % n_slots
        return pltpu.make_async_copy(wmod_hbm.at[0, :, pl.ds(c0, mod_blk)], wstage_ref.at[slot], sem.at[slot])

    for i in range(n_slots - 1):
        mod_copy(i).start()

    def pair_body(q, carry):
        mod_copy(q).wait()

        @pl.when(q + n_slots - 1 < PAIRS)
        def _refill():
            mod_copy(q + n_slots - 1).start()

        w = wstage_ref[q % n_slots]
        for r in range(n_rows):
            mod_ref[r, q] = jnp.sum(w * act_ref[:, r:r + 1], axis=0, keepdims=True) + bmod_ref[q]
        for r in range(n_rows, mod_ref.shape[0]):
            mod_ref[r, q] = jnp.zeros((1, mod_blk), _F32)

        for d in range(2):
            ar = a_ref[0, d, q]
            ai = a_ref[1, d, q]
            dt = jnp.exp(a_ref[2, d, q])
            mag = jnp.exp(ar * dt)
            ang = ai * dt
            abr, abi = mag * jnp.cos(ang), mag * jnp.sin(ang)
            den = ar * ar + ai * ai
            fr = ((abr - 1.0) * ar + abi * ai) / den
            fi = (abi * ar - (abr - 1.0) * ai) / den
            bbr, bbi = _cmul(bt_ref[0, d, q], bt_ref[1, d, q], fr, fi)
            cr = c_ref[0, d, q]
            ci = c_ref[1, d, q]
            pw = [(jnp.ones_like(ar), jnp.zeros_like(ar))]
            for _ in range(CHUNK):
                pw.append(_cmul(pw[-1][0], pw[-1][1], abr, abi))
            lanes_re = slice(2 * d * LANE, (2 * d + 1) * LANE)
            lanes_im = slice((2 * d + 1) * LANE, (2 * d + 2) * LANE)
            for lanes in (lanes_re, lanes_im):
                a16r_ref[q, :, lanes] = pw[CHUNK][0]
                a16i_ref[q, :, lanes] = pw[CHUNK][1]
            for sg in range(CHUNK):
                e = (CHUNK - 1 - sg) if d == 0 else sg
                wr, wi = _cmul(bbr, bbi, pw[e][0], pw[e][1])
                e = (sg + 1) if d == 0 else (CHUNK - sg)
                vr, vi = _cmul(cr, ci, pw[e][0], pw[e][1])
                for gg in range(2):
                    src = slice(gg * GROUP_CH, (gg + 1) * GROUP_CH)
                    dst = slice(gg * FLAT + sg * GROUP_CH, gg * FLAT + (sg + 1) * GROUP_CH)
                    wst_ref[q, dst, lanes_re] = wr[src].astype(_BF16)
                    wst_ref[q, dst, lanes_im] = wi[src].astype(_BF16)
                    vnat_ref[2 * d, dst, :] = vr[src]
                    vnat_ref[2 * d + 1, dst, :] = -vi[src]
            for gg in range(2):
                src = slice(gg * GROUP_CH, (gg + 1) * GROUP_CH)
                for k in range(CHUNK):
                    e = k if d == 0 else (CHUNK - 1 - k)
                    pr, pi = _cmul(cr[src], ci[src], pw[e][0], pw[e][1])
                    cp_ref[0, k * GROUP_CH:(k + 1) * GROUP_CH, :] = pr
                    cp_ref[1, k * GROUP_CH:(k + 1) * GROUP_CH, :] = pi
                kt = _dot3_nt(bbr[src], cp_ref[0]) - _dot3_nt(bbi[src], cp_ref[1])
                r0 = pl.multiple_of((2 * q + gg) * GROUP_CH, GROUP_CH)
                kt_ref[d, pl.ds(r0, GROUP_CH), :] = kt
        for part in range(4):
            mv_ref[q, part * LANE:(part + 1) * LANE, :] = vnat_ref[part].T.astype(_BF16)
        return carry

    lax.fori_loop(0, PAIRS, pair_body, 0, unroll=2)

    for d in range(2):
        khl_ref[2 * d], khl_ref[2 * d + 1] = _split_bf16(kt_ref[d])

    row = lax.broadcasted_iota(jnp.int32, (FLAT, FLAT), 0)
    col = lax.broadcasted_iota(jnp.int32, (FLAT, FLAT), 1)
    same_ch = (row % GROUP_CH) == (col % GROUP_CH)
    row_blk = row // GROUP_CH
    col_blk = col // GROUP_CH
    orow = lax.broadcasted_iota(jnp.int32, (GROUPS * GROUP_CH, FLAT), 0)
    ocol = lax.broadcasted_iota(jnp.int32, (GROUPS * GROUP_CH, FLAT), 1)
    skip_ch = (orow % GROUP_CH) == (ocol % GROUP_CH)
    ocol_blk = ocol // GROUP_CH
    d_col = d_ref[...]

    def toeplitz_body(sg, carry):
        sf = jnp.where(same_ch & (row_blk + sg == col_blk), 1.0, 0.0).astype(_BF16)
        sb = jnp.where(same_ch & (row_blk == col_blk + (CHUNK - 1) - sg), 1.0, 0.0).astype(_BF16)
        out = (_dot(khl_ref[0], sf) + _dot(khl_ref[1], sf)) + (_dot(khl_ref[2], sb) + _dot(khl_ref[3], sb))
        out = out + jnp.where(skip_ch & (ocol_blk == sg), d_col, 0.0)
        r0 = pl.multiple_of(sg * GROUP_CH, GROUP_CH)
        mu_ref[:, pl.ds(r0, GROUP_CH), :] = out.reshape(GROUPS, GROUP_CH, FLAT).astype(_BF16)
        return carry

    lax.fori_loop(0, CHUNK, toeplitz_body, 0, unroll=4)


def _s5_tables(a_re, a_im, log_dt, b_re, b_im, c_re, c_im, d_skip, cc_t, w_mod, b_mod, n_rows):
    f32 = _F32
    n_mod = w_mod.shape[-1]
    mod_blk = n_mod // PAIRS
    assert n_mod % PAIRS == 0 and mod_blk % LANE == 0
    eye2 = jnp.eye(2, dtype=f32)

    def pair_blocks(re, im):
        v = jnp.stack([re, im]).astype(f32).reshape(2, 2, PAIRS, 2, GROUP_CH, 1, STATE)
        v = v * eye2[None, None, None, :, None, :, None]
        return v.reshape(2, 2, PAIRS, 2 * GROUP_CH, 2 * STATE)

    ldt = jnp.broadcast_to(log_dt.astype(f32)[..., None], (2, GROUPS, STATE))
    a_rows = jnp.stack([a_re.astype(f32), a_im.astype(f32), ldt]).reshape(3, 2, PAIRS, 1, 2 * STATE)
    args = (a_rows, pair_blocks(jnp.swapaxes(b_re, 2, 3), jnp.swapaxes(b_im, 2, 3)),
            pair_blocks(c_re, c_im), d_skip.astype(f32).reshape(D_SSM, 1),
            cc_t, b_mod.reshape(PAIRS, 1, mod_blk))
    whole = lambda a: pl.BlockSpec(a.shape, lambda i, n=a.ndim: (0,) * n)
    out_shape = [
        jax.ShapeDtypeStruct((GROUPS, FLAT, FLAT), _BF16),
        jax.ShapeDtypeStruct((PAIRS, 2 * FLAT, PAIR_LANES), _BF16),
        jax.ShapeDtypeStruct((PAIRS, PAIR_LANES, 2 * FLAT), _BF16),
        jax.ShapeDtypeStruct((PAIRS, 1, PAIR_LANES), _F32),
        jax.ShapeDtypeStruct((PAIRS, 1, PAIR_LANES), _F32),
        jax.ShapeDtypeStruct((SUBLANES, PAIRS, 1, mod_blk), _F32),
    ]
    *tables, mod = pl.pallas_call(
        functools.partial(_tables_kernel, n_rows=n_rows),
        grid=(1,),
        in_specs=[whole(a) for a in args] + [pl.BlockSpec(memory_space=pl.ANY)],
        out_specs=[whole(s) for s in out_shape],
        out_shape=out_shape,
        scratch_shapes=[
            pltpu.VMEM((2, GROUPS * GROUP_CH, FLAT), _F32),
            pltpu.VMEM((4, GROUPS * GROUP_CH, FLAT), _BF16),
            pltpu.VMEM((4, 2 * FLAT, LANE), _F32),
            pltpu.VMEM((2, FLAT, LANE), _F32),
            pltpu.VMEM((D_MODEL, SUBLANES), _F32),
            pltpu.VMEM((WEIGHT_STAGE_SLOTS, D_MODEL, mod_blk), _F32),
            pltpu.SemaphoreType.DMA((WEIGHT_STAGE_SLOTS,)),
        ],
        compiler_params=pltpu.CompilerParams(
            dimension_semantics=("arbitrary",), vmem_limit_bytes=VMEM_LIMIT),
        name="s5_tables_adaln",
    )(*args, w_mod)
    return tables, mod.reshape(SUBLANES, 1, n_mod)


def kernel(x, c, ctx, c_ctx, w_mod, b_mod, norm1_g, norm2_g, w_in, s5_a_re, s5_a_im, s5_log_dt,
           s5_b_re, s5_b_im, s5_c_re, s5_c_im, s5_d, w_glu, b_glu, pool_w, pool_scale,
           w_branch_a, w_branch_b, w_out, w_ffn_in, w_ffn_out, final_norm_g):
    bsz, n_tok, d = x.shape
    ctx_len = ctx.shape[1]
    assert d == D_MODEL and w_mod.shape[0] == 1 and bsz + 1 <= SUBLANES
    assert n_tok % SCAN_ROWS == 0 and n_tok % MIX_ROWS == 0 and MIX_ROWS % GRID_W == 0
    assert bsz * ctx_len <= SCAN_ROWS

    cc_t = jnp.concatenate(
        [c.T, c_ctx[:, None], jnp.zeros((D_MODEL, SUBLANES - bsz - 1), _F32)], axis=1)
    (mu, wst, mv, a16_re, a16_im), mod3 = _s5_tables(
        s5_a_re[0], s5_a_im[0], s5_log_dt[0], s5_b_re[0], s5_b_im[0], s5_c_re[0], s5_c_im[0], s5_d[0],
        cc_t, w_mod, b_mod, bsz + 1)

    seed_f, seed_b = _pass1_ctx(ctx.reshape(1, bsz * ctx_len, D_MODEL), mod3, bsz, norm1_g, w_in, wst,
                                a16_re, a16_im, bsz)
    uflat, x_fwd, s_bwd = _pass1(x, mod3, norm1_g, w_in, wst, a16_re, a16_im, seed_f)
    ypre = _readout(uflat, x_fwd, s_bwd, seed_b, a16_re, a16_im, mu, mv)

    x1 = _mixer(x, ypre, mod3, norm1_g, b_glu, pool_scale, w_in, w_glu, w_branch_a, pool_w, w_branch_b, w_out)
    return _ffn(x1, mod3, norm2_g, final_norm_g.reshape(1, D_MODEL), w_ffn_in, w_ffn_out)
```

```python
import functools

import numpy as np
import jax
import jax.numpy as jnp
from jax import lax
from jax.experimental import pallas as pl
from jax.experimental.pallas import tpu as pltpu

_F32 = jnp.float32
_BF16 = jnp.bfloat16

D_MODEL = 1024
D_SSM = 512
D_POOL = 512
GROUPS = 32
STATE = 64
GROUP_CH = 16
CHUNK = 16
FLAT = CHUNK * GROUP_CH
PAIRS = GROUPS // 2
PAIR_LANES = 4 * 2 * STATE
LANE = 128
SUBLANES = 8
CHUNK_PITCH = 40
COL_BLOCKS = D_SSM // LANE
GRID_W = 64
POOL_WINDOWS = (2, 4, 8, 16)
POOL_GROUP_CH = D_POOL // len(POOL_WINDOWS)
FFN_HIDDEN = 2816
RMS_EPS = 1e-6

SCAN_TILE = 128
SCAN_ROWS = SCAN_TILE * CHUNK
NORM_ROWS = 512
MIX_ROWS = 1024
FFN_ROWS = 1024
FFN_CHUNKS = ((0, 1536), (1536, FFN_HIDDEN))
RELAYOUT_UNROLL = 16
WEIGHT_STAGE_ROWS = 64
WEIGHT_STAGE_SLOTS = 3
VMEM_LIMIT = 56 * 1024 * 1024


def _rms_mod(x, gain, sh):
    ms = jnp.mean(x * x, axis=-1, keepdims=True)
    return (x * lax.rsqrt(ms + RMS_EPS)) * gain + sh


def _dot(a, b):
    return jnp.dot(a, b, preferred_element_type=_F32)


def _const_spec(shape, index=None):
    index = (0,) * len(shape) if index is None else index
    return pl.BlockSpec(shape, lambda *_: index, pipeline_mode=pl.Buffered(1))


def _to_pitch(v):
    n = v.shape[0] // CHUNK
    v = v.reshape(n, CHUNK, v.shape[1])
    pad = jnp.zeros((n, CHUNK_PITCH - CHUNK, v.shape[2]), v.dtype)
    return jnp.concatenate([v, pad], axis=1).reshape(n * CHUNK_PITCH, v.shape[2])


def _from_pitch(v):
    n = v.shape[0] // CHUNK_PITCH
    return v.reshape(n, CHUNK_PITCH, v.shape[1])[:, :CHUNK, :].reshape(n * CHUNK, v.shape[1])


def _scan_constants(ar_ref, ai_ref, c_ref, dirs):
    rb = SUBLANES
    rid = lax.broadcasted_iota(jnp.int32, (rb, LANE), 0)

    def body(q, carry):
        for d in dirs:
            o = d * 2 * LANE
            a_r = ar_ref[q][:, o:o + LANE]
            a_i = ai_ref[q][:, o:o + LANE]
            pows = [(a_r, a_i)]
            for _ in range(rb - 1):
                pows.append(_cmul(pows[-1][0], pows[-1][1], a_r, a_i))
            for k, shift in enumerate((1, 2, 4)):
                keep = (rid >= shift) if d == 0 else (rid < rb - shift)
                c_ref[d, q, 2 * k] = jnp.where(keep, pows[shift - 1][0], 0.0)
                c_ref[d, q, 2 * k + 1] = jnp.where(keep, pows[shift - 1][1], 0.0)
            p_r = jnp.zeros((rb, LANE), _F32)
            p_i = jnp.zeros((rb, LANE), _F32)
            for r in range(rb):
                e = r if d == 0 else rb - 1 - r
                p_r = jnp.where(rid == r, pows[e][0], p_r)
                p_i = jnp.where(rid == r, pows[e][1], p_i)
            c_ref[d, q, 6] = p_r
            c_ref[d, q, 7] = p_i
        return carry

    lax.fori_loop(0, PAIRS, body, 0)


def _scan_block(c_ref, d, q, s_re, s_im, xin_re, xin_im):
    rb = SUBLANES
    rid = lax.broadcasted_iota(jnp.int32, (rb, LANE), 0)

    def shifted(v, k):
        return pltpu.roll(v, k if d == 0 else rb - k, 0)

    t_re, t_im = s_re, s_im
    for k in range(3):
        a_r = c_ref[d, q, 2 * k]
        a_i = c_ref[d, q, 2 * k + 1]
        u_re, u_im = shifted(t_re, 1 << k), shifted(t_im, 1 << k)
        t_re, t_im = t_re + (a_r * u_re - a_i * u_im), t_im + (a_r * u_im + a_i * u_re)
    p_r = c_ref[d, q, 6]
    p_i = c_ref[d, q, 7]
    after_re = t_re + (p_r * xin_re - p_i * xin_im)
    after_im = t_im + (p_r * xin_im + p_i * xin_re)
    first = 0 if d == 0 else rb - 1
    last = rb - 1 - first
    start_re = jnp.where(rid == first, xin_re, shifted(after_re, 1))
    start_im = jnp.where(rid == first, xin_im, shifted(after_im, 1))
    return start_re, start_im, after_re[last:last + 1], after_im[last:last + 1]


def _scan_rows(c_ref, d, q, s, xin, blocks):
    rb = SUBLANES
    x_re, x_im = xin[:, :LANE], xin[:, LANE:]
    starts = {}
    for blk in (blocks if d == 0 else blocks[::-1]):
        r = slice(blk * rb, (blk + 1) * rb)
        st_re, st_im, x_re, x_im = _scan_block(c_ref, d, q, s[r, :LANE], s[r, LANE:], x_re, x_im)
        starts[blk] = (st_re, st_im)
    return starts, jnp.concatenate([x_re, x_im], axis=1)


def _starts_to_rows(starts, blocks):
    return jnp.concatenate(
        [jnp.concatenate([starts[b][0] for b in blocks], axis=0),
         jnp.concatenate([starts[b][1] for b in blocks], axis=0)], axis=1)


def _p1_project(x_ref, sh_ref, sc_ref, g_ref, wa_ref, h_ref, u_ref, ut_ref, rows):
    gain = g_ref[...] * (1.0 + sc_ref[0])
    sh = sh_ref[0]
    wa = wa_ref[...].astype(_BF16)
    block_chunks = NORM_ROWS // CHUNK

    def norm_block(i):
        r = slice(i * NORM_ROWS, (i + 1) * NORM_ROWS)
        h_ref[r, :] = _rms_mod(x_ref[0, r, :], gain, sh).astype(_BF16)

    norm_block(0)
    for i in range(rows // NORM_ROWS):
        if (i + 1) * NORM_ROWS < rows:
            norm_block(i + 1)
        u = _dot(h_ref[i * NORM_ROWS:(i + 1) * NORM_ROWS, :], wa)
        pr = slice(i * block_chunks * CHUNK_PITCH, (i + 1) * block_chunks * CHUNK_PITCH)
        for cb in range(COL_BLOCKS):
            u_ref[cb, pr, :] = _to_pitch(u[:, cb * LANE:(cb + 1) * LANE])
    if rows < SCAN_ROWS:
        first = rows // CHUNK * CHUNK_PITCH
        for cb in range(COL_BLOCKS):
            u_ref[cb, first:, :] = jnp.zeros((SCAN_TILE * CHUNK_PITCH - first, LANE), _F32)

    def slab_body(sg, carry):
        r0 = pl.multiple_of(sg * GROUP_CH, GROUP_CH)
        for cb in range(COL_BLOCKS):
            slab = u_ref[cb, pl.ds(sg, SCAN_TILE, stride=CHUNK_PITCH), :]
            ut_ref[cb * 8:(cb + 1) * 8, pl.ds(r0, GROUP_CH), :] = (
                slab.astype(_BF16).T.reshape(8, GROUP_CH, SCAN_TILE))
        return carry

    lax.fori_loop(0, CHUNK, slab_body, 0, unroll=RELAYOUT_UNROLL)


def _pair_states(ut_ref, wst_ref, q):
    uf0 = ut_ref[2 * q].T
    uf1 = ut_ref[2 * q + 1].T
    return uf0, uf1, _dot(uf0, wst_ref[q, :FLAT, :]) + _dot(uf1, wst_ref[q, FLAT:, :])


def _p1_kernel(x_ref, sh_ref, sc_ref, g_ref, wa_ref, wst_ref, ar_ref, ai_ref, seed_ref,
               uflat_ref, xf_ref, sb_ref, h_ref, u_ref, ut_ref, c_ref, carry_ref):
    @pl.when(_first_step())
    def _constants():
        _scan_constants(ar_ref, ai_ref, c_ref, (0,))

    @pl.when(pl.program_id(1) == 0)
    def _seed():
        carry_ref[...] = seed_ref[0]

    _p1_project(x_ref, sh_ref, sc_ref, g_ref, wa_ref, h_ref, u_ref, ut_ref, SCAN_ROWS)
    half = PAIR_LANES // 2
    blocks = list(range(SCAN_TILE // SUBLANES))

    def pair_body(q, carry):
        uf0, uf1, s = _pair_states(ut_ref, wst_ref, q)
        uflat_ref[0, 2 * q] = uf0
        uflat_ref[0, 2 * q + 1] = uf1
        starts, x_out = _scan_rows(c_ref, 0, q, s[:, :half], carry_ref[q], blocks)
        xf_ref[0, q] = _starts_to_rows(starts, blocks).astype(_BF16)
        carry_ref[q] = x_out
        sb_ref[0, q] = s[:, half:]
        return carry

    lax.fori_loop(0, PAIRS, pair_body, 0, unroll=RELAYOUT_UNROLL)


def _p1_ctx_kernel(x_ref, sh_ref, sc_ref, g_ref, wa_ref, wst_ref, ar_ref, ai_ref,
                   seedf_ref, seedb_ref, h_ref, u_ref, ut_ref, c_ref, *, rows, bsz):
    _scan_constants(ar_ref, ai_ref, c_ref, (0, 1))
    _p1_project(x_ref, sh_ref, sc_ref, g_ref, wa_ref, h_ref, u_ref, ut_ref, rows)
    half = PAIR_LANES // 2
    per_seq = rows // CHUNK // SUBLANES // bsz
    zero = jnp.zeros((1, half), _F32)

    def pair_body(q, carry):
        _, _, s = _pair_states(ut_ref, wst_ref, q)
        for b in range(bsz):
            blocks = list(range(b * per_seq, (b + 1) * per_seq))
            _, seedf_ref[b, q] = _scan_rows(c_ref, 0, q, s[:, :half], zero, blocks)
            _, seedb_ref[b, q] = _scan_rows(c_ref, 1, q, s[:, half:], zero, blocks)
        return carry

    lax.fori_loop(0, PAIRS, pair_body, 0, unroll=RELAYOUT_UNROLL)


def _p1_scratch():
    return [
        pltpu.VMEM((SCAN_ROWS, D_MODEL), _BF16),
        pltpu.VMEM((COL_BLOCKS, SCAN_TILE * CHUNK_PITCH, LANE), _F32),
        pltpu.VMEM((GROUPS, FLAT, SCAN_TILE), _BF16),
        pltpu.VMEM((2, PAIRS, 8, SUBLANES, LANE), _F32),
    ]


def _pass1_ctx(ctx_rows, mod3, mod_row, norm_g, w_in, wst, a16_re, a16_im, bsz):
    rows = ctx_rows.shape[1]
    assert rows % NORM_ROWS == 0 and rows <= SCAN_ROWS and rows % (bsz * CHUNK * SUBLANES) == 0
    half = PAIR_LANES // 2
    seed = jax.ShapeDtypeStruct((bsz, PAIRS, 1, half), _F32)
    return pl.pallas_call(
        functools.partial(_p1_ctx_kernel, rows=rows, bsz=bsz),
        grid=(1, 1),
        in_specs=[
            pl.BlockSpec((1, rows, D_MODEL), lambda b, t: (0, 0, 0)),
            pl.BlockSpec((1, 1, D_MODEL), lambda b, t: (mod_row, 0, 0)),
            pl.BlockSpec((1, 1, D_MODEL), lambda b, t: (mod_row, 0, 1)),
            _const_spec((1, D_MODEL)),
            _const_spec((None, D_MODEL, D_SSM)),
            _const_spec((PAIRS, 2 * FLAT, PAIR_LANES)),
            _const_spec((PAIRS, 1, PAIR_LANES)),
            _const_spec((PAIRS, 1, PAIR_LANES)),
        ],
        out_specs=[pl.BlockSpec((bsz, PAIRS, 1, half), lambda b, t: (0, 0, 0, 0))] * 2,
        out_shape=[seed, seed],
        scratch_shapes=_p1_scratch(),
        compiler_params=pltpu.CompilerParams(
            dimension_semantics=("arbitrary", "arbitrary"), vmem_limit_bytes=VMEM_LIMIT),
        name="s5_context_states",
    )(ctx_rows, mod3, mod3, norm_g, w_in, wst, a16_re, a16_im)


def _pass1(x, mod3, norm_g, w_in, wst, a16_re, a16_im, seed_f):
    bsz, n_tok, _ = x.shape
    assert n_tok % SCAN_ROWS == 0
    nt = n_tok // SCAN_ROWS
    n_chunks = nt * SCAN_TILE
    half = PAIR_LANES // 2
    return pl.pallas_call(
        _p1_kernel,
        grid=(bsz, nt),
        in_specs=[
            pl.BlockSpec((1, SCAN_ROWS, D_MODEL), lambda b, t: (b, t, 0)),
            pl.BlockSpec((1, 1, D_MODEL), lambda b, t: (b, 0, 0)),
            pl.BlockSpec((1, 1, D_MODEL), lambda b, t: (b, 0, 1)),
            _const_spec((1, D_MODEL)),
            _const_spec((None, D_MODEL, D_SSM)),
            _const_spec((PAIRS, 2 * FLAT, PAIR_LANES)),
            _const_spec((PAIRS, 1, PAIR_LANES)),
            _const_spec((PAIRS, 1, PAIR_LANES)),
            pl.BlockSpec((1, PAIRS, 1, half), lambda b, t: (b, 0, 0, 0)),
        ],
        out_specs=[
            pl.BlockSpec((1, GROUPS, SCAN_TILE, FLAT), lambda b, t: (b, 0, t, 0)),
            pl.BlockSpec((1, PAIRS, SCAN_TILE, half), lambda b, t: (b, 0, t, 0)),
            pl.BlockSpec((1, PAIRS, SCAN_TILE, half), lambda b, t: (b, 0, t, 0)),
        ],
        out_shape=[
            jax.ShapeDtypeStruct((bsz, GROUPS, n_chunks, FLAT), _BF16),
            jax.ShapeDtypeStruct((bsz, PAIRS, n_chunks, half), _BF16),
            jax.ShapeDtypeStruct((bsz, PAIRS, n_chunks, half), _F32),
        ],
        scratch_shapes=_p1_scratch() + [pltpu.VMEM((PAIRS, 1, half), _F32)],
        compiler_params=pltpu.CompilerParams(
            dimension_semantics=("arbitrary", "arbitrary"), vmem_limit_bytes=VMEM_LIMIT),
        name="s5_chunk_states",
    )(x, mod3, mod3, norm_g, w_in, wst, a16_re, a16_im, seed_f)


def _readout_kernel(uflat_ref, xf_ref, sb_ref, seed_ref, ar_ref, ai_ref, mu_ref, mv_ref, y_ref,
                    yt_ref, ys_ref, c_ref, carry_ref):
    @pl.when(_first_step())
    def _constants():
        _scan_constants(ar_ref, ai_ref, c_ref, (1,))

    @pl.when(pl.program_id(1) == 0)
    def _seed():
        carry_ref[...] = seed_ref[0]

    blocks = list(range(SCAN_TILE // SUBLANES))

    def pair_body(q, carry):
        y0 = _dot(uflat_ref[0, 2 * q], mu_ref[2 * q])
        y1 = _dot(uflat_ref[0, 2 * q + 1], mu_ref[2 * q + 1])
        starts, x_out = _scan_rows(c_ref, 1, q, sb_ref[0, q], carry_ref[q], blocks)
        carry_ref[q] = x_out
        xs = jnp.concatenate([xf_ref[0, q], _starts_to_rows(starts, blocks).astype(_BF16)], axis=1)
        yx = _dot(xs, mv_ref[q])
        y = jnp.concatenate([y0, y1], axis=1) + yx
        yt = y.astype(yt_ref.dtype).T
        yt_ref[2 * q] = yt[:FLAT]
        yt_ref[2 * q + 1] = yt[FLAT:]
        return carry

    lax.fori_loop(0, PAIRS, pair_body, 0, unroll=RELAYOUT_UNROLL)

    def slab_body(sg, carry):
        r0 = pl.multiple_of(sg * GROUP_CH, GROUP_CH)
        for cb in range(COL_BLOCKS):
            yt = yt_ref[cb * 8:(cb + 1) * 8, pl.ds(r0, GROUP_CH), :].reshape(LANE, SCAN_TILE)
            ys_ref[cb, pl.ds(sg, SCAN_TILE, stride=CHUNK_PITCH), :] = yt.T.astype(_F32)
        return carry

    lax.fori_loop(0, CHUNK, slab_body, 0, unroll=RELAYOUT_UNROLL)
    for cb in range(COL_BLOCKS):
        y_ref[0, :, cb * LANE:(cb + 1) * LANE] = _from_pitch(ys_ref[cb]).astype(y_ref.dtype)


def _readout(uflat, xf, sb, seed_b, a16_re, a16_im, mu, mv):
    bsz, _, n_chunks, _ = uflat.shape
    nt = n_chunks // SCAN_TILE
    half = PAIR_LANES // 2
    rev = lambda b, t: (b, 0, nt - 1 - t, 0)
    return pl.pallas_call(
        _readout_kernel,
        grid=(bsz, nt),
        in_specs=[
            pl.BlockSpec((1, GROUPS, SCAN_TILE, FLAT), rev),
            pl.BlockSpec((1, PAIRS, SCAN_TILE, half), rev),
            pl.BlockSpec((1, PAIRS, SCAN_TILE, half), rev),
            pl.BlockSpec((1, PAIRS, 1, half), lambda b, t: (b, 0, 0, 0)),
            _const_spec((PAIRS, 1, PAIR_LANES)),
            _const_spec((PAIRS, 1, PAIR_LANES)),
            _const_spec((GROUPS, FLAT, FLAT)),
            _const_spec((PAIRS, PAIR_LANES, 2 * FLAT)),
        ],
        out_specs=pl.BlockSpec((1, SCAN_ROWS, D_SSM), lambda b, t: (b, nt - 1 - t, 0)),
        out_shape=jax.ShapeDtypeStruct((bsz, n_chunks * CHUNK, D_SSM), _BF16),
        scratch_shapes=[
            pltpu.VMEM((GROUPS, FLAT, SCAN_TILE), _BF16),
            pltpu.VMEM((COL_BLOCKS, SCAN_TILE * CHUNK_PITCH, LANE), _F32),
            pltpu.VMEM((2, PAIRS, 8, SUBLANES, LANE), _F32),
            pltpu.VMEM((PAIRS, 1, half), _F32),
        ],
        compiler_params=pltpu.CompilerParams(
            dimension_semantics=("arbitrary", "arbitrary"), vmem_limit_bytes=VMEM_LIMIT),
        name="s5_readout",
    )(uflat, xf, sb, seed_b, a16_re, a16_im, mu, mv)


def _window_sum(u, w):
    assert w // 2 <= SUBLANES
    rows, lanes = u.shape
    nb = rows // GRID_W
    pad = jnp.zeros((nb, SUBLANES, lanes), _F32)
    z = jnp.concatenate([pad, u.reshape(nb, GRID_W, lanes), pad], axis=1)
    n = nb * (GRID_W + 2 * SUBLANES)
    z = z.reshape(n, lanes)
    acc = z + pltpu.roll(z, 1, 0)
    m = 2
    while m < w:
        acc = pltpu.roll(acc, m // 2, 0) + pltpu.roll(acc, n - m // 2, 0)
        m *= 2
    return acc.reshape(nb, GRID_W + 2 * SUBLANES, lanes)[:, SUBLANES:SUBLANES + GRID_W, :].reshape(rows, lanes)


def _mix_kernel(x_ref, y_ref, sh_ref, sc_ref, gt_ref, g_ref, bglu_ref, pinv_ref, ps_ref, pw_ref,
                win_hbm, wglu_hbm, wa_hbm, wb_hbm, wo_hbm, o_ref,
                wr_ref, wglu_ref, wa_ref, wb_ref, wo_ref, stage_ref, sem):
    @pl.when(_first_step())
    def _load_weights():
        r = stage_ref.shape[1]
        n_rest = D_POOL + 2 * D_MODEL

        def pieces(src3, dst, n_rows, c0, n_cols):
            return [(src3.at[0, pl.ds(k * r, r), pl.ds(c0, n_cols)],
                     [(dst.at[pl.ds(k * r, r), :], slice(0, n_cols))]) for k in range(n_rows // r)]

        jobs = pieces(win_hbm, wr_ref, D_MODEL, D_SSM, n_rest)
        jobs += pieces(wglu_hbm, wglu_ref, D_SSM, 0, D_SSM)
        jobs += pieces(wa_hbm, wa_ref, D_SSM, 0, D_MODEL)
        jobs += pieces(wb_hbm, wb_ref, D_POOL, 0, D_MODEL)
        jobs += pieces(wo_hbm, wo_ref, D_MODEL, 0, D_MODEL)
        _stream_cast(jobs, stage_ref, sem)

    x = x_ref[0]
    h = _rms_mod(x, g_ref[...] * (1.0 + sc_ref[0]), sh_ref[0]).astype(_BF16)
    c_pool, c_ga, c_gb = 0, D_POOL, D_POOL + D_MODEL
    windows = range(len(POOL_WINDOWS))
    group = lambda wi: slice(wi * POOL_GROUP_CH, (wi + 1) * POOL_GROUP_CH)

    ub = _dot(h, wr_ref[:, c_pool:c_ga])
    y = jax.nn.gelu(y_ref[0].astype(_F32))
    glu = _dot(y.astype(_BF16), wglu_ref[...])
    wsums = [_window_sum(ub[:, group(wi)], POOL_WINDOWS[wi]) for wi in windows]
    gate_a = _dot(h, wr_ref[:, c_ga:c_gb])
    z = y * jax.nn.sigmoid(glu + bglu_ref[...])
    ya = _dot(z.astype(_BF16), wa_ref[...])
    outs = [_dot((wsums[wi] * pinv_ref[wi] - ub[:, group(wi)]).astype(_BF16), pw_ref[wi].astype(_BF16))
            for wi in windows]
    gate_b = _dot(h, wr_ref[:, c_gb:])
    pb = jnp.concatenate(outs, axis=1) * ps_ref[...]
    yb = _dot(pb.astype(_BF16), wb_ref[...])

    merged = jax.nn.sigmoid(gate_a) * ya + jax.nn.sigmoid(gate_b) * yb
    mixed = _dot(merged.astype(_BF16), wo_ref[...])
    o_ref[0] = x + gt_ref[0] * mixed


def _pool_inverse_counts(rows):
    pos = np.arange(rows) % GRID_W
    invs = []
    for w in POOL_WINDOWS:
        lo = np.clip(pos - w // 2, 0, GRID_W - 1)
        hi = np.clip(pos + w - 1 - w // 2, 0, GRID_W - 1) + 1
        invs.append(np.broadcast_to((1.0 / (hi - lo).astype(np.float32))[:, None], (rows, POOL_GROUP_CH)))
    return np.stack(invs)


def _mixer(x, ypre, mod3, norm_g, b_glu, pool_scale, w_in, w_glu, w_a, pool_w, w_b, w_out):
    bsz, n_tok, _ = x.shape
    tm = MIX_ROWS
    pinv = jnp.asarray(_pool_inverse_counts(tm), _F32)
    nw = len(POOL_WINDOWS)
    n_rest = D_POOL + 2 * D_MODEL
    hbm = pl.BlockSpec(memory_space=pl.ANY)
    return pl.pallas_call(
        _mix_kernel,
        grid=(bsz, n_tok // tm),
        in_specs=[
            pl.BlockSpec((1, tm, D_MODEL), lambda b, t: (b, t, 0)),
            pl.BlockSpec((1, tm, D_SSM), lambda b, t: (b, t, 0)),
            pl.BlockSpec((1, 1, D_MODEL), lambda b, t: (b, 0, 0)),
            pl.BlockSpec((1, 1, D_MODEL), lambda b, t: (b, 0, 1)),
            pl.BlockSpec((1, 1, D_MODEL), lambda b, t: (b, 0, 2)),
            _const_spec((1, D_MODEL)),
            _const_spec((1, D_SSM)),
            _const_spec((nw, tm, POOL_GROUP_CH)),
            _const_spec((1, D_POOL)),
            _const_spec((None, nw, POOL_GROUP_CH, POOL_GROUP_CH)),
            hbm, hbm, hbm, hbm, hbm,
        ],
        out_specs=pl.BlockSpec((1, tm, D_MODEL), lambda b, t: (b, t, 0)),
        out_shape=jax.ShapeDtypeStruct(x.shape, _F32),
        scratch_shapes=[
            pltpu.VMEM((D_MODEL, n_rest), _BF16),
            pltpu.VMEM((D_SSM, D_SSM), _BF16),
            pltpu.VMEM((D_SSM, D_MODEL), _BF16),
            pltpu.VMEM((D_POOL, D_MODEL), _BF16),
            pltpu.VMEM((D_MODEL, D_MODEL), _BF16),
            pltpu.VMEM((WEIGHT_STAGE_SLOTS, 4 * WEIGHT_STAGE_ROWS, n_rest), _F32),
            pltpu.SemaphoreType.DMA((WEIGHT_STAGE_SLOTS,)),
        ],
        compiler_params=pltpu.CompilerParams(
            dimension_semantics=("arbitrary", "arbitrary"), vmem_limit_bytes=VMEM_LIMIT),
        name="token_mixer",
    )(x, ypre, mod3, mod3, mod3, norm_g, b_glu, pinv, pool_scale, pool_w,
      w_in, w_glu, w_a, w_b, w_out)


def _stream_cast(jobs, stage_ref, sem):
    n_slots = stage_ref.shape[0]

    def copy(i):
        src = jobs[i][0]
        rows, cols = src.shape
        slot = i % n_slots
        return pltpu.make_async_copy(src, stage_ref.at[slot, pl.ds(0, rows), pl.ds(0, cols)], sem.at[slot])

    for i in range(min(n_slots - 1, len(jobs))):
        copy(i).start()
    for i, (src, dsts) in enumerate(jobs):
        if i + n_slots - 1 < len(jobs):
            copy(i + n_slots - 1).start()
        copy(i).wait()
        rows = src.shape[0]
        for dst, cols in dsts:
            dst[...] = stage_ref[i % n_slots, 0:rows, cols].astype(_BF16)


def _first_step():
    return jnp.logical_and(pl.program_id(0) == 0, pl.program_id(1) == 0)


def _ffn_kernel(x_ref, sh_ref, sc_ref, gt_ref, g2_ref, gf_ref, win_hbm, wout_hbm, o_ref,
                wg_ref, wu_ref, wo_ref, stage_in_ref, stage_out_ref, sem_in, sem_out):
    @pl.when(_first_step())
    def _load_weights():
        r = stage_in_ref.shape[1]
        _stream_cast([(win_hbm.at[0, pl.ds(k * r, r), :],
                       [(wg_ref.at[pl.ds(k * r, r), :], slice(0, FFN_HIDDEN)),
                        (wu_ref.at[pl.ds(k * r, r), :], slice(FFN_HIDDEN, 2 * FFN_HIDDEN))])
                      for k in range(D_MODEL // r)], stage_in_ref, sem_in)
        r = stage_out_ref.shape[1]
        _stream_cast([(wout_hbm.at[0, pl.ds(k * r, r), :], [(wo_ref.at[pl.ds(k * r, r), :], slice(0, D_MODEL))])
                      for k in range(FFN_HIDDEN // r)], stage_out_ref, sem_out)

    x = x_ref[0]
    h = _rms_mod(x, g2_ref[...] * (1.0 + sc_ref[0]), sh_ref[0]).astype(_BF16)
    down = None
    for c0, c1 in FFN_CHUNKS:
        gate = _dot(h, wg_ref[:, c0:c1])
        up = _dot(h, wu_ref[:, c0:c1])
        part = _dot((gate * jax.nn.sigmoid(gate) * up).astype(_BF16), wo_ref[c0:c1, :])
        down = part if down is None else down + part
    y = x + gt_ref[0] * down
    ms = jnp.mean(y * y, axis=-1, keepdims=True)
    o_ref[0] = (y * lax.rsqrt(ms + RMS_EPS)) * gf_ref[...]


def _ffn(x1, mod3, norm2_g, final_g, w_ffn, w_down):
    bsz, n_tok, _ = x1.shape
    tm = FFN_ROWS
    hbm = pl.BlockSpec(memory_space=pl.ANY)
    return pl.pallas_call(
        _ffn_kernel,
        grid=(bsz, n_tok // tm),
        in_specs=[
            pl.BlockSpec((1, tm, D_MODEL), lambda b, t: (b, t, 0)),
            pl.BlockSpec((1, 1, D_MODEL), lambda b, t: (b, 0, 3)),
            pl.BlockSpec((1, 1, D_MODEL), lambda b, t: (b, 0, 4)),
            pl.BlockSpec((1, 1, D_MODEL), lambda b, t: (b, 0, 5)),
            _const_spec((1, D_MODEL)),
            _const_spec((1, D_MODEL)),
            hbm,
            hbm,
        ],
        out_specs=pl.BlockSpec((1, tm, D_MODEL), lambda b, t: (b, t, 0)),
        out_shape=jax.ShapeDtypeStruct(x1.shape, _F32),
        scratch_shapes=[
            pltpu.VMEM((D_MODEL, FFN_HIDDEN), _BF16),
            pltpu.VMEM((D_MODEL, FFN_HIDDEN), _BF16),
            pltpu.VMEM((FFN_HIDDEN, D_MODEL), _BF16),
            pltpu.VMEM((WEIGHT_STAGE_SLOTS, WEIGHT_STAGE_ROWS, 2 * FFN_HIDDEN), _F32),
            pltpu.VMEM((WEIGHT_STAGE_SLOTS, FFN_HIDDEN // SUBLANES, D_MODEL), _F32),
            pltpu.SemaphoreType.DMA((WEIGHT_STAGE_SLOTS,)),
            pltpu.SemaphoreType.DMA((WEIGHT_STAGE_SLOTS,)),
        ],
        compiler_params=pltpu.CompilerParams(
            dimension_semantics=("arbitrary", "arbitrary"), vmem_limit_bytes=VMEM_LIMIT),
        name="swiglu_ffn",
    )(x1, mod3, mod3, mod3, norm2_g, final_g, w_ffn, w_down)


def _cmul(xr, xi, yr, yi):
    return xr * yr - xi * yi, xr * yi + xi * yr


def _split_bf16(v):
    hi = v.astype(_BF16)
    return hi, (v - hi.astype(_F32)).astype(_BF16)


def _dot3_nt(a, b):
    dims = (((1,), (1,)), ((), ()))
    nt = lambda p, q: lax.dot_general(p, q, dims, preferred_element_type=_F32)
    a_hi, a_lo = _split_bf16(a)
    b_hi, b_lo = _split_bf16(b)
    return nt(a_hi, b_hi) + (nt(a_hi, b_lo) + nt(a_lo, b_hi))


def _tables_kernel(a_ref, bt_ref, c_ref, d_ref, ct_ref, bmod_ref, wmod_hbm,
                   mu_ref, wst_ref, mv_ref, a16r_ref, a16i_ref, mod_ref,
                   kt_ref, khl_ref, vnat_ref, cp_ref, act_ref, wstage_ref, sem, *, n_rows):
    n_slots = wstage_ref.shape[0]
    mod_blk = wstage_ref.shape[2]
    ct = ct_ref[...]
    act_ref[...] = ct * jax.nn.sigmoid(ct)

    def mod_copy(i):
        c0 = pl.multiple_of(i * mod_blk, LANE)
        slot = i % n_slots
        return pltpu.make_async_copy(wmod_hbm.at[0, :, pl.ds(c0, mod_blk)], wstage_ref.at[slot], sem.at[slot])

    for i in range(n_slots - 1):
        mod_copy(i).start()

    def pair_body(q, carry):
        mod_copy(q).wait()

        @pl.when(q + n_slots - 1 < PAIRS)
        def _refill():
            mod_copy(q + n_slots - 1).start()

        w = wstage_ref[q % n_slots]
        for r in range(n_rows):
            mod_ref[r, q] = jnp.sum(w * act_ref[:, r:r + 1], axis=0, keepdims=True) + bmod_ref[q]
        for r in range(n_rows, mod_ref.shape[0]):
            mod_ref[r, q] = jnp.zeros((1, mod_blk), _F32)

        for d in range(2):
            ar = a_ref[0, d, q]
            ai = a_ref[1, d, q]
            dt = jnp.exp(a_ref[2, d, q])
            mag = jnp.exp(ar * dt)
            ang = ai * dt
            abr, abi = mag * jnp.cos(ang), mag * jnp.sin(ang)
            den = ar * ar + ai * ai
            fr = ((abr - 1.0) * ar + abi * ai) / den
            fi = (abi * ar - (abr - 1.0) * ai) / den
            bbr, bbi = _cmul(bt_ref[0, d, q], bt_ref[1, d, q], fr, fi)
            cr = c_ref[0, d, q]
            ci = c_ref[1, d, q]
            pw = [(jnp.ones_like(ar), jnp.zeros_like(ar))]
            for _ in range(CHUNK):
                pw.append(_cmul(pw[-1][0], pw[-1][1], abr, abi))
            lanes_re = slice(2 * d * LANE, (2 * d + 1) * LANE)
            lanes_im = slice((2 * d + 1) * LANE, (2 * d + 2) * LANE)
            for lanes in (lanes_re, lanes_im):
                a16r_ref[q, :, lanes] = pw[CHUNK][0]
                a16i_ref[q, :, lanes] = pw[CHUNK][1]
            for sg in range(CHUNK):
                e = (CHUNK - 1 - sg) if d == 0 else sg
                wr, wi = _cmul(bbr, bbi, pw[e][0], pw[e][1])
                e = (sg + 1) if d == 0 else (CHUNK - sg)
                vr, vi = _cmul(cr, ci, pw[e][0], pw[e][1])
                for gg in range(2):
                    src = slice(gg * GROUP_CH, (gg + 1) * GROUP_CH)
                    dst = slice(gg * FLAT + sg * GROUP_CH, gg * FLAT + (sg + 1) * GROUP_CH)
                    wst_ref[q, dst, lanes_re] = wr[src].astype(_BF16)
                    wst_ref[q, dst, lanes_im] = wi[src].astype(_BF16)
                    vnat_ref[2 * d, dst, :] = vr[src]
                    vnat_ref[2 * d + 1, dst, :] = -vi[src]
            for gg in range(2):
                src = slice(gg * GROUP_CH, (gg + 1) * GROUP_CH)
                for k in range(CHUNK):
                    e = k if d == 0 else (CHUNK - 1 - k)
                    pr, pi = _cmul(cr[src], ci[src], pw[e][0], pw[e][1])
                    cp_ref[0, k * GROUP_CH:(k + 1) * GROUP_CH, :] = pr
                    cp_ref[1, k * GROUP_CH:(k + 1) * GROUP_CH, :] = pi
                kt = _dot3_nt(bbr[src], cp_ref[0]) - _dot3_nt(bbi[src], cp_ref[1])
                r0 = pl.multiple_of((2 * q + gg) * GROUP_CH, GROUP_CH)
                kt_ref[d, pl.ds(r0, GROUP_CH), :] = kt
        for part in range(4):
            mv_ref[q, part * LANE:(part + 1) * LANE, :] = vnat_ref[part].T.astype(_BF16)
        return carry

    lax.fori_loop(0, PAIRS, pair_body, 0, unroll=2)

    for d in range(2):
        khl_ref[2 * d], khl_ref[2 * d + 1] = _split_bf16(kt_ref[d])

    row = lax.broadcasted_iota(jnp.int32, (FLAT, FLAT), 0)
    col = lax.broadcasted_iota(jnp.int32, (FLAT, FLAT), 1)
    same_ch = (row % GROUP_CH) == (col % GROUP_CH)
    row_blk = row // GROUP_CH
    col_blk = col // GROUP_CH
    orow = lax.broadcasted_iota(jnp.int32, (GROUPS * GROUP_CH, FLAT), 0)
    ocol = lax.broadcasted_iota(jnp.int32, (GROUPS * GROUP_CH, FLAT), 1)
    skip_ch = (orow % GROUP_CH) == (ocol % GROUP_CH)
    ocol_blk = ocol // GROUP_CH
    d_col = d_ref[...]

    def toeplitz_body(sg, carry):
        sf = jnp.where(same_ch & (row_blk + sg == col_blk), 1.0, 0.0).astype(_BF16)
        sb = jnp.where(same_ch & (row_blk == col_blk + (CHUNK - 1) - sg), 1.0, 0.0).astype(_BF16)
        out = (_dot(khl_ref[0], sf) + _dot(khl_ref[1], sf)) + (_dot(khl_ref[2], sb) + _dot(khl_ref[3], sb))
        out = out + jnp.where(skip_ch & (ocol_blk == sg), d_col, 0.0)
        r0 = pl.multiple_of(sg * GROUP_CH, GROUP_CH)
        mu_ref[:, pl.ds(r0, GROUP_CH), :] = out.reshape(GROUPS, GROUP_CH, FLAT).astype(_BF16)
        return carry

    lax.fori_loop(0, CHUNK, toeplitz_body, 0, unroll=4)


def _s5_tables(a_re, a_im, log_dt, b_re, b_im, c_re, c_im, d_skip, cc_t, w_mod, b_mod, n_rows):
    f32 = _F32
    n_mod = w_mod.shape[-1]
    mod_blk = n_mod // PAIRS
    assert n_mod % PAIRS == 0 and mod_blk % LANE == 0
    eye2 = jnp.eye(2, dtype=f32)

    def pair_blocks(re, im):
        v = jnp.stack([re, im]).astype(f32).reshape(2, 2, PAIRS, 2, GROUP_CH, 1, STATE)
        v = v * eye2[None, None, None, :, None, :, None]
        return v.reshape(2, 2, PAIRS, 2 * GROUP_CH, 2 * STATE)

    ldt = jnp.broadcast_to(log_dt.astype(f32)[..., None], (2, GROUPS, STATE))
    a_rows = jnp.stack([a_re.astype(f32), a_im.astype(f32), ldt]).reshape(3, 2, PAIRS, 1, 2 * STATE)
    args = (a_rows, pair_blocks(jnp.swapaxes(b_re, 2, 3), jnp.swapaxes(b_im, 2, 3)),
            pair_blocks(c_re, c_im), d_skip.astype(f32).reshape(D_SSM, 1),
            cc_t, b_mod.reshape(PAIRS, 1, mod_blk))
    whole = lambda a: pl.BlockSpec(a.shape, lambda i, n=a.ndim: (0,) * n)
    out_shape = [
        jax.ShapeDtypeStruct((GROUPS, FLAT, FLAT), _BF16),
        jax.ShapeDtypeStruct((PAIRS, 2 * FLAT, PAIR_LANES), _BF16),
        jax.ShapeDtypeStruct((PAIRS, PAIR_LANES, 2 * FLAT), _BF16),
        jax.ShapeDtypeStruct((PAIRS, 1, PAIR_LANES), _F32),
        jax.ShapeDtypeStruct((PAIRS, 1, PAIR_LANES), _F32),
        jax.ShapeDtypeStruct((SUBLANES, PAIRS, 1, mod_blk), _F32),
    ]
    *tables, mod = pl.pallas_call(
        functools.partial(_tables_kernel, n_rows=n_rows),
        grid=(1,),
        in_specs=[whole(a) for a in args] + [pl.BlockSpec(memory_space=pl.ANY)],
        out_specs=[whole(s) for s in out_shape],
        out_shape=out_shape,
        scratch_shapes=[
            pltpu.VMEM((2, GROUPS * GROUP_CH, FLAT), _F32),
            pltpu.VMEM((4, GROUPS * GROUP_CH, FLAT), _BF16),
            pltpu.VMEM((4, 2 * FLAT, LANE), _F32),
            pltpu.VMEM((2, FLAT, LANE), _F32),
            pltpu.VMEM((D_MODEL, SUBLANES), _F32),
            pltpu.VMEM((WEIGHT_STAGE_SLOTS, D_MODEL, mod_blk), _F32),
            pltpu.SemaphoreType.DMA((WEIGHT_STAGE_SLOTS,)),
        ],
        compiler_params=pltpu.CompilerParams(
            dimension_semantics=("arbitrary",), vmem_limit_bytes=VMEM_LIMIT),
        name="s5_tables_adaln",
    )(*args, w_mod)
    return tables, mod.reshape(SUBLANES, 1, n_mod)


def kernel(x, c, ctx, c_ctx, w_mod, b_mod, norm1_g, norm2_g, w_in, s5_a_re, s5_a_im, s5_log_dt,
           s5_b_re, s5_b_im, s5_c_re, s5_c_im, s5_d, w_glu, b_glu, pool_w, pool_scale,
           w_branch_a, w_branch_b, w_out, w_ffn_in, w_ffn_out, final_norm_g):
    bsz, n_tok, d = x.shape
    ctx_len = ctx.shape[1]
    assert d == D_MODEL and w_mod.shape[0] == 1 and bsz + 1 <= SUBLANES
    assert n_tok % SCAN_ROWS == 0 and n_tok % MIX_ROWS == 0 and MIX_ROWS % GRID_W == 0
    assert bsz * ctx_len <= SCAN_ROWS

    cc_t = jnp.concatenate(
        [c.T, c_ctx[:, None], jnp.zeros((D_MODEL, SUBLANES - bsz - 1), _F32)], axis=1)
    (mu, wst, mv, a16_re, a16_im), mod3 = _s5_tables(
        s5_a_re[0], s5_a_im[0], s5_log_dt[0], s5_b_re[0], s5_b_im[0], s5_c_re[0], s5_c_im[0], s5_d[0],
        cc_t, w_mod, b_mod, bsz + 1)

    seed_f, seed_b = _pass1_ctx(ctx.reshape(1, bsz * ctx_len, D_MODEL), mod3, bsz, norm1_g, w_in, wst,
                                a16_re, a16_im, bsz)
    uflat, x_fwd, s_bwd = _pass1(x, mod3, norm1_g, w_in, wst, a16_re, a16_im, seed_f)
    ypre = _readout(uflat, x_fwd, s_bwd, seed_b, a16_re, a16_im, mu, mv)

    x1 = _mixer(x, ypre, mod3, norm1_g, b_glu, pool_scale, w_in, w_glu, w_branch_a, pool_w, w_branch_b, w_out)
    return _ffn(x1, mod3, norm2_g, final_norm_g.reshape(1, D_MODEL), w_ffn_in, w_ffn_out)
```

```python
import functools

import numpy as np
import jax
import jax.numpy as jnp
from jax import lax
from jax.experimental import pallas as pl
from jax.experimental.pallas import tpu as pltpu

_F32 = jnp.float32
_BF16 = jnp.bfloat16

D_MODEL = 1024
D_SSM = 512
D_POOL = 512
GROUPS = 32
STATE = 64
GROUP_CH = 16
CHUNK = 16
FLAT = CHUNK * GROUP_CH
PAIRS = GROUPS // 2
PAIR_LANES = 4 * 2 * STATE
LANE = 128
SUBLANES = 8
CHUNK_PITCH = 24
COL_BLOCKS = D_SSM // LANE
GRID_W = 64
POOL_WINDOWS = (2, 4, 8, 16)
POOL_GROUP_CH = D_POOL // len(POOL_WINDOWS)
FFN_HIDDEN = 2816
RMS_EPS = 1e-6

SCAN_TILE = 128
SCAN_ROWS = SCAN_TILE * CHUNK
NORM_ROWS = 512
MIX_ROWS = 1024
FFN_ROWS = 1024
FFN_CHUNKS = ((0, 1536), (1536, FFN_HIDDEN))
RELAYOUT_UNROLL = 16
WEIGHT_STAGE_ROWS = 64
WEIGHT_STAGE_SLOTS = 3
VMEM_LIMIT = 56 * 1024 * 1024


def _rms_mod(x, gain, sh):
    ms = jnp.mean(x * x, axis=-1, keepdims=True)
    return (x * lax.rsqrt(ms + RMS_EPS)) * gain + sh


def _dot(a, b):
    return jnp.dot(a, b, preferred_element_type=_F32)


def _const_spec(shape, index=None):
    index = (0,) * len(shape) if index is None else index
    return pl.BlockSpec(shape, lambda *_: index, pipeline_mode=pl.Buffered(1))


def _to_pitch(v):
    n = v.shape[0] // CHUNK
    v = v.reshape(n, CHUNK, v.shape[1])
    pad = jnp.zeros((n, CHUNK_PITCH - CHUNK, v.shape[2]), v.dtype)
    return jnp.concatenate([v, pad], axis=1).reshape(n * CHUNK_PITCH, v.shape[2])


def _from_pitch(v):
    n = v.shape[0] // CHUNK_PITCH
    return v.reshape(n, CHUNK_PITCH, v.shape[1])[:, :CHUNK, :].reshape(n * CHUNK, v.shape[1])


def _scan_constants(ar_ref, ai_ref, c_ref, dirs):
    rb = SUBLANES
    rid = lax.broadcasted_iota(jnp.int32, (rb, LANE), 0)

    def body(q, carry):
        for d in dirs:
            o = d * 2 * LANE
            a_r = ar_ref[q][:, o:o + LANE]
            a_i = ai_ref[q][:, o:o + LANE]
            pows = [(a_r, a_i)]
            for _ in range(rb - 1):
                pows.append(_cmul(pows[-1][0], pows[-1][1], a_r, a_i))
            for k, shift in enumerate((1, 2, 4)):
                keep = (rid >= shift) if d == 0 else (rid < rb - shift)
                c_ref[d, q, 2 * k] = jnp.where(keep, pows[shift - 1][0], 0.0)
                c_ref[d, q, 2 * k + 1] = jnp.where(keep, pows[shift - 1][1], 0.0)
            p_r = jnp.zeros((rb, LANE), _F32)
            p_i = jnp.zeros((rb, LANE), _F32)
            for r in range(rb):
                e = r if d == 0 else rb - 1 - r
                p_r = jnp.where(rid == r, pows[e][0], p_r)
                p_i = jnp.where(rid == r, pows[e][1], p_i)
            c_ref[d, q, 6] = p_r
            c_ref[d, q, 7] = p_i
        return carry

    lax.fori_loop(0, PAIRS, body, 0)


def _scan_block(c_ref, d, q, s_re, s_im, xin_re, xin_im):
    rb = SUBLANES
    rid = lax.broadcasted_iota(jnp.int32, (rb, LANE), 0)

    def shifted(v, k):
        return pltpu.roll(v, k if d == 0 else rb - k, 0)

    t_re, t_im = s_re, s_im
    for k in range(3):
        a_r = c_ref[d, q, 2 * k]
        a_i = c_ref[d, q, 2 * k + 1]
        u_re, u_im = shifted(t_re, 1 << k), shifted(t_im, 1 << k)
        t_re, t_im = t_re + (a_r * u_re - a_i * u_im), t_im + (a_r * u_im + a_i * u_re)
    p_r = c_ref[d, q, 6]
    p_i = c_ref[d, q, 7]
    after_re = t_re + (p_r * xin_re - p_i * xin_im)
    after_im = t_im + (p_r * xin_im + p_i * xin_re)
    first = 0 if d == 0 else rb - 1
    last = rb - 1 - first
    start_re = jnp.where(rid == first, xin_re, shifted(after_re, 1))
    start_im = jnp.where(rid == first, xin_im, shifted(after_im, 1))
    return start_re, start_im, after_re[last:last + 1], after_im[last:last + 1]


def _scan_rows(c_ref, d, q, s, xin, blocks):
    rb = SUBLANES
    x_re, x_im = xin[:, :LANE], xin[:, LANE:]
    starts = {}
    for blk in (blocks if d == 0 else blocks[::-1]):
        r = slice(blk * rb, (blk + 1) * rb)
        st_re, st_im, x_re, x_im = _scan_block(c_ref, d, q, s[r, :LANE], s[r, LANE:], x_re, x_im)
        starts[blk] = (st_re, st_im)
    return starts, jnp.concatenate([x_re, x_im], axis=1)


def _starts_to_rows(starts, blocks):
    return jnp.concatenate(
        [jnp.concatenate([starts[b][0] for b in blocks], axis=0),
         jnp.concatenate([starts[b][1] for b in blocks], axis=0)], axis=1)


def _p1_project(x_ref, sh_ref, sc_ref, g_ref, wa_ref, h_ref, u_ref, ut_ref, rows):
    gain = g_ref[...] * (1.0 + sc_ref[0])
    sh = sh_ref[0]
    wa = wa_ref[...].astype(_BF16)
    block_chunks = NORM_ROWS // CHUNK

    def norm_block(i):
        r = slice(i * NORM_ROWS, (i + 1) * NORM_ROWS)
        h_ref[r, :] = _rms_mod(x_ref[0, r, :], gain, sh).astype(_BF16)

    norm_block(0)
    for i in range(rows // NORM_ROWS):
        if (i + 1) * NORM_ROWS < rows:
            norm_block(i + 1)
        u = _dot(h_ref[i * NORM_ROWS:(i + 1) * NORM_ROWS, :], wa)
        pr = slice(i * block_chunks * CHUNK_PITCH, (i + 1) * block_chunks * CHUNK_PITCH)
        for cb in range(COL_BLOCKS):
            u_ref[cb, pr, :] = _to_pitch(u[:, cb * LANE:(cb + 1) * LANE])
    if rows < SCAN_ROWS:
        first = rows // CHUNK * CHUNK_PITCH
        for cb in range(COL_BLOCKS):
            u_ref[cb, first:, :] = jnp.zeros((SCAN_TILE * CHUNK_PITCH - first, LANE), _F32)

    def slab_body(sg, carry):
        r0 = pl.multiple_of(sg * GROUP_CH, GROUP_CH)
        for cb in range(COL_BLOCKS):
            slab = u_ref[cb, pl.ds(sg, SCAN_TILE, stride=CHUNK_PITCH), :]
            ut_ref[cb * 8:(cb + 1) * 8, pl.ds(r0, GROUP_CH), :] = (
                slab.astype(_BF16).T.reshape(8, GROUP_CH, SCAN_TILE))
        return carry

    lax.fori_loop(0, CHUNK, slab_body, 0, unroll=RELAYOUT_UNROLL)


def _pair_states(ut_ref, wst_ref, q):
    uf0 = ut_ref[2 * q].T
    uf1 = ut_ref[2 * q + 1].T
    return uf0, uf1, _dot(uf0, wst_ref[q, :FLAT, :]) + _dot(uf1, wst_ref[q, FLAT:, :])


def _p1_kernel(x_ref, ctx_ref, sh_ref, sc_ref, g_ref, wa_ref, wst_ref, ar_ref, ai_ref,
               uflat_ref, xf_ref, sb_ref, seedb_ref, h_ref, u_ref, ut_ref, c_ref, carry_ref, seedf_ref,
               *, bsz, nt, ctx_rows):
    step = pl.program_id(0)
    half = PAIR_LANES // 2

    @pl.when(step == 0)
    def _context():
        _scan_constants(ar_ref, ai_ref, c_ref, (0, 1))
        _p1_project(ctx_ref, sh_ref, sc_ref, g_ref, wa_ref, h_ref, u_ref, ut_ref, ctx_rows)
        per_seq = ctx_rows // CHUNK // SUBLANES // bsz
        zero = jnp.zeros((1, half), _F32)

        def pair_body(q, carry):
            _, _, s = _pair_states(ut_ref, wst_ref, q)
            for b in range(bsz):
                blocks = list(range(b * per_seq, (b + 1) * per_seq))
                _, seedf_ref[b, q] = _scan_rows(c_ref, 0, q, s[:, :half], zero, blocks)
                _, seedb_ref[b, q] = _scan_rows(c_ref, 1, q, s[:, half:], zero, blocks)
            return carry

        lax.fori_loop(0, PAIRS, pair_body, 0, unroll=RELAYOUT_UNROLL)

    @pl.when(step > 0)
    def _latent():
        tile = step - 1

        @pl.when(tile % nt == 0)
        def _seed():
            carry_ref[...] = seedf_ref[tile // nt]

        _p1_project(x_ref, sh_ref, sc_ref, g_ref, wa_ref, h_ref, u_ref, ut_ref, SCAN_ROWS)
        blocks = list(range(SCAN_TILE // SUBLANES))

        def pair_body(q, carry):
            uf0, uf1, s = _pair_states(ut_ref, wst_ref, q)
            uflat_ref[0, 2 * q] = uf0
            uflat_ref[0, 2 * q + 1] = uf1
            starts, x_out = _scan_rows(c_ref, 0, q, s[:, :half], carry_ref[q], blocks)
            xf_ref[0, q] = _starts_to_rows(starts, blocks).astype(_BF16)
            carry_ref[q] = x_out
            sb_ref[0, q] = s[:, half:]
            return carry

        lax.fori_loop(0, PAIRS, pair_body, 0, unroll=RELAYOUT_UNROLL)


def _pass1(x, ctx_rows, mod3, norm_g, w_in, wst, a16_re, a16_im):
    bsz, n_tok, _ = x.shape
    rows_c = ctx_rows.shape[1]
    assert n_tok % SCAN_ROWS == 0
    assert rows_c % NORM_ROWS == 0 and rows_c <= SCAN_ROWS and rows_c % (bsz * CHUNK * SUBLANES) == 0
    nt = n_tok // SCAN_ROWS
    n_chunks = nt * SCAN_TILE
    half = PAIR_LANES // 2
    tile = lambda s: jnp.maximum(s - 1, 0)
    mod_row = lambda s: jnp.where(s == 0, bsz, tile(s) // nt)
    by_tile = lambda s: (tile(s) // nt, 0, tile(s) % nt, 0)
    return pl.pallas_call(
        functools.partial(_p1_kernel, bsz=bsz, nt=nt, ctx_rows=rows_c),
        grid=(bsz * nt + 1,),
        in_specs=[
            pl.BlockSpec((1, SCAN_ROWS, D_MODEL), lambda s: (tile(s) // nt, tile(s) % nt, 0)),
            _const_spec((1, rows_c, D_MODEL)),
            pl.BlockSpec((1, 1, D_MODEL), lambda s: (mod_row(s), 0, 0)),
            pl.BlockSpec((1, 1, D_MODEL), lambda s: (mod_row(s), 0, 1)),
            _const_spec((1, D_MODEL)),
            _const_spec((None, D_MODEL, D_SSM)),
            _const_spec((PAIRS, 2 * FLAT, PAIR_LANES)),
            _const_spec((PAIRS, 1, PAIR_LANES)),
            _const_spec((PAIRS, 1, PAIR_LANES)),
        ],
        out_specs=[
            pl.BlockSpec((1, GROUPS, SCAN_TILE, FLAT), by_tile),
            pl.BlockSpec((1, PAIRS, SCAN_TILE, half), by_tile),
            pl.BlockSpec((1, PAIRS, SCAN_TILE, half), by_tile),
            pl.BlockSpec((bsz, PAIRS, 1, half), lambda s: (0, 0, 0, 0)),
        ],
        out_shape=[
            jax.ShapeDtypeStruct((bsz, GROUPS, n_chunks, FLAT), _BF16),
            jax.ShapeDtypeStruct((bsz, PAIRS, n_chunks, half), _BF16),
            jax.ShapeDtypeStruct((bsz, PAIRS, n_chunks, half), _F32),
            jax.ShapeDtypeStruct((bsz, PAIRS, 1, half), _F32),
        ],
        scratch_shapes=[
            pltpu.VMEM((SCAN_ROWS, D_MODEL), _BF16),
            pltpu.VMEM((COL_BLOCKS, SCAN_TILE * CHUNK_PITCH, LANE), _F32),
            pltpu.VMEM((GROUPS, FLAT, SCAN_TILE), _BF16),
            pltpu.VMEM((2, PAIRS, 8, SUBLANES, LANE), _F32),
            pltpu.VMEM((PAIRS, 1, half), _F32),
            pltpu.VMEM((bsz, PAIRS, 1, half), _F32),
        ],
        compiler_params=pltpu.CompilerParams(
            dimension_semantics=("arbitrary",), vmem_limit_bytes=VMEM_LIMIT),
        name="s5_chunk_states",
    )(x, ctx_rows, mod3, mod3, norm_g, w_in, wst, a16_re, a16_im)


def _readout_kernel(uflat_ref, xf_ref, sb_ref, seed_ref, ar_ref, ai_ref, mu_ref, mv_ref, y_ref,
                    yt_ref, ys_ref, c_ref, carry_ref):
    @pl.when(_first_step())
    def _constants():
        _scan_constants(ar_ref, ai_ref, c_ref, (1,))

    @pl.when(pl.program_id(1) == 0)
    def _seed():
        carry_ref[...] = seed_ref[0]

    blocks = list(range(SCAN_TILE // SUBLANES))

    def pair_body(q, carry):
        y0 = _dot(uflat_ref[0, 2 * q], mu_ref[2 * q])
        y1 = _dot(uflat_ref[0, 2 * q + 1], mu_ref[2 * q + 1])
        starts, x_out = _scan_rows(c_ref, 1, q, sb_ref[0, q], carry_ref[q], blocks)
        carry_ref[q] = x_out
        xs = jnp.concatenate([xf_ref[0, q], _starts_to_rows(starts, blocks).astype(_BF16)], axis=1)
        yx = _dot(xs, mv_ref[q])
        y = jnp.concatenate([y0, y1], axis=1) + yx
        yt = y.astype(yt_ref.dtype).T
        yt_ref[2 * q] = yt[:FLAT]
        yt_ref[2 * q + 1] = yt[FLAT:]
        return carry

    lax.fori_loop(0, PAIRS, pair_body, 0, unroll=RELAYOUT_UNROLL)

    def slab_body(sg, carry):
        r0 = pl.multiple_of(sg * GROUP_CH, GROUP_CH)
        for cb in range(COL_BLOCKS):
            yt = yt_ref[cb * 8:(cb + 1) * 8, pl.ds(r0, GROUP_CH), :].reshape(LANE, SCAN_TILE)
            ys_ref[cb, pl.ds(sg, SCAN_TILE, stride=CHUNK_PITCH), :] = yt.T.astype(_F32)
        return carry

    lax.fori_loop(0, CHUNK, slab_body, 0, unroll=RELAYOUT_UNROLL)
    for cb in range(COL_BLOCKS):
        y_ref[0, :, cb * LANE:(cb + 1) * LANE] = _from_pitch(ys_ref[cb]).astype(y_ref.dtype)


def _readout(uflat, xf, sb, seed_b, a16_re, a16_im, mu, mv):
    bsz, _, n_chunks, _ = uflat.shape
    nt = n_chunks // SCAN_TILE
    half = PAIR_LANES // 2
    rev = lambda b, t: (b, 0, nt - 1 - t, 0)
    return pl.pallas_call(
        _readout_kernel,
        grid=(bsz, nt),
        in_specs=[
            pl.BlockSpec((1, GROUPS, SCAN_TILE, FLAT), rev),
            pl.BlockSpec((1, PAIRS, SCAN_TILE, half), rev),
            pl.BlockSpec((1, PAIRS, SCAN_TILE, half), rev),
            pl.BlockSpec((1, PAIRS, 1, half), lambda b, t: (b, 0, 0, 0)),
            _const_spec((PAIRS, 1, PAIR_LANES)),
            _const_spec((PAIRS, 1, PAIR_LANES)),
            _const_spec((GROUPS, FLAT, FLAT)),
            _const_spec((PAIRS, PAIR_LANES, 2 * FLAT)),
        ],
        out_specs=pl.BlockSpec((1, SCAN_ROWS, D_SSM), lambda b, t: (b, nt - 1 - t, 0)),
        out_shape=jax.ShapeDtypeStruct((bsz, n_chunks * CHUNK, D_SSM), _BF16),
        scratch_shapes=[
            pltpu.VMEM((GROUPS, FLAT, SCAN_TILE), _BF16),
            pltpu.VMEM((COL_BLOCKS, SCAN_TILE * CHUNK_PITCH, LANE), _F32),
            pltpu.VMEM((2, PAIRS, 8, SUBLANES, LANE), _F32),
            pltpu.VMEM((PAIRS, 1, half), _F32),
        ],
        compiler_params=pltpu.CompilerParams(
            dimension_semantics=("arbitrary", "arbitrary"), vmem_limit_bytes=VMEM_LIMIT),
        name="s5_readout",
    )(uflat, xf, sb, seed_b, a16_re, a16_im, mu, mv)


def _window_sum(u, w):
    assert w // 2 <= SUBLANES
    rows, lanes = u.shape
    nb = rows // GRID_W
    pad = jnp.zeros((nb, SUBLANES, lanes), _F32)
    z = jnp.concatenate([pad, u.reshape(nb, GRID_W, lanes), pad], axis=1)
    n = nb * (GRID_W + 2 * SUBLANES)
    z = z.reshape(n, lanes)
    acc = z + pltpu.roll(z, 1, 0)
    m = 2
    while m < w:
        acc = pltpu.roll(acc, m // 2, 0) + pltpu.roll(acc, n - m // 2, 0)
        m *= 2
    return acc.reshape(nb, GRID_W + 2 * SUBLANES, lanes)[:, SUBLANES:SUBLANES + GRID_W, :].reshape(rows, lanes)


def _mix_kernel(x_ref, y_ref, sh_ref, sc_ref, gt_ref, g_ref, bglu_ref, pinv_ref, ps_ref, pw_ref,
                win_hbm, wglu_hbm, wa_hbm, wb_hbm, wo_hbm, o_ref,
                wr_ref, wglu_ref, wa_ref, wb_ref, wo_ref, stage_ref, sem):
    @pl.when(_first_step())
    def _load_weights():
        r = stage_ref.shape[1]
        n_rest = D_POOL + 2 * D_MODEL

        def pieces(src3, dst, n_rows, c0, n_cols):
            return [(src3.at[0, pl.ds(k * r, r), pl.ds(c0, n_cols)],
                     [(dst.at[pl.ds(k * r, r), :], slice(0, n_cols))]) for k in range(n_rows // r)]

        jobs = pieces(win_hbm, wr_ref, D_MODEL, D_SSM, n_rest)
        jobs += pieces(wglu_hbm, wglu_ref, D_SSM, 0, D_SSM)
        jobs += pieces(wa_hbm, wa_ref, D_SSM, 0, D_MODEL)
        jobs += pieces(wb_hbm, wb_ref, D_POOL, 0, D_MODEL)
        jobs += pieces(wo_hbm, wo_ref, D_MODEL, 0, D_MODEL)
        _stream_cast(jobs, stage_ref, sem)

    x = x_ref[0]
    h = _rms_mod(x, g_ref[...] * (1.0 + sc_ref[0]), sh_ref[0]).astype(_BF16)
    c_pool, c_ga, c_gb = 0, D_POOL, D_POOL + D_MODEL
    windows = range(len(POOL_WINDOWS))
    group = lambda wi: slice(wi * POOL_GROUP_CH, (wi + 1) * POOL_GROUP_CH)

    ub = _dot(h, wr_ref[:, c_pool:c_ga])
    y = jax.nn.gelu(y_ref[0].astype(_F32))
    glu = _dot(y.astype(_BF16), wglu_ref[...])
    wsums = [_window_sum(ub[:, group(wi)], POOL_WINDOWS[wi]) for wi in windows]
    gate_a = _dot(h, wr_ref[:, c_ga:c_gb])
    z = y * jax.nn.sigmoid(glu + bglu_ref[...])
    ya = _dot(z.astype(_BF16), wa_ref[...])
    outs = [_dot((wsums[wi] * pinv_ref[wi] - ub[:, group(wi)]).astype(_BF16), pw_ref[wi].astype(_BF16))
            for wi in windows]
    gate_b = _dot(h, wr_ref[:, c_gb:])
    pb = jnp.concatenate(outs, axis=1) * ps_ref[...]
    yb = _dot(pb.astype(_BF16), wb_ref[...])

    merged = jax.nn.sigmoid(gate_a) * ya + jax.nn.sigmoid(gate_b) * yb
    mixed = _dot(merged.astype(_BF16), wo_ref[...])
    o_ref[0] = x + gt_ref[0] * mixed


def _pool_inverse_counts(rows):
    pos = np.arange(rows) % GRID_W
    invs = []
    for w in POOL_WINDOWS:
        lo = np.clip(pos - w // 2, 0, GRID_W - 1)
        hi = np.clip(pos + w - 1 - w // 2, 0, GRID_W - 1) + 1
        invs.append(np.broadcast_to((1.0 / (hi - lo).astype(np.float32))[:, None], (rows, POOL_GROUP_CH)))
    return np.stack(invs)


def _mixer(x, ypre, mod3, norm_g, b_glu, pool_scale, w_in, w_glu, w_a, pool_w, w_b, w_out):
    bsz, n_tok, _ = x.shape
    tm = MIX_ROWS
    pinv = jnp.asarray(_pool_inverse_counts(tm), _F32)
    nw = len(POOL_WINDOWS)
    n_rest = D_POOL + 2 * D_MODEL
    hbm = pl.BlockSpec(memory_space=pl.ANY)
    return pl.pallas_call(
        _mix_kernel,
        grid=(bsz, n_tok // tm),
        in_specs=[
            pl.BlockSpec((1, tm, D_MODEL), lambda b, t: (b, t, 0)),
            pl.BlockSpec((1, tm, D_SSM), lambda b, t: (b, t, 0)),
            pl.BlockSpec((1, 1, D_MODEL), lambda b, t: (b, 0, 0)),
            pl.BlockSpec((1, 1, D_MODEL), lambda b, t: (b, 0, 1)),
            pl.BlockSpec((1, 1, D_MODEL), lambda b, t: (b, 0, 2)),
            _const_spec((1, D_MODEL)),
            _const_spec((1, D_SSM)),
            _const_spec((nw, tm, POOL_GROUP_CH)),
            _const_spec((1, D_POOL)),
            _const_spec((None, nw, POOL_GROUP_CH, POOL_GROUP_CH)),
            hbm, hbm, hbm, hbm, hbm,
        ],
        out_specs=pl.BlockSpec((1, tm, D_MODEL), lambda b, t: (b, t, 0)),
        out_shape=jax.ShapeDtypeStruct(x.shape, _F32),
        scratch_shapes=[
            pltpu.VMEM((D_MODEL, n_rest), _BF16),
            pltpu.VMEM((D_SSM, D_SSM), _BF16),
            pltpu.VMEM((D_SSM, D_MODEL), _BF16),
            pltpu.VMEM((D_POOL, D_MODEL), _BF16),
            pltpu.VMEM((D_MODEL, D_MODEL), _BF16),
            pltpu.VMEM((WEIGHT_STAGE_SLOTS, 4 * WEIGHT_STAGE_ROWS, n_rest), _F32),
            pltpu.SemaphoreType.DMA((WEIGHT_STAGE_SLOTS,)),
        ],
        compiler_params=pltpu.CompilerParams(
            dimension_semantics=("arbitrary", "arbitrary"), vmem_limit_bytes=VMEM_LIMIT),
        name="token_mixer",
    )(x, ypre, mod3, mod3, mod3, norm_g, b_glu, pinv, pool_scale, pool_w,
      w_in, w_glu, w_a, w_b, w_out)


def _stream_cast(jobs, stage_ref, sem):
    n_slots = stage_ref.shape[0]

    def copy(i):
        src = jobs[i][0]
        rows, cols = src.shape
        slot = i % n_slots
        return pltpu.make_async_copy(src, stage_ref.at[slot, pl.ds(0, rows), pl.ds(0, cols)], sem.at[slot])

    for i in range(min(n_slots - 1, len(jobs))):
        copy(i).start()
    for i, (src, dsts) in enumerate(jobs):
        if i + n_slots - 1 < len(jobs):
            copy(i + n_slots - 1).start()
        copy(i).wait()
        rows = src.shape[0]
        for dst, cols in dsts:
            dst[...] = stage_ref[i % n_slots, 0:rows, cols].astype(_BF16)


def _first_step():
    return jnp.logical_and(pl.program_id(0) == 0, pl.program_id(1) == 0)


def _ffn_kernel(x_ref, sh_ref, sc_ref, gt_ref, g2_ref, gf_ref, win_hbm, wout_hbm, o_ref,
                wg_ref, wu_ref, wo_ref, stage_in_ref, stage_out_ref, sem_in, sem_out):
    @pl.when(_first_step())
    def _load_weights():
        r = stage_in_ref.shape[1]
        _stream_cast([(win_hbm.at[0, pl.ds(k * r, r), :],
                       [(wg_ref.at[pl.ds(k * r, r), :], slice(0, FFN_HIDDEN)),
                        (wu_ref.at[pl.ds(k * r, r), :], slice(FFN_HIDDEN, 2 * FFN_HIDDEN))])
                      for k in range(D_MODEL // r)], stage_in_ref, sem_in)
        r = stage_out_ref.shape[1]
        _stream_cast([(wout_hbm.at[0, pl.ds(k * r, r), :], [(wo_ref.at[pl.ds(k * r, r), :], slice(0, D_MODEL))])
                      for k in range(FFN_HIDDEN // r)], stage_out_ref, sem_out)

    x = x_ref[0]
    h = _rms_mod(x, g2_ref[...] * (1.0 + sc_ref[0]), sh_ref[0]).astype(_BF16)
    down = None
    for c0, c1 in FFN_CHUNKS:
        gate = _dot(h, wg_ref[:, c0:c1])
        up = _dot(h, wu_ref[:, c0:c1])
        part = _dot((gate * jax.nn.sigmoid(gate) * up).astype(_BF16), wo_ref[c0:c1, :])
        down = part if down is None else down + part
    y = x + gt_ref[0] * down
    ms = jnp.mean(y * y, axis=-1, keepdims=True)
    o_ref[0] = (y * lax.rsqrt(ms + RMS_EPS)) * gf_ref[...]


def _ffn(x1, mod3, norm2_g, final_g, w_ffn, w_down):
    bsz, n_tok, _ = x1.shape
    tm = FFN_ROWS
    hbm = pl.BlockSpec(memory_space=pl.ANY)
    return pl.pallas_call(
        _ffn_kernel,
        grid=(bsz, n_tok // tm),
        in_specs=[
            pl.BlockSpec((1, tm, D_MODEL), lambda b, t: (b, t, 0)),
            pl.BlockSpec((1, 1, D_MODEL), lambda b, t: (b, 0, 3)),
            pl.BlockSpec((1, 1, D_MODEL), lambda b, t: (b, 0, 4)),
            pl.BlockSpec((1, 1, D_MODEL), lambda b, t: (b, 0, 5)),
            _const_spec((1, D_MODEL)),
            _const_spec((1, D_MODEL)),
            hbm,
            hbm,
        ],
        out_specs=pl.BlockSpec((1, tm, D_MODEL), lambda b, t: (b, t, 0)),
        out_shape=jax.ShapeDtypeStruct(x1.shape, _F32),
        scratch_shapes=[
            pltpu.VMEM((D_MODEL, FFN_HIDDEN), _BF16),
            pltpu.VMEM((D_MODEL, FFN_HIDDEN), _BF16),
            pltpu.VMEM((FFN_HIDDEN, D_MODEL), _BF16),
            pltpu.VMEM((WEIGHT_STAGE_SLOTS, WEIGHT_STAGE_ROWS, 2 * FFN_HIDDEN), _F32),
            pltpu.VMEM((WEIGHT_STAGE_SLOTS, FFN_HIDDEN // SUBLANES, D_MODEL), _F32),
            pltpu.SemaphoreType.DMA((WEIGHT_STAGE_SLOTS,)),
            pltpu.SemaphoreType.DMA((WEIGHT_STAGE_SLOTS,)),
        ],
        compiler_params=pltpu.CompilerParams(
            dimension_semantics=("arbitrary", "arbitrary"), vmem_limit_bytes=VMEM_LIMIT),
        name="swiglu_ffn",
    )(x1, mod3, mod3, mod3, norm2_g, final_g, w_ffn, w_down)


def _cmul(xr, xi, yr, yi):
    return xr * yr - xi * yi, xr * yi + xi * yr


def _split_bf16(v):
    hi = v.astype(_BF16)
    return hi, (v - hi.astype(_F32)).astype(_BF16)


def _dot3_nt(a, b):
    dims = (((1,), (1,)), ((), ()))
    nt = lambda p, q: lax.dot_general(p, q, dims, preferred_element_type=_F32)
    a_hi, a_lo = _split_bf16(a)
    b_hi, b_lo = _split_bf16(b)
    return nt(a_hi, b_hi) + (nt(a_hi, b_lo) + nt(a_lo, b_hi))


def _tables_kernel(a_ref, bt_ref, c_ref, d_ref, ct_ref, bmod_ref, wmod_hbm,
                   mu_ref, wst_ref, mv_ref, a16r_ref, a16i_ref, mod_ref,
                   kt_ref, khl_ref, vnat_ref, cp_ref, act_ref, wstage_ref, sem, *, n_rows):
    n_slots = wstage_ref.shape[0]
    mod_blk = wstage_ref.shape[2]
    ct = ct_ref[...]
    act_ref[...] = ct * jax.nn.sigmoid(ct)

    def mod_copy(i):
        c0 = pl.multiple_of(i * mod_blk, LANE)
        slot = i % n_slots
        return pltpu.make_async_copy(wmod_hbm.at[0, :, pl.ds(c0, mod_blk)], wstage_ref.at[slot], sem.at[slot])

    for i in range(n_slots - 1):
        mod_copy(i).start()

    def pair_body(q, carry):
        mod_copy(q).wait()

        @pl.when(q + n_slots - 1 < PAIRS)
        def _refill():
            mod_copy(q + n_slots - 1).start()

        w = wstage_ref[q % n_slots]
        for r in range(n_rows):
            mod_ref[r, q] = jnp.sum(w * act_ref[:, r:r + 1], axis=0, keepdims=True) + bmod_ref[q]
        for r in range(n_rows, mod_ref.shape[0]):
            mod_ref[r, q] = jnp.zeros((1, mod_blk), _F32)

        for d in range(2):
            ar = a_ref[0, d, q]
            ai = a_ref[1, d, q]
            dt = jnp.exp(a_ref[2, d, q])
            mag = jnp.exp(ar * dt)
            ang = ai * dt
            abr, abi = mag * jnp.cos(ang), mag * jnp.sin(ang)
            den = ar * ar + ai * ai
            fr = ((abr - 1.0) * ar + abi * ai) / den
            fi = (abi * ar - (abr - 1.0) * ai) / den
            bbr, bbi = _cmul(bt_ref[0, d, q], bt_ref[1, d, q], fr, fi)
            cr = c_ref[0, d, q]
            ci = c_ref[1, d, q]
            pw = [(jnp.ones_like(ar), jnp.zeros_like(ar))]
            for _ in range(CHUNK):
                pw.append(_cmul(pw[-1][0], pw[-1][1], abr, abi))
            lanes_re = slice(2 * d * LANE, (2 * d + 1) * LANE)
            lanes_im = slice((2 * d + 1) * LANE, (2 * d + 2) * LANE)
            for lanes in (lanes_re, lanes_im):
                a16r_ref[q, :, lanes] = pw[CHUNK][0]
                a16i_ref[q, :, lanes] = pw[CHUNK][1]
            for sg in range(CHUNK):
                e = (CHUNK - 1 - sg) if d == 0 else sg
                wr, wi = _cmul(bbr, bbi, pw[e][0], pw[e][1])
                e = (sg + 1) if d == 0 else (CHUNK - sg)
                vr, vi = _cmul(cr, ci, pw[e][0], pw[e][1])
                for gg in range(2):
                    src = slice(gg * GROUP_CH, (gg + 1) * GROUP_CH)
                    dst = slice(gg * FLAT + sg * GROUP_CH, gg * FLAT + (sg + 1) * GROUP_CH)
                    wst_ref[q, dst, lanes_re] = wr[src].astype(_BF16)
                    wst_ref[q, dst, lanes_im] = wi[src].astype(_BF16)
                    vnat_ref[2 * d, dst, :] = vr[src]
                    vnat_ref[2 * d + 1, dst, :] = -vi[src]
            for gg in range(2):
                src = slice(gg * GROUP_CH, (gg + 1) * GROUP_CH)
                for k in range(CHUNK):
                    e = k if d == 0 else (CHUNK - 1 - k)
                    pr, pi = _cmul(cr[src], ci[src], pw[e][0], pw[e][1])
                    cp_ref[0, k * GROUP_CH:(k + 1) * GROUP_CH, :] = pr
                    cp_ref[1, k * GROUP_CH:(k + 1) * GROUP_CH, :] = pi
                kt = _dot3_nt(bbr[src], cp_ref[0]) - _dot3_nt(bbi[src], cp_ref[1])
                r0 = pl.multiple_of((2 * q + gg) * GROUP_CH, GROUP_CH)
                kt_ref[d, pl.ds(r0, GROUP_CH), :] = kt
        for part in range(4):
            mv_ref[q, part * LANE:(part + 1) * LANE, :] = vnat_ref[part].T.astype(_BF16)
        return carry

    lax.fori_loop(0, PAIRS, pair_body, 0, unroll=2)

    for d in range(2):
        khl_ref[2 * d], khl_ref[2 * d + 1] = _split_bf16(kt_ref[d])

    row = lax.broadcasted_iota(jnp.int32, (FLAT, FLAT), 0)
    col = lax.broadcasted_iota(jnp.int32, (FLAT, FLAT), 1)
    same_ch = (row % GROUP_CH) == (col % GROUP_CH)
    row_blk = row // GROUP_CH
    col_blk = col // GROUP_CH
    orow = lax.broadcasted_iota(jnp.int32, (GROUPS * GROUP_CH, FLAT), 0)
    ocol = lax.broadcasted_iota(jnp.int32, (GROUPS * GROUP_CH, FLAT), 1)
    skip_ch = (orow % GROUP_CH) == (ocol % GROUP_CH)
    ocol_blk = ocol // GROUP_CH
    d_col = d_ref[...]

    def toeplitz_body(sg, carry):
        sf = jnp.where(same_ch & (row_blk + sg == col_blk), 1.0, 0.0).astype(_BF16)
        sb = jnp.where(same_ch & (row_blk == col_blk + (CHUNK - 1) - sg), 1.0, 0.0).astype(_BF16)
        out = (_dot(khl_ref[0], sf) + _dot(khl_ref[1], sf)) + (_dot(khl_ref[2], sb) + _dot(khl_ref[3], sb))
        out = out + jnp.where(skip_ch & (ocol_blk == sg), d_col, 0.0)
        r0 = pl.multiple_of(sg * GROUP_CH, GROUP_CH)
        mu_ref[:, pl.ds(r0, GROUP_CH), :] = out.reshape(GROUPS, GROUP_CH, FLAT).astype(_BF16)
        return carry

    lax.fori_loop(0, CHUNK, toeplitz_body, 0, unroll=4)


def _s5_tables(a_re, a_im, log_dt, b_re, b_im, c_re, c_im, d_skip, cc_t, w_mod, b_mod, n_rows):
    f32 = _F32
    n_mod = w_mod.shape[-1]
    mod_blk = n_mod // PAIRS
    assert n_mod % PAIRS == 0 and mod_blk % LANE == 0
    eye2 = jnp.eye(2, dtype=f32)

    def pair_blocks(re, im):
        v = jnp.stack([re, im]).astype(f32).reshape(2, 2, PAIRS, 2, GROUP_CH, 1, STATE)
        v = v * eye2[None, None, None, :, None, :, None]
        return v.reshape(2, 2, PAIRS, 2 * GROUP_CH, 2 * STATE)

    ldt = jnp.broadcast_to(log_dt.astype(f32)[..., None], (2, GROUPS, STATE))
    a_rows = jnp.stack([a_re.astype(f32), a_im.astype(f32), ldt]).reshape(3, 2, PAIRS, 1, 2 * STATE)
    args = (a_rows, pair_blocks(jnp.swapaxes(b_re, 2, 3), jnp.swapaxes(b_im, 2, 3)),
            pair_blocks(c_re, c_im), d_skip.astype(f32).reshape(D_SSM, 1),
            cc_t, b_mod.reshape(PAIRS, 1, mod_blk))
    whole = lambda a: pl.BlockSpec(a.shape, lambda i, n=a.ndim: (0,) * n)
    out_shape = [
        jax.ShapeDtypeStruct((GROUPS, FLAT, FLAT), _BF16),
        jax.ShapeDtypeStruct((PAIRS, 2 * FLAT, PAIR_LANES), _BF16),
        jax.ShapeDtypeStruct((PAIRS, PAIR_LANES, 2 * FLAT), _BF16),
        jax.ShapeDtypeStruct((PAIRS, 1, PAIR_LANES), _F32),
        jax.ShapeDtypeStruct((PAIRS, 1, PAIR_LANES), _F32),
        jax.ShapeDtypeStruct((SUBLANES, PAIRS, 1, mod_blk), _F32),
    ]
    *tables, mod = pl.pallas_call(
        functools.partial(_tables_kernel, n_rows=n_rows),
        grid=(1,),
        in_specs=[whole(a) for a in args] + [pl.BlockSpec(memory_space=pl.ANY)],
        out_specs=[whole(s) for s in out_shape],
        out_shape=out_shape,
        scratch_shapes=[
            pltpu.VMEM((2, GROUPS * GROUP_CH, FLAT), _F32),
            pltpu.VMEM((4, GROUPS * GROUP_CH, FLAT), _BF16),
            pltpu.VMEM((4, 2 * FLAT, LANE), _F32),
            pltpu.VMEM((2, FLAT, LANE), _F32),
            pltpu.VMEM((D_MODEL, SUBLANES), _F32),
            pltpu.VMEM((WEIGHT_STAGE_SLOTS, D_MODEL, mod_blk), _F32),
            pltpu.SemaphoreType.DMA((WEIGHT_STAGE_SLOTS,)),
        ],
        compiler_params=pltpu.CompilerParams(
            dimension_semantics=("arbitrary",), vmem_limit_bytes=VMEM_LIMIT),
        name="s5_tables_adaln",
    )(*args, w_mod)
    return tables, mod.reshape(SUBLANES, 1, n_mod)


def kernel(x, c, ctx, c_ctx, w_mod, b_mod, norm1_g, norm2_g, w_in, s5_a_re, s5_a_im, s5_log_dt,
           s5_b_re, s5_b_im, s5_c_re, s5_c_im, s5_d, w_glu, b_glu, pool_w, pool_scale,
           w_branch_a, w_branch_b, w_out, w_ffn_in, w_ffn_out, final_norm_g):
    bsz, n_tok, d = x.shape
    ctx_len = ctx.shape[1]
    assert d == D_MODEL and w_mod.shape[0] == 1 and bsz + 1 <= SUBLANES
    assert n_tok % SCAN_ROWS == 0 and n_tok % MIX_ROWS == 0 and MIX_ROWS % GRID_W == 0
    assert bsz * ctx_len <= SCAN_ROWS

    cc_t = jnp.concatenate(
        [c.T, c_ctx[:, None], jnp.zeros((D_MODEL, SUBLANES - bsz - 1), _F32)], axis=1)
    (mu, wst, mv, a16_re, a16_im), mod3 = _s5_tables(
        s5_a_re[0], s5_a_im[0], s5_log_dt[0], s5_b_re[0], s5_b_im[0], s5_c_re[0], s5_c_im[0], s5_d[0],
        cc_t, w_mod, b_mod, bsz + 1)

    uflat, x_fwd, s_bwd, seed_b = _pass1(x, ctx.reshape(1, bsz * ctx_len, D_MODEL), mod3, norm1_g, w_in, wst,
                                         a16_re, a16_im)
    ypre = _readout(uflat, x_fwd, s_bwd, seed_b, a16_re, a16_im, mu, mv)

    x1 = _mixer(x, ypre, mod3, norm1_g, b_glu, pool_scale, w_in, w_glu, w_branch_a, pool_w, w_branch_b, w_out)
    return _ffn(x1, mod3, norm2_g, final_norm_g.reshape(1, D_MODEL), w_ffn_in, w_ffn_out)
```

```python
import functools

import numpy as np
import jax
import jax.numpy as jnp
from jax import lax
from jax.experimental import pallas as pl
from jax.experimental.pallas import tpu as pltpu

_F32 = jnp.float32
_BF16 = jnp.bfloat16

D_MODEL = 1024
D_SSM = 512
D_POOL = 512
GROUPS = 32
STATE = 64
GROUP_CH = 16
CHUNK = 16
FLAT = CHUNK * GROUP_CH
PAIRS = GROUPS // 2
PAIR_LANES = 4 * 2 * STATE
LANE = 128
SUBLANES = 8
CHUNK_PITCH = 24
COL_BLOCKS = D_SSM // LANE
GRID_W = 64
POOL_WINDOWS = (2, 4, 8, 16)
POOL_GROUP_CH = D_POOL // len(POOL_WINDOWS)
FFN_HIDDEN = 2816
RMS_EPS = 1e-6

SCAN_TILE = 128
SCAN_ROWS = SCAN_TILE * CHUNK
NORM_ROWS = 512
MIX_ROWS = 1024
FFN_ROWS = 1024
FFN_CHUNKS = ((0, 1536), (1536, FFN_HIDDEN))
RELAYOUT_UNROLL = 16
WEIGHT_STAGE_SLOTS = 3
VMEM_LIMIT = 56 * 1024 * 1024


def _rms_mod(x, gain, sh):
    ms = jnp.mean(x * x, axis=-1, keepdims=True)
    return (x * lax.rsqrt(ms + RMS_EPS)) * gain + sh


def _dot(a, b):
    return jnp.dot(a, b, preferred_element_type=_F32)


def _const_spec(shape, index=None):
    index = (0,) * len(shape) if index is None else index
    return pl.BlockSpec(shape, lambda *_: index, pipeline_mode=pl.Buffered(1))


def _to_pitch(v):
    n = v.shape[0] // CHUNK
    v = v.reshape(n, CHUNK, v.shape[1])
    pad = jnp.zeros((n, CHUNK_PITCH - CHUNK, v.shape[2]), v.dtype)
    return jnp.concatenate([v, pad], axis=1).reshape(n * CHUNK_PITCH, v.shape[2])


def _from_pitch(v):
    n = v.shape[0] // CHUNK_PITCH
    return v.reshape(n, CHUNK_PITCH, v.shape[1])[:, :CHUNK, :].reshape(n * CHUNK, v.shape[1])


def _scan_constants(ar_ref, ai_ref, c_ref, dirs):
    rb = SUBLANES
    rid = lax.broadcasted_iota(jnp.int32, (rb, LANE), 0)

    def body(q, carry):
        for d in dirs:
            o = d * 2 * LANE
            a_r = ar_ref[q][:, o:o + LANE]
            a_i = ai_ref[q][:, o:o + LANE]
            pows = [(a_r, a_i)]
            for _ in range(rb - 1):
                pows.append(_cmul(pows[-1][0], pows[-1][1], a_r, a_i))
            for k, shift in enumerate((1, 2, 4)):
                keep = (rid >= shift) if d == 0 else (rid < rb - shift)
                c_ref[d, q, 2 * k] = jnp.where(keep, pows[shift - 1][0], 0.0)
                c_ref[d, q, 2 * k + 1] = jnp.where(keep, pows[shift - 1][1], 0.0)
            p_r = jnp.zeros((rb, LANE), _F32)
            p_i = jnp.zeros((rb, LANE), _F32)
            for r in range(rb):
                e = r if d == 0 else rb - 1 - r
                p_r = jnp.where(rid == r, pows[e][0], p_r)
                p_i = jnp.where(rid == r, pows[e][1], p_i)
            c_ref[d, q, 6] = p_r
            c_ref[d, q, 7] = p_i
        return carry

    lax.fori_loop(0, PAIRS, body, 0)


def _scan_block(c_ref, d, q, s_re, s_im, xin_re, xin_im):
    rb = SUBLANES
    rid = lax.broadcasted_iota(jnp.int32, (rb, LANE), 0)

    def shifted(v, k):
        return pltpu.roll(v, k if d == 0 else rb - k, 0)

    t_re, t_im = s_re, s_im
    for k in range(3):
        a_r = c_ref[d, q, 2 * k]
        a_i = c_ref[d, q, 2 * k + 1]
        u_re, u_im = shifted(t_re, 1 << k), shifted(t_im, 1 << k)
        t_re, t_im = t_re + (a_r * u_re - a_i * u_im), t_im + (a_r * u_im + a_i * u_re)
    p_r = c_ref[d, q, 6]
    p_i = c_ref[d, q, 7]
    after_re = t_re + (p_r * xin_re - p_i * xin_im)
    after_im = t_im + (p_r * xin_im + p_i * xin_re)
    first = 0 if d == 0 else rb - 1
    last = rb - 1 - first
    start_re = jnp.where(rid == first, xin_re, shifted(after_re, 1))
    start_im = jnp.where(rid == first, xin_im, shifted(after_im, 1))
    return start_re, start_im, after_re[last:last + 1], after_im[last:last + 1]


def _scan_rows(c_ref, d, q, s, xin, blocks):
    rb = SUBLANES
    x_re, x_im = xin[:, :LANE], xin[:, LANE:]
    starts = {}
    for blk in (blocks if d == 0 else blocks[::-1]):
        r = slice(blk * rb, (blk + 1) * rb)
        st_re, st_im, x_re, x_im = _scan_block(c_ref, d, q, s[r, :LANE], s[r, LANE:], x_re, x_im)
        starts[blk] = (st_re, st_im)
    return starts, jnp.concatenate([x_re, x_im], axis=1)


def _starts_to_rows(starts, blocks):
    return jnp.concatenate(
        [jnp.concatenate([starts[b][0] for b in blocks], axis=0),
         jnp.concatenate([starts[b][1] for b in blocks], axis=0)], axis=1)


def _p1_project(x_ref, sh_ref, sc_ref, g_ref, wa_ref, h_ref, u_ref, ut_ref, rows):
    gain = g_ref[...] * (1.0 + sc_ref[0])
    sh = sh_ref[0]
    wa = wa_ref[...].astype(_BF16)
    block_chunks = NORM_ROWS // CHUNK

    def norm_block(i):
        r = slice(i * NORM_ROWS, (i + 1) * NORM_ROWS)
        h_ref[r, :] = _rms_mod(x_ref[0, r, :], gain, sh).astype(_BF16)

    norm_block(0)
    for i in range(rows // NORM_ROWS):
        if (i + 1) * NORM_ROWS < rows:
            norm_block(i + 1)
        u = _dot(h_ref[i * NORM_ROWS:(i + 1) * NORM_ROWS, :], wa)
        pr = slice(i * block_chunks * CHUNK_PITCH, (i + 1) * block_chunks * CHUNK_PITCH)
        for cb in range(COL_BLOCKS):
            u_ref[cb, pr, :] = _to_pitch(u[:, cb * LANE:(cb + 1) * LANE])
    if rows < SCAN_ROWS:
        first = rows // CHUNK * CHUNK_PITCH
        for cb in range(COL_BLOCKS):
            u_ref[cb, first:, :] = jnp.zeros((SCAN_TILE * CHUNK_PITCH - first, LANE), _F32)

    def slab_body(sg, carry):
        r0 = pl.multiple_of(sg * GROUP_CH, GROUP_CH)
        for cb in range(COL_BLOCKS):
            slab = u_ref[cb, pl.ds(sg, SCAN_TILE, stride=CHUNK_PITCH), :]
            ut_ref[cb * 8:(cb + 1) * 8, pl.ds(r0, GROUP_CH), :] = (
                slab.astype(_BF16).T.reshape(8, GROUP_CH, SCAN_TILE))
        return carry

    lax.fori_loop(0, CHUNK, slab_body, 0, unroll=RELAYOUT_UNROLL)


def _pair_states(ut_ref, wst_ref, q):
    uf0 = ut_ref[2 * q].T
    uf1 = ut_ref[2 * q + 1].T
    return uf0, uf1, _dot(uf0, wst_ref[q, :FLAT, :]) + _dot(uf1, wst_ref[q, FLAT:, :])


def _p1_kernel(x_ref, ctx_ref, sh_ref, sc_ref, g_ref, wa_ref, wst_ref, ar_ref, ai_ref,
               uflat_ref, xf_ref, sb_ref, seedb_ref, h_ref, u_ref, ut_ref, c_ref, carry_ref, seedf_ref,
               *, bsz, nt, ctx_rows):
    step = pl.program_id(0)
    half = PAIR_LANES // 2

    @pl.when(step == 0)
    def _context():
        _scan_constants(ar_ref, ai_ref, c_ref, (0, 1))
        _p1_project(ctx_ref, sh_ref, sc_ref, g_ref, wa_ref, h_ref, u_ref, ut_ref, ctx_rows)
        per_seq = ctx_rows // CHUNK // SUBLANES // bsz
        zero = jnp.zeros((1, half), _F32)

        def pair_body(q, carry):
            _, _, s = _pair_states(ut_ref, wst_ref, q)
            for b in range(bsz):
                blocks = list(range(b * per_seq, (b + 1) * per_seq))
                _, seedf_ref[b, q] = _scan_rows(c_ref, 0, q, s[:, :half], zero, blocks)
                _, seedb_ref[b, q] = _scan_rows(c_ref, 1, q, s[:, half:], zero, blocks)
            return carry

        lax.fori_loop(0, PAIRS, pair_body, 0, unroll=RELAYOUT_UNROLL)

    @pl.when(step > 0)
    def _latent():
        tile = step - 1

        @pl.when(tile % nt == 0)
        def _seed():
            carry_ref[...] = seedf_ref[tile // nt]

        _p1_project(x_ref, sh_ref, sc_ref, g_ref, wa_ref, h_ref, u_ref, ut_ref, SCAN_ROWS)
        blocks = list(range(SCAN_TILE // SUBLANES))

        def pair_body(q, carry):
            uf0, uf1, s = _pair_states(ut_ref, wst_ref, q)
            uflat_ref[0, 2 * q] = uf0
            uflat_ref[0, 2 * q + 1] = uf1
            starts, x_out = _scan_rows(c_ref, 0, q, s[:, :half], carry_ref[q], blocks)
            xf_ref[0, q] = _starts_to_rows(starts, blocks).astype(_BF16)
            carry_ref[q] = x_out
            sb_ref[0, q] = s[:, half:]
            return carry

        lax.fori_loop(0, PAIRS, pair_body, 0, unroll=RELAYOUT_UNROLL)


def _pass1(x, ctx_rows, mod3, norm_g, w_in, wst, a16_re, a16_im):
    bsz, n_tok, _ = x.shape
    rows_c = ctx_rows.shape[1]
    assert n_tok % SCAN_ROWS == 0
    assert rows_c % NORM_ROWS == 0 and rows_c <= SCAN_ROWS and rows_c % (bsz * CHUNK * SUBLANES) == 0
    nt = n_tok // SCAN_ROWS
    n_chunks = nt * SCAN_TILE
    half = PAIR_LANES // 2
    tile = lambda s: jnp.maximum(s - 1, 0)
    mod_row = lambda s: jnp.where(s == 0, bsz, tile(s) // nt)
    by_tile = lambda s: (tile(s) // nt, 0, tile(s) % nt, 0)
    return pl.pallas_call(
        functools.partial(_p1_kernel, bsz=bsz, nt=nt, ctx_rows=rows_c),
        grid=(bsz * nt + 1,),
        in_specs=[
            pl.BlockSpec((1, SCAN_ROWS, D_MODEL), lambda s: (tile(s) // nt, tile(s) % nt, 0)),
            _const_spec((1, rows_c, D_MODEL)),
            pl.BlockSpec((1, 1, D_MODEL), lambda s: (mod_row(s), 0, 0)),
            pl.BlockSpec((1, 1, D_MODEL), lambda s: (mod_row(s), 0, 1)),
            _const_spec((1, D_MODEL)),
            _const_spec((None, D_MODEL, D_SSM)),
            _const_spec((PAIRS, 2 * FLAT, PAIR_LANES)),
            _const_spec((PAIRS, 1, PAIR_LANES)),
            _const_spec((PAIRS, 1, PAIR_LANES)),
        ],
        out_specs=[
            pl.BlockSpec((1, GROUPS, SCAN_TILE, FLAT), by_tile),
            pl.BlockSpec((1, PAIRS, SCAN_TILE, half), by_tile),
            pl.BlockSpec((1, PAIRS, SCAN_TILE, half), by_tile),
            pl.BlockSpec((bsz, PAIRS, 1, half), lambda s: (0, 0, 0, 0)),
        ],
        out_shape=[
            jax.ShapeDtypeStruct((bsz, GROUPS, n_chunks, FLAT), _BF16),
            jax.ShapeDtypeStruct((bsz, PAIRS, n_chunks, half), _BF16),
            jax.ShapeDtypeStruct((bsz, PAIRS, n_chunks, half), _F32),
            jax.ShapeDtypeStruct((bsz, PAIRS, 1, half), _F32),
        ],
        scratch_shapes=[
            pltpu.VMEM((SCAN_ROWS, D_MODEL), _BF16),
            pltpu.VMEM((COL_BLOCKS, SCAN_TILE * CHUNK_PITCH, LANE), _F32),
            pltpu.VMEM((GROUPS, FLAT, SCAN_TILE), _BF16),
            pltpu.VMEM((2, PAIRS, 8, SUBLANES, LANE), _F32),
            pltpu.VMEM((PAIRS, 1, half), _F32),
            pltpu.VMEM((bsz, PAIRS, 1, half), _F32),
        ],
        compiler_params=pltpu.CompilerParams(
            dimension_semantics=("arbitrary",), vmem_limit_bytes=VMEM_LIMIT),
        name="s5_chunk_states",
    )(x, ctx_rows, mod3, mod3, norm_g, w_in, wst, a16_re, a16_im)


def _readout_kernel(uflat_ref, xf_ref, sb_ref, seed_ref, ar_ref, ai_ref, mu_ref, mv_ref,
                    w0_ref, w1_ref, w2_ref, w3_ref, w4_ref, y_ref, b0_ref, b1_ref, b2_ref, b3_ref, b4_ref,
                    yt_ref, ys_ref, c_ref, carry_ref):
    for w_ref, b_ref in ((w0_ref, b0_ref), (w1_ref, b1_ref), (w2_ref, b2_ref), (w3_ref, b3_ref), (w4_ref, b4_ref)):
        b_ref[...] = w_ref[...].astype(_BF16)

    @pl.when(_first_step())
    def _constants():
        _scan_constants(ar_ref, ai_ref, c_ref, (1,))

    @pl.when(pl.program_id(1) == 0)
    def _seed():
        carry_ref[...] = seed_ref[0]

    blocks = list(range(SCAN_TILE // SUBLANES))

    def pair_body(q, carry):
        y0 = _dot(uflat_ref[0, 2 * q], mu_ref[2 * q])
        y1 = _dot(uflat_ref[0, 2 * q + 1], mu_ref[2 * q + 1])
        starts, x_out = _scan_rows(c_ref, 1, q, sb_ref[0, q], carry_ref[q], blocks)
        carry_ref[q] = x_out
        xs = jnp.concatenate([xf_ref[0, q], _starts_to_rows(starts, blocks).astype(_BF16)], axis=1)
        yx = _dot(xs, mv_ref[q])
        y = jnp.concatenate([y0, y1], axis=1) + yx
        yt = y.astype(yt_ref.dtype).T
        yt_ref[2 * q] = yt[:FLAT]
        yt_ref[2 * q + 1] = yt[FLAT:]
        return carry

    lax.fori_loop(0, PAIRS, pair_body, 0, unroll=RELAYOUT_UNROLL)

    def slab_body(sg, carry):
        r0 = pl.multiple_of(sg * GROUP_CH, GROUP_CH)
        for cb in range(COL_BLOCKS):
            yt = yt_ref[cb * 8:(cb + 1) * 8, pl.ds(r0, GROUP_CH), :].reshape(LANE, SCAN_TILE)
            ys_ref[cb, pl.ds(sg, SCAN_TILE, stride=CHUNK_PITCH), :] = yt.T.astype(_F32)
        return carry

    lax.fori_loop(0, CHUNK, slab_body, 0, unroll=RELAYOUT_UNROLL)
    for cb in range(COL_BLOCKS):
        y_ref[0, :, cb * LANE:(cb + 1) * LANE] = _from_pitch(ys_ref[cb]).astype(y_ref.dtype)


def _cast_specs(weights, n_steps, step_of):
    in_specs, out_specs, out_shapes = [], [], []
    for w in weights:
        _, rows, cols = w.shape
        r = rows // n_steps
        assert rows % n_steps == 0 and r % (2 * SUBLANES) == 0
        in_specs.append(pl.BlockSpec((None, r, cols), lambda *g: (0, step_of(*g), 0)))
        out_specs.append(pl.BlockSpec((r, cols), lambda *g: (step_of(*g), 0)))
        out_shapes.append(jax.ShapeDtypeStruct((rows, cols), _BF16))
    return in_specs, out_specs, out_shapes


def _readout(uflat, xf, sb, seed_b, a16_re, a16_im, mu, mv, mixer_weights):
    bsz, _, n_chunks, _ = uflat.shape
    nt = n_chunks // SCAN_TILE
    half = PAIR_LANES // 2
    rev = lambda b, t: (b, 0, nt - 1 - t, 0)
    w_in_specs, w_out_specs, w_shapes = _cast_specs(mixer_weights, bsz * nt, lambda b, t: b * nt + t)
    return pl.pallas_call(
        _readout_kernel,
        grid=(bsz, nt),
        in_specs=[
            pl.BlockSpec((1, GROUPS, SCAN_TILE, FLAT), rev),
            pl.BlockSpec((1, PAIRS, SCAN_TILE, half), rev),
            pl.BlockSpec((1, PAIRS, SCAN_TILE, half), rev),
            pl.BlockSpec((1, PAIRS, 1, half), lambda b, t: (b, 0, 0, 0)),
            _const_spec((PAIRS, 1, PAIR_LANES)),
            _const_spec((PAIRS, 1, PAIR_LANES)),
            _const_spec((GROUPS, FLAT, FLAT)),
            _const_spec((PAIRS, PAIR_LANES, 2 * FLAT)),
        ] + w_in_specs,
        out_specs=[pl.BlockSpec((1, SCAN_ROWS, D_SSM), lambda b, t: (b, nt - 1 - t, 0))] + w_out_specs,
        out_shape=[jax.ShapeDtypeStruct((bsz, n_chunks * CHUNK, D_SSM), _BF16)] + w_shapes,
        scratch_shapes=[
            pltpu.VMEM((GROUPS, FLAT, SCAN_TILE), _BF16),
            pltpu.VMEM((COL_BLOCKS, SCAN_TILE * CHUNK_PITCH, LANE), _F32),
            pltpu.VMEM((2, PAIRS, 8, SUBLANES, LANE), _F32),
            pltpu.VMEM((PAIRS, 1, half), _F32),
        ],
        compiler_params=pltpu.CompilerParams(
            dimension_semantics=("arbitrary", "arbitrary"), vmem_limit_bytes=VMEM_LIMIT),
        name="s5_readout",
    )(uflat, xf, sb, seed_b, a16_re, a16_im, mu, mv, *mixer_weights)


def _window_sum(u, w):
    assert w // 2 <= SUBLANES
    rows, lanes = u.shape
    nb = rows // GRID_W
    pad = jnp.zeros((nb, SUBLANES, lanes), _F32)
    z = jnp.concatenate([pad, u.reshape(nb, GRID_W, lanes), pad], axis=1)
    n = nb * (GRID_W + 2 * SUBLANES)
    z = z.reshape(n, lanes)
    acc = z + pltpu.roll(z, 1, 0)
    m = 2
    while m < w:
        acc = pltpu.roll(acc, m // 2, 0) + pltpu.roll(acc, n - m // 2, 0)
        m *= 2
    return acc.reshape(nb, GRID_W + 2 * SUBLANES, lanes)[:, SUBLANES:SUBLANES + GRID_W, :].reshape(rows, lanes)


def _mix_kernel(x_ref, y_ref, sh_ref, sc_ref, gt_ref, g_ref, bglu_ref, pinv_ref, ps_ref, pw_ref,
                wr_ref, wglu_ref, wa_ref, wb_ref, wo_ref, ffn_in_ref, ffn_out_ref,
                o_ref, ffn_in_b_ref, ffn_out_b_ref):
    ffn_in_b_ref[...] = ffn_in_ref[...].astype(_BF16)
    ffn_out_b_ref[...] = ffn_out_ref[...].astype(_BF16)

    x = x_ref[0]
    h = _rms_mod(x, g_ref[...] * (1.0 + sc_ref[0]), sh_ref[0]).astype(_BF16)
    c_pool, c_ga, c_gb = D_SSM, D_SSM + D_POOL, D_SSM + D_POOL + D_MODEL
    windows = range(len(POOL_WINDOWS))
    group = lambda wi: slice(wi * POOL_GROUP_CH, (wi + 1) * POOL_GROUP_CH)

    ub = _dot(h, wr_ref[:, c_pool:c_ga])
    y = jax.nn.gelu(y_ref[0].astype(_F32))
    glu = _dot(y.astype(_BF16), wglu_ref[...])
    wsums = [_window_sum(ub[:, group(wi)], POOL_WINDOWS[wi]) for wi in windows]
    gate_a = _dot(h, wr_ref[:, c_ga:c_gb])
    z = y * jax.nn.sigmoid(glu + bglu_ref[...])
    ya = _dot(z.astype(_BF16), wa_ref[...])
    outs = [_dot((wsums[wi] * pinv_ref[wi] - ub[:, group(wi)]).astype(_BF16), pw_ref[wi].astype(_BF16))
            for wi in windows]
    gate_b = _dot(h, wr_ref[:, c_gb:])
    pb = jnp.concatenate(outs, axis=1) * ps_ref[...]
    yb = _dot(pb.astype(_BF16), wb_ref[...])

    merged = jax.nn.sigmoid(gate_a) * ya + jax.nn.sigmoid(gate_b) * yb
    mixed = _dot(merged.astype(_BF16), wo_ref[...])
    o_ref[0] = x + gt_ref[0] * mixed


def _pool_inverse_counts(rows):
    pos = np.arange(rows) % GRID_W
    invs = []
    for w in POOL_WINDOWS:
        lo = np.clip(pos - w // 2, 0, GRID_W - 1)
        hi = np.clip(pos + w - 1 - w // 2, 0, GRID_W - 1) + 1
        invs.append(np.broadcast_to((1.0 / (hi - lo).astype(np.float32))[:, None], (rows, POOL_GROUP_CH)))
    return np.stack(invs)


def _mixer(x, ypre, mod3, norm_g, b_glu, pool_scale, pool_w, w_in_b, w_glu_b, w_a_b, w_b_b, w_out_b, ffn_weights):
    bsz, n_tok, _ = x.shape
    tm = MIX_ROWS
    nt = n_tok // tm
    pinv = jnp.asarray(_pool_inverse_counts(tm), _F32)
    nw = len(POOL_WINDOWS)
    w_in_specs, w_out_specs, w_shapes = _cast_specs(ffn_weights, bsz * nt, lambda b, t: b * nt + t)
    return pl.pallas_call(
        _mix_kernel,
        grid=(bsz, nt),
        in_specs=[
            pl.BlockSpec((1, tm, D_MODEL), lambda b, t: (b, t, 0)),
            pl.BlockSpec((1, tm, D_SSM), lambda b, t: (b, t, 0)),
            pl.BlockSpec((1, 1, D_MODEL), lambda b, t: (b, 0, 0)),
            pl.BlockSpec((1, 1, D_MODEL), lambda b, t: (b, 0, 1)),
            pl.BlockSpec((1, 1, D_MODEL), lambda b, t: (b, 0, 2)),
            _const_spec((1, D_MODEL)),
            _const_spec((1, D_SSM)),
            _const_spec((nw, tm, POOL_GROUP_CH)),
            _const_spec((1, D_POOL)),
            _const_spec((None, nw, POOL_GROUP_CH, POOL_GROUP_CH)),
            _const_spec((D_MODEL, D_SSM + D_POOL + 2 * D_MODEL)),
            _const_spec((D_SSM, D_SSM)),
            _const_spec((D_SSM, D_MODEL)),
            _const_spec((D_POOL, D_MODEL)),
            _const_spec((D_MODEL, D_MODEL)),
        ] + w_in_specs,
        out_specs=[pl.BlockSpec((1, tm, D_MODEL), lambda b, t: (b, t, 0))] + w_out_specs,
        out_shape=[jax.ShapeDtypeStruct(x.shape, _F32)] + w_shapes,
        compiler_params=pltpu.CompilerParams(
            dimension_semantics=("arbitrary", "arbitrary"), vmem_limit_bytes=VMEM_LIMIT),
        name="token_mixer",
    )(x, ypre, mod3, mod3, mod3, norm_g, b_glu, pinv, pool_scale, pool_w,
      w_in_b, w_glu_b, w_a_b, w_b_b, w_out_b, *ffn_weights)


def _first_step():
    return jnp.logical_and(pl.program_id(0) == 0, pl.program_id(1) == 0)


def _ffn_kernel(x_ref, sh_ref, sc_ref, gt_ref, g2_ref, gf_ref, wg_ref, wu_ref, wo_ref, o_ref):
    x = x_ref[0]
    h = _rms_mod(x, g2_ref[...] * (1.0 + sc_ref[0]), sh_ref[0]).astype(_BF16)
    down = None
    for c0, c1 in FFN_CHUNKS:
        gate = _dot(h, wg_ref[:, c0:c1])
        up = _dot(h, wu_ref[:, c0:c1])
        part = _dot((gate * jax.nn.sigmoid(gate) * up).astype(_BF16), wo_ref[c0:c1, :])
        down = part if down is None else down + part
    y = x + gt_ref[0] * down
    ms = jnp.mean(y * y, axis=-1, keepdims=True)
    o_ref[0] = (y * lax.rsqrt(ms + RMS_EPS)) * gf_ref[...]


def _ffn(x1, mod3, norm2_g, final_g, w_ffn_b, w_down_b):
    bsz, n_tok, _ = x1.shape
    tm = FFN_ROWS
    return pl.pallas_call(
        _ffn_kernel,
        grid=(bsz, n_tok // tm),
        in_specs=[
            pl.BlockSpec((1, tm, D_MODEL), lambda b, t: (b, t, 0)),
            pl.BlockSpec((1, 1, D_MODEL), lambda b, t: (b, 0, 3)),
            pl.BlockSpec((1, 1, D_MODEL), lambda b, t: (b, 0, 4)),
            pl.BlockSpec((1, 1, D_MODEL), lambda b, t: (b, 0, 5)),
            _const_spec((1, D_MODEL)),
            _const_spec((1, D_MODEL)),
            _const_spec((D_MODEL, FFN_HIDDEN), (0, 0)),
            _const_spec((D_MODEL, FFN_HIDDEN), (0, 1)),
            _const_spec((FFN_HIDDEN, D_MODEL)),
        ],
        out_specs=pl.BlockSpec((1, tm, D_MODEL), lambda b, t: (b, t, 0)),
        out_shape=jax.ShapeDtypeStruct(x1.shape, _F32),
        compiler_params=pltpu.CompilerParams(
            dimension_semantics=("arbitrary", "arbitrary"), vmem_limit_bytes=VMEM_LIMIT),
        name="swiglu_ffn",
    )(x1, mod3, mod3, mod3, norm2_g, final_g, w_ffn_b, w_ffn_b, w_down_b)


def _cmul(xr, xi, yr, yi):
    return xr * yr - xi * yi, xr * yi + xi * yr


def _split_bf16(v):
    hi = v.astype(_BF16)
    return hi, (v - hi.astype(_F32)).astype(_BF16)


def _dot3_nt(a, b):
    dims = (((1,), (1,)), ((), ()))
    nt = lambda p, q: lax.dot_general(p, q, dims, preferred_element_type=_F32)
    a_hi, a_lo = _split_bf16(a)
    b_hi, b_lo = _split_bf16(b)
    return nt(a_hi, b_hi) + (nt(a_hi, b_lo) + nt(a_lo, b_hi))


def _tables_kernel(a_ref, bt_ref, c_ref, d_ref, ct_ref, bmod_ref, wmod_hbm,
                   mu_ref, wst_ref, mv_ref, a16r_ref, a16i_ref, mod_ref,
                   kt_ref, khl_ref, vnat_ref, cp_ref, act_ref, wstage_ref, sem, *, n_rows):
    n_slots = wstage_ref.shape[0]
    mod_blk = wstage_ref.shape[2]
    ct = ct_ref[...]
    act_ref[...] = ct * jax.nn.sigmoid(ct)

    def mod_copy(i):
        c0 = pl.multiple_of(i * mod_blk, LANE)
        slot = i % n_slots
        return pltpu.make_async_copy(wmod_hbm.at[0, :, pl.ds(c0, mod_blk)], wstage_ref.at[slot], sem.at[slot])

    for i in range(n_slots - 1):
        mod_copy(i).start()

    def pair_body(q, carry):
        mod_copy(q).wait()

        @pl.when(q + n_slots - 1 < PAIRS)
        def _refill():
            mod_copy(q + n_slots - 1).start()

        w = wstage_ref[q % n_slots]
        for r in range(n_rows):
            mod_ref[r, q] = jnp.sum(w * act_ref[:, r:r + 1], axis=0, keepdims=True) + bmod_ref[q]
        for r in range(n_rows, mod_ref.shape[0]):
            mod_ref[r, q] = jnp.zeros((1, mod_blk), _F32)

        for d in range(2):
            ar = a_ref[0, d, q]
            ai = a_ref[1, d, q]
            dt = jnp.exp(a_ref[2, d, q])
            mag = jnp.exp(ar * dt)
            ang = ai * dt
            abr, abi = mag * jnp.cos(ang), mag * jnp.sin(ang)
            den = ar * ar + ai * ai
            fr = ((abr - 1.0) * ar + abi * ai) / den
            fi = (abi * ar - (abr - 1.0) * ai) / den
            bbr, bbi = _cmul(bt_ref[0, d, q], bt_ref[1, d, q], fr, fi)
            cr = c_ref[0, d, q]
            ci = c_ref[1, d, q]
            pw = [(jnp.ones_like(ar), jnp.zeros_like(ar))]
            for _ in range(CHUNK):
                pw.append(_cmul(pw[-1][0], pw[-1][1], abr, abi))
            lanes_re = slice(2 * d * LANE, (2 * d + 1) * LANE)
            lanes_im = slice((2 * d + 1) * LANE, (2 * d + 2) * LANE)
            for lanes in (lanes_re, lanes_im):
                a16r_ref[q, :, lanes] = pw[CHUNK][0]
                a16i_ref[q, :, lanes] = pw[CHUNK][1]
            for sg in range(CHUNK):
                e = (CHUNK - 1 - sg) if d == 0 else sg
                wr, wi = _cmul(bbr, bbi, pw[e][0], pw[e][1])
                e = (sg + 1) if d == 0 else (CHUNK - sg)
                vr, vi = _cmul(cr, ci, pw[e][0], pw[e][1])
                for gg in range(2):
                    src = slice(gg * GROUP_CH, (gg + 1) * GROUP_CH)
                    dst = slice(gg * FLAT + sg * GROUP_CH, gg * FLAT + (sg + 1) * GROUP_CH)
                    wst_ref[q, dst, lanes_re] = wr[src].astype(_BF16)
                    wst_ref[q, dst, lanes_im] = wi[src].astype(_BF16)
                    vnat_ref[2 * d, dst, :] = vr[src]
                    vnat_ref[2 * d + 1, dst, :] = -vi[src]
            for gg in range(2):
                src = slice(gg * GROUP_CH, (gg + 1) * GROUP_CH)
                for k in range(CHUNK):
                    e = k if d == 0 else (CHUNK - 1 - k)
                    pr, pi = _cmul(cr[src], ci[src], pw[e][0], pw[e][1])
                    cp_ref[0, k * GROUP_CH:(k + 1) * GROUP_CH, :] = pr
                    cp_ref[1, k * GROUP_CH:(k + 1) * GROUP_CH, :] = pi
                kt = _dot3_nt(bbr[src], cp_ref[0]) - _dot3_nt(bbi[src], cp_ref[1])
                r0 = pl.multiple_of((2 * q + gg) * GROUP_CH, GROUP_CH)
                kt_ref[d, pl.ds(r0, GROUP_CH), :] = kt
        for part in range(4):
            mv_ref[q, part * LANE:(part + 1) * LANE, :] = vnat_ref[part].T.astype(_BF16)
        return carry

    lax.fori_loop(0, PAIRS, pair_body, 0, unroll=2)

    for d in range(2):
        khl_ref[2 * d], khl_ref[2 * d + 1] = _split_bf16(kt_ref[d])

    row = lax.broadcasted_iota(jnp.int32, (FLAT, FLAT), 0)
    col = lax.broadcasted_iota(jnp.int32, (FLAT, FLAT), 1)
    same_ch = (row % GROUP_CH) == (col % GROUP_CH)
    row_blk = row // GROUP_CH
    col_blk = col // GROUP_CH
    orow = lax.broadcasted_iota(jnp.int32, (GROUPS * GROUP_CH, FLAT), 0)
    ocol = lax.broadcasted_iota(jnp.int32, (GROUPS * GROUP_CH, FLAT), 1)
    skip_ch = (orow % GROUP_CH) == (ocol % GROUP_CH)
    ocol_blk = ocol // GROUP_CH
    d_col = d_ref[...]

    def toeplitz_body(sg, carry):
        sf = jnp.where(same_ch & (row_blk + sg == col_blk), 1.0, 0.0).astype(_BF16)
        sb = jnp.where(same_ch & (row_blk == col_blk + (CHUNK - 1) - sg), 1.0, 0.0).astype(_BF16)
        out = (_dot(khl_ref[0], sf) + _dot(khl_ref[1], sf)) + (_dot(khl_ref[2], sb) + _dot(khl_ref[3], sb))
        out = out + jnp.where(skip_ch & (ocol_blk == sg), d_col, 0.0)
        r0 = pl.multiple_of(sg * GROUP_CH, GROUP_CH)
        mu_ref[:, pl.ds(r0, GROUP_CH), :] = out.reshape(GROUPS, GROUP_CH, FLAT).astype(_BF16)
        return carry

    lax.fori_loop(0, CHUNK, toeplitz_body, 0, unroll=4)


def _s5_tables(a_re, a_im, log_dt, b_re, b_im, c_re, c_im, d_skip, cc_t, w_mod, b_mod, n_rows):
    f32 = _F32
    n_mod = w_mod.shape[-1]
    mod_blk = n_mod // PAIRS
    assert n_mod % PAIRS == 0 and mod_blk % LANE == 0
    eye2 = jnp.eye(2, dtype=f32)

    def pair_blocks(re, im):
        v = jnp.stack([re, im]).astype(f32).reshape(2, 2, PAIRS, 2, GROUP_CH, 1, STATE)
        v = v * eye2[None, None, None, :, None, :, None]
        return v.reshape(2, 2, PAIRS, 2 * GROUP_CH, 2 * STATE)

    ldt = jnp.broadcast_to(log_dt.astype(f32)[..., None], (2, GROUPS, STATE))
    a_rows = jnp.stack([a_re.astype(f32), a_im.astype(f32), ldt]).reshape(3, 2, PAIRS, 1, 2 * STATE)
    args = (a_rows, pair_blocks(jnp.swapaxes(b_re, 2, 3), jnp.swapaxes(b_im, 2, 3)),
            pair_blocks(c_re, c_im), d_skip.astype(f32).reshape(D_SSM, 1),
            cc_t, b_mod.reshape(PAIRS, 1, mod_blk))
    whole = lambda a: pl.BlockSpec(a.shape, lambda i, n=a.ndim: (0,) * n)
    out_shape = [
        jax.ShapeDtypeStruct((GROUPS, FLAT, FLAT), _BF16),
        jax.ShapeDtypeStruct((PAIRS, 2 * FLAT, PAIR_LANES), _BF16),
        jax.ShapeDtypeStruct((PAIRS, PAIR_LANES, 2 * FLAT), _BF16),
        jax.ShapeDtypeStruct((PAIRS, 1, PAIR_LANES), _F32),
        jax.ShapeDtypeStruct((PAIRS, 1, PAIR_LANES), _F32),
        jax.ShapeDtypeStruct((SUBLANES, PAIRS, 1, mod_blk), _F32),
    ]
    *tables, mod = pl.pallas_call(
        functools.partial(_tables_kernel, n_rows=n_rows),
        grid=(1,),
        in_specs=[whole(a) for a in args] + [pl.BlockSpec(memory_space=pl.ANY)],
        out_specs=[whole(s) for s in out_shape],
        out_shape=out_shape,
        scratch_shapes=[
            pltpu.VMEM((2, GROUPS * GROUP_CH, FLAT), _F32),
            pltpu.VMEM((4, GROUPS * GROUP_CH, FLAT), _BF16),
            pltpu.VMEM((4, 2 * FLAT, LANE), _F32),
            pltpu.VMEM((2, FLAT, LANE), _F32),
            pltpu.VMEM((D_MODEL, SUBLANES), _F32),
            pltpu.VMEM((WEIGHT_STAGE_SLOTS, D_MODEL, mod_blk), _F32),
            pltpu.SemaphoreType.DMA((WEIGHT_STAGE_SLOTS,)),
        ],
        compiler_params=pltpu.CompilerParams(
            dimension_semantics=("arbitrary",), vmem_limit_bytes=VMEM_LIMIT),
        name="s5_tables_adaln",
    )(*args, w_mod)
    return tables, mod.reshape(SUBLANES, 1, n_mod)


def kernel(x, c, ctx, c_ctx, w_mod, b_mod, norm1_g, norm2_g, w_in, s5_a_re, s5_a_im, s5_log_dt,
           s5_b_re, s5_b_im, s5_c_re, s5_c_im, s5_d, w_glu, b_glu, pool_w, pool_scale,
           w_branch_a, w_branch_b, w_out, w_ffn_in, w_ffn_out, final_norm_g):
    bsz, n_tok, d = x.shape
    ctx_len = ctx.shape[1]
    assert d == D_MODEL and w_mod.shape[0] == 1 and bsz + 1 <= SUBLANES
    assert n_tok % SCAN_ROWS == 0 and n_tok % MIX_ROWS == 0 and MIX_ROWS % GRID_W == 0
    assert bsz * ctx_len <= SCAN_ROWS

    cc_t = jnp.concatenate(
        [c.T, c_ctx[:, None], jnp.zeros((D_MODEL, SUBLANES - bsz - 1), _F32)], axis=1)
    (mu, wst, mv, a16_re, a16_im), mod3 = _s5_tables(
        s5_a_re[0], s5_a_im[0], s5_log_dt[0], s5_b_re[0], s5_b_im[0], s5_c_re[0], s5_c_im[0], s5_d[0],
        cc_t, w_mod, b_mod, bsz + 1)

    uflat, x_fwd, s_bwd, seed_b = _pass1(x, ctx.reshape(1, bsz * ctx_len, D_MODEL), mod3, norm1_g, w_in, wst,
                                         a16_re, a16_im)
    ypre, *mixer_w = _readout(uflat, x_fwd, s_bwd, seed_b, a16_re, a16_im, mu, mv,
                              (w_in, w_glu, w_branch_a, w_branch_b, w_out))
    x1, w_ffn_b, w_down_b = _mixer(x, ypre, mod3, norm1_g, b_glu, pool_scale, pool_w, *mixer_w,
                                   (w_ffn_in, w_ffn_out))
    return _ffn(x1, mod3, norm2_g, final_norm_g.reshape(1, D_MODEL), w_ffn_b, w_down_b)
```
